```python
import math
import jax
import jax.numpy as jnp
from jax import lax
import numpy as np

D_MODEL = 1024
BATCH = 16
SEQ = 256
DEPTH = 4
DEC_BATCH = 8
DEC_SEQ = 2048
PAST_LEN = 256

GRID_W = 64
N_EVEN = (DEPTH + 1) // 2
N_ODD = DEPTH // 2
ATT_WIDTH = D_MODEL // 2
POOL_WIDTH = D_MODEL - ATT_WIDTH
ATT_HEADS = 4
ATT_DH = ATT_WIDTH // (2 * ATT_HEADS)
ATT_DV = 2 * ATT_DH
Q_BLOCK = 128
ROPE_BASE = 10000.0
POOL_WINDOWS = (2, 4, 8, 16)
POOL_GROUP = POOL_WIDTH // len(POOL_WINDOWS)
GLA_HEADS = 4
GLA_DK_TOTAL = D_MODEL // 2
GLA_DV_TOTAL = D_MODEL
GLA_DK = GLA_DK_TOTAL // GLA_HEADS
GLA_DV = GLA_DV_TOTAL // GLA_HEADS
GATE_RANK = 16
GATE_TEMP = 16.0
CHUNK = 64
ODD_IN = 2 * GLA_DK_TOTAL + 2 * GLA_DV_TOTAL + 2 * GATE_RANK
FFN_HIDDEN = ((8 * D_MODEL + 3 * 256 - 1) // (3 * 256)) * 256
EPS = 1e-6

kernel_name = "hybrid_diffattn_pool_gla_prefix_dit_step"

F32 = jnp.float32


def rmsnorm(x, g):
    xf = x.astype(F32)
    y = xf * lax.rsqrt(jnp.mean(xf * xf, axis=-1, keepdims=True) + EPS)
    return (y * g.astype(F32)).astype(x.dtype)


def adaln(cvec, w, b):
    mod = jax.nn.silu(cvec) @ w + b
    return jnp.split(mod, 6, axis=-1)


def modulate(h, shift, scale):
    return h * (1 + scale) + shift


def axial_rope(x, rows, cols):
    half = ATT_DH // 2
    inv = 1.0 / (ROPE_BASE ** (jnp.arange(0, half, 2, dtype=F32) / half))

    def rot(xh, pos):
        ang = pos[:, None] * inv[None, :]
        ang = jnp.concatenate([ang, ang], axis=-1)
        cos = jnp.cos(ang)[None, :, None, None, :].astype(x.dtype)
        sin = jnp.sin(ang)[None, :, None, None, :].astype(x.dtype)
        x1, x2 = jnp.split(xh, 2, axis=-1)
        return xh * cos + jnp.concatenate([-x2, x1], axis=-1) * sin

    return jnp.concatenate([rot(x[..., :half], rows), rot(x[..., half:], cols)], axis=-1)


def diff_lambda(lam_p, layer):
    lp = lam_p.astype(F32)
    lam_init = 0.8 - 0.6 * math.exp(-0.3 * layer)
    lam = jnp.exp(jnp.sum(lp[0] * lp[1])) - jnp.exp(jnp.sum(lp[2] * lp[3])) + lam_init
    return lam, lam_init


def diff_attention(q, k, v, lam):
    B, Tq = q.shape[0], q.shape[1]
    nb = Tq // Q_BLOCK
    qb = q.reshape(B, nb, Q_BLOCK, ATT_HEADS, 2, ATT_DH).transpose(1, 0, 2, 3, 4, 5)
    scale = ATT_DH ** -0.5

    def one_block(qblk):
        s = jnp.einsum('bqhad,bkhad->bhaqk', qblk, k).astype(F32) * scale
        p = jax.nn.softmax(s, axis=-1)
        a = p[:, :, 0] - lam * p[:, :, 1]
        return jnp.einsum('bhqk,bkhv->bqhv', a.astype(v.dtype), v)

    o = lax.map(one_block, qb)
    return o.transpose(1, 0, 2, 3, 4).reshape(B, Tq, ATT_HEADS, ATT_DV)


def multiscale_pool(u, w_pool, pool_scale):
    B, T, C = u.shape
    cs = jnp.concatenate([jnp.zeros((B, 1, C), F32), jnp.cumsum(u.astype(F32), axis=1)], axis=1)
    t = jnp.arange(T)
    outs = []
    for g, w in enumerate(POOL_WINDOWS):
        lo = jnp.clip(t - w // 2, 0, T)
        hi = jnp.clip(t - w // 2 + w, 0, T)
        csg = cs[..., g * POOL_GROUP:(g + 1) * POOL_GROUP]
        cnt = (hi - lo).astype(F32)[None, :, None]
        ug = u[..., g * POOL_GROUP:(g + 1) * POOL_GROUP].astype(F32)
        y = (csg[:, hi] - csg[:, lo]) / cnt - ug
        outs.append(jnp.einsum('btc,cd->btd', y.astype(u.dtype), w_pool[g]))
    return jnp.concatenate(outs, axis=-1) * pool_scale


def even_project(h, w_in):
    B, T, _ = h.shape
    q, k, v, u = jnp.split(h @ w_in, [ATT_WIDTH, 2 * ATT_WIDTH, 3 * ATT_WIDTH], axis=-1)
    q = q.reshape(B, T, ATT_HEADS, 2, ATT_DH)
    k = k.reshape(B, T, ATT_HEADS, 2, ATT_DH)
    v = v.reshape(B, T, ATT_HEADS, ATT_DV)
    return q, k, v, u


def even_output(o, u, lam_init, subln_g, w_pool, pool_scale, w_out):
    B, T = o.shape[0], o.shape[1]
    o = rmsnorm(o, subln_g.reshape(ATT_HEADS, ATT_DV)) * (1.0 - lam_init)
    p = multiscale_pool(u, w_pool, pool_scale)
    return jnp.concatenate([o.reshape(B, T, ATT_WIDTH).astype(p.dtype), p], axis=-1) @ w_out


def gla_chunked(q, k, v, g, s0):
    B, T = q.shape[0], q.shape[1]
    N = T // CHUNK

    def r(a):
        return a.reshape(B, N, CHUNK, GLA_HEADS, a.shape[-1])

    q, k, v, g = r(q), r(k), r(v), r(g)
    b = jnp.cumsum(g, axis=2)
    b_last = b[:, :, -1]
    q_t = q * jnp.exp(b)
    k_t = k * jnp.exp(-b)
    k_end = k * jnp.exp(b_last[:, :, None] - b)
    mask = jnp.tril(jnp.ones((CHUNK, CHUNK), dtype=bool))
    A = jnp.where(mask, jnp.einsum('bnihd,bnjhd->bnhij', q_t, k_t), 0.0)
    o_intra = jnp.einsum('bnhij,bnjhv->bnihv', A, v)
    kv_chunk = jnp.einsum('bnjhd,bnjhv->bnhdv', k_end, v)

    def step(S, xs):
        qc, decay, kv = xs
        o = jnp.einsum('bihd,bhdv->bihv', qc, S)
        return S * decay[..., None] + kv, o

    S, o_inter = lax.scan(step, s0, (q_t.transpose(1, 0, 2, 3, 4),
                                     jnp.exp(b_last).transpose(1, 0, 2, 3),
                                     kv_chunk.transpose(1, 0, 2, 3, 4)))
    o = o_intra + o_inter.transpose(1, 0, 2, 3, 4)
    return o.reshape(B, T, GLA_HEADS, GLA_DV), S


def gla_project(h, w_in, w_gate_up, b_gate):
    B, T, _ = h.shape
    q, k, v, r, gd = jnp.split(h @ w_in, [GLA_DK_TOTAL, 2 * GLA_DK_TOTAL,
                                         2 * GLA_DK_TOTAL + GLA_DV_TOTAL,
                                         2 * GLA_DK_TOTAL + 2 * GLA_DV_TOTAL], axis=-1)
    q = q.reshape(B, T, GLA_HEADS, GLA_DK).astype(F32) * (GLA_DK ** -0.5)
    k = k.reshape(B, T, GLA_HEADS, GLA_DK).astype(F32)
    v = v.reshape(B, T, GLA_HEADS, GLA_DV).astype(F32)
    gates = []
    for d in range(2):
        logit = gd[..., d * GATE_RANK:(d + 1) * GATE_RANK] @ w_gate_up[d] + b_gate[d]
        gates.append((jax.nn.log_sigmoid(logit.astype(F32)) / GATE_TEMP).reshape(B, T, GLA_HEADS, GLA_DK))
    return q, k, v, r, gates


def gla_bidir(q, k, v, gates, s_f0, s_b0):
    o_f, s_f = gla_chunked(q, k, v, gates[0], s_f0)
    fl = lambda a: jnp.flip(a, axis=1)
    o_b, s_b = gla_chunked(fl(q), fl(k), fl(v), fl(gates[1]), s_b0)
    return o_f + fl(o_b), jnp.stack([s_f, s_b], axis=1)


def gla_output(o, r, norm_g, w_out):
    B, T = o.shape[0], o.shape[1]
    o = rmsnorm(o, norm_g.reshape(GLA_HEADS, GLA_DV)).reshape(B, T, GLA_DV_TOTAL).astype(r.dtype)
    return (o * jax.nn.silu(r)) @ w_out


def swiglu(h, w_in, w_out):
    a, b = jnp.split(h @ w_in, 2, axis=-1)
    return (jax.nn.silu(a) * b) @ w_out


def setup_inputs(seed: int = 0) -> dict:
    key = jax.random.key(seed)
    ks = jax.random.split(key, 32)

    def nrm(k, shape, scale):
        return jax.random.normal(k, shape, F32) * scale

    D = D_MODEL
    return {
        "x_prompt": nrm(ks[0], (BATCH, SEQ, D), 1.0),
        "x_sample": nrm(ks[1], (DEC_BATCH, DEC_SEQ, D), 1.0),
        "c": nrm(ks[2], (DEC_BATCH, D), 1.0),
        "cache_attn_k": nrm(ks[3], (DEC_BATCH, N_EVEN, PAST_LEN, ATT_HEADS, 2, ATT_DH), 1.0),
        "cache_attn_v": nrm(ks[4], (DEC_BATCH, N_EVEN, PAST_LEN, ATT_HEADS, ATT_DV), 1.0),
        "state_gla": nrm(ks[5], (DEC_BATCH, N_ODD, 2, GLA_HEADS, GLA_DK, GLA_DV), 0.5),
        "c_ctx": nrm(ks[6], (D,), 1.0),
        "norm1_g": 1.0 + nrm(ks[7], (DEPTH, D), 0.05),
        "norm2_g": 1.0 + nrm(ks[8], (DEPTH, D), 0.05),
        "w_mod": nrm(ks[9], (DEPTH, D, 6 * D), 0.5 * D ** -0.5),
        "b_mod": nrm(ks[10], (DEPTH, 6 * D), 0.02),
        "w_in_even": nrm(ks[11], (N_EVEN, D, 3 * ATT_WIDTH + POOL_WIDTH), D ** -0.5),
        "lam_params": nrm(ks[12], (N_EVEN, 4, ATT_DH), 0.1),
        "subln_g": 1.0 + nrm(ks[13], (N_EVEN, ATT_WIDTH), 0.05),
        "w_pool": nrm(ks[14], (N_EVEN, len(POOL_WINDOWS), POOL_GROUP, POOL_GROUP), POOL_GROUP ** -0.5),
        "pool_scale": 1.0 + nrm(ks[15], (N_EVEN, POOL_WIDTH), 0.1),
        "w_out_even": nrm(ks[16], (N_EVEN, D, D), D ** -0.5),
        "w_in_odd": nrm(ks[17], (N_ODD, D, ODD_IN), D ** -0.5),
        "w_gate_up": nrm(ks[18], (N_ODD, 2, GATE_RANK, GLA_DK_TOTAL), GATE_RANK ** -0.5),
        "b_gate": nrm(ks[19], (N_ODD, 2, GLA_DK_TOTAL), 0.1),
        "gla_norm_g": 1.0 + nrm(ks[20], (N_ODD, GLA_DV_TOTAL), 0.05),
        "w_out_odd": nrm(ks[21], (N_ODD, GLA_DV_TOTAL, D), GLA_DV_TOTAL ** -0.5),
        "w_ffn_in": nrm(ks[22], (DEPTH, D, 2 * FFN_HIDDEN), D ** -0.5),
        "w_ffn_out": nrm(ks[23], (DEPTH, FFN_HIDDEN, D), FFN_HIDDEN ** -0.5),
        "final_g": 1.0 + nrm(ks[24], (D,), 0.05),
    }


def reference(x_prompt, x_sample, c, cache_attn_k, cache_attn_v, state_gla, c_ctx,
              norm1_g, norm2_g, w_mod, b_mod, w_in_even, lam_params, subln_g, w_pool,
              pool_scale, w_out_even, w_in_odd, w_gate_up, b_gate, gla_norm_g, w_out_odd,
              w_ffn_in, w_ffn_out, final_g):
    xp, xs = x_prompt, x_sample
    Bp = xp.shape[0]
    n_lat = xs.shape[1]
    n_rows = n_lat // GRID_W
    rows = jnp.repeat(jnp.arange(n_rows, dtype=F32), GRID_W)
    cols = jnp.tile(jnp.arange(GRID_W, dtype=F32), n_rows)
    ks_new, vs_new, ss_new = [], [], []
    for l in range(DEPTH):
        m_ctx = adaln(c_ctx, w_mod[l], b_mod[l])
        m_lat = [m[:, None, :] for m in adaln(c, w_mod[l], b_mod[l])]
        hp = modulate(rmsnorm(xp, norm1_g[l]), m_ctx[0], m_ctx[1])
        hs = modulate(rmsnorm(xs, norm1_g[l]), m_lat[0], m_lat[1])
        i = l // 2
        if l % 2 == 0:
            lam, lam_init = diff_lambda(lam_params[i], l)
            q, k, v, u = even_project(hp, w_in_even[i])
            o = diff_attention(q, k, v, lam)
            out_p = even_output(o, u, lam_init, subln_g[i], w_pool[i], pool_scale[i], w_out_even[i])
            ks_new.append(k)
            vs_new.append(v)
            q, k, v, u = even_project(hs, w_in_even[i])
            q = axial_rope(q, rows, cols)
            k = axial_rope(k, rows, cols)
            k_all = jnp.concatenate([k, cache_attn_k[:, i].astype(k.dtype)], axis=1)
            v_all = jnp.concatenate([v, cache_attn_v[:, i].astype(v.dtype)], axis=1)
            o = diff_attention(q, k_all, v_all, lam)
            out_s = even_output(o, u, lam_init, subln_g[i], w_pool[i], pool_scale[i], w_out_even[i])
        else:
            q, k, v, r, gates = gla_project(hp, w_in_odd[i], w_gate_up[i], b_gate[i])
            z0 = jnp.zeros((Bp, GLA_HEADS, GLA_DK, GLA_DV), F32)
            o, s_ctx = gla_bidir(q, k, v, gates, z0, z0)
            out_p = gla_output(o, r, gla_norm_g[i], w_out_odd[i])
            ss_new.append(s_ctx)
            q, k, v, r, gates = gla_project(hs, w_in_odd[i], w_gate_up[i], b_gate[i])
            st = state_gla[:, i].astype(F32)
            o, _ = gla_bidir(q, k, v, gates, st[:, 0], st[:, 1])
            out_s = gla_output(o, r, gla_norm_g[i], w_out_odd[i])
        xp = xp + m_ctx[2] * out_p
        xs = xs + m_lat[2] * out_s
        hp = modulate(rmsnorm(xp, norm2_g[l]), m_ctx[3], m_ctx[4])
        hs = modulate(rmsnorm(xs, norm2_g[l]), m_lat[3], m_lat[4])
        xp = xp + m_ctx[5] * swiglu(hp, w_ffn_in[l], w_ffn_out[l])
        xs = xs + m_lat[5] * swiglu(hs, w_ffn_in[l], w_ffn_out[l])
    y_prompt = rmsnorm(xp, final_g)
    y_sample = rmsnorm(xs, final_g)
    new_attn_k = jnp.stack(ks_new, axis=1)
    new_attn_v = jnp.stack(vs_new, axis=1)
    new_gla_state = jnp.stack(ss_new, axis=1)
    return (y_prompt, y_sample, new_attn_k, new_attn_v, new_gla_state)
```

```python
import functools
import math

import jax
import jax.numpy as jnp
from jax import lax
from jax.experimental import pallas as pl
from jax.experimental.pallas import tpu as pltpu

F32 = jnp.float32
BF16 = jnp.bfloat16

D_MODEL = 1024
DEPTH = 4
GRID_W = 64
ATT_WIDTH = 512
POOL_WIDTH = 512
ATT_HEADS = 4
ATT_DH = 64
ATT_DV = 128
ROPE_BASE = 10000.0
POOL_WINDOWS = (2, 4, 8, 16)
POOL_GROUP = 128
POOL_PAD = 8
GLA_HEADS = 4
GLA_DK = 128
GLA_DV = 256
GATE_RANK = 16
GATE_TEMP = 16.0
CHUNK = 64
FFN_HIDDEN = 2816
FFN_CHUNK = 256
EPS = 1e-6
MOD_ROWS = 16

V7X_VMEM_LIMIT = 56 * 1024 * 1024


def _params(n_axes, vmem_bytes=V7X_VMEM_LIMIT):
    return pltpu.CompilerParams(dimension_semantics=("arbitrary",) * n_axes,
                                vmem_limit_bytes=vmem_bytes)


def _resident(shape):
    nd = len(shape)
    return pl.BlockSpec(shape, lambda *_: (0,) * nd, pipeline_mode=pl.Buffered(1))


def _norm_mod(x, g, shift, scale):
    y = x * lax.rsqrt(jnp.mean(x * x, axis=-1, keepdims=True) + EPS)
    return (y * g) * (1.0 + scale) + shift


def _silu(x):
    return x * jax.nn.sigmoid(x)


def _adaln_kernel(cv_ref, w_ref, b_ref, o_ref):
    s = _silu(cv_ref[...]).astype(BF16)
    o_ref[...] = jnp.dot(s, w_ref[...].astype(BF16), preferred_element_type=F32) + b_ref[...]


def _adaln(cvec, w_mod, b_mod):
    tn = 1536
    out = pl.pallas_call(
        _adaln_kernel,
        grid=(DEPTH, 6 * D_MODEL // tn),
        in_specs=[pl.BlockSpec((MOD_ROWS, D_MODEL), lambda l, j: (0, 0)),
                  pl.BlockSpec((None, D_MODEL, tn), lambda l, j: (l, 0, j)),
                  pl.BlockSpec((None, 1, tn), lambda l, j: (l, 0, j))],
        out_specs=pl.BlockSpec((None, MOD_ROWS, tn), lambda l, j: (l, 0, j)),
        out_shape=jax.ShapeDtypeStruct((DEPTH, MOD_ROWS, 6 * D_MODEL), F32),
        compiler_params=_params(2),
        name="adaln",
    )(cvec, w_mod, b_mod.reshape(DEPTH, 1, 6 * D_MODEL))
    return out.reshape(DEPTH, MOD_ROWS, 6, D_MODEL)


def _mod_spec(layer, group_of_step):
    return pl.BlockSpec((None, None, 6, D_MODEL),
                        lambda *ids: (layer, group_of_step(*ids), 0, 0))


def _even_in_kernel(*refs, tm, rc, rope):
    if rope:
        x_ref, mod_ref, g_ref, w_ref, cos_ref, sin_ref, q_ref, k_ref, v_ref, u_ref = refs
    else:
        x_ref, mod_ref, g_ref, w_ref, q_ref, k_ref, v_ref, u_ref = refs
    shift, scale, g = mod_ref[0:1, :], mod_ref[1:2, :], g_ref[...]
    lo = (lax.broadcasted_iota(jnp.int32, (rc, 128), 1) % 32) < 16

    def rotary(y, cos, sin):
        parts = []
        for h in range(ATT_HEADS):
            yh = y[:, h * 128:(h + 1) * 128]
            fwd = pltpu.roll(yh, 128 - 16, axis=1)
            bwd = pltpu.roll(yh, 16, axis=1)
            parts.append(yh * cos + jnp.where(lo, fwd, bwd) * sin)
        return jnp.concatenate(parts, axis=1)

    def body(r, carry):
        rows = pl.ds(pl.multiple_of(r * rc, rc), rc)
        h = _norm_mod(x_ref[rows, :], g, shift, scale).astype(BF16)
        q = jnp.dot(h, w_ref[:, 0:512], preferred_element_type=F32)
        k = jnp.dot(h, w_ref[:, 512:1024], preferred_element_type=F32)
        if rope:
            cos, sin = cos_ref[rows, :], sin_ref[rows, :]
            q, k = rotary(q, cos, sin), rotary(k, cos, sin)
        q_ref[rows, :] = (q * (ATT_DH ** -0.5)).astype(q_ref.dtype)
        k_ref[rows, :] = k.astype(k_ref.dtype)
        v_ref[rows, :] = jnp.dot(h, w_ref[:, 1024:1536], preferred_element_type=F32).astype(v_ref.dtype)
        u_ref[rows, :] = jnp.dot(h, w_ref[:, 1536:2048], preferred_element_type=F32)
        return carry

    lax.fori_loop(0, tm // rc, body, 0)


def _even_in(x, mod, layer, group_of_step, g1, w_bf, rope_tabs, seq_len, kv_dtype):
    n = x.shape[0]
    tm, rc = 1024, 256
    row = lambda i: (i, 0)
    in_specs = [pl.BlockSpec((tm, D_MODEL), row), _mod_spec(layer, group_of_step),
                _resident((1, D_MODEL)), _resident((D_MODEL, 2048))]
    args = [x, mod, g1, w_bf]
    if rope_tabs is not None:
        per_seq = seq_len // tm
        pos = lambda i: (i % per_seq, 0)
        in_specs += [pl.BlockSpec((tm, 128), pos), pl.BlockSpec((tm, 128), pos)]
        args += list(rope_tabs)
    out_dt = (BF16, kv_dtype, kv_dtype, F32)
    return pl.pallas_call(
        functools.partial(_even_in_kernel, tm=tm, rc=rc, rope=rope_tabs is not None),
        grid=(n // tm,),
        in_specs=in_specs,
        out_specs=[pl.BlockSpec((tm, 512), row)] * 4,
        out_shape=[jax.ShapeDtypeStruct((n, 512), dt) for dt in out_dt],
        compiler_params=_params(1),
        name="even_in",
    )(*args)


def _even_mix_kernel(*refs, seq, tq, n_cache, lam_init):
    if n_cache:
        (x_ref, q_ref, k_ref, v_ref, u_ref, ck_ref, cv_ref, mod_ref, lamp_ref, subg_ref,
         wpool_ref, pscale_ref, wout_ref, o_ref, upad, ypool) = refs
    else:
        (x_ref, q_ref, k_ref, v_ref, u_ref, mod_ref, lamp_ref, subg_ref,
         wpool_ref, pscale_ref, wout_ref, o_ref, upad, ypool) = refs
    qi = pl.program_id(1)

    @pl.when(qi == 0)
    def _pool():
        zeros = jnp.zeros((POOL_PAD, POOL_WIDTH), F32)
        upad[0:POOL_PAD, :] = zeros
        upad[POOL_PAD + seq:POOL_PAD + seq + POOL_PAD, :] = zeros
        upad[POOL_PAD:POOL_PAD + seq, :] = u_ref[...]
        rb = 256
        for r in range(seq // rb):
            t = lax.broadcasted_iota(jnp.int32, (rb, POOL_GROUP), 0) + r * rb
            for g, w in enumerate(POOL_WINDOWS):
                cols = slice(g * POOL_GROUP, (g + 1) * POOL_GROUP)
                acc = None
                for j in range(-(w // 2), w // 2):
                    term = upad[POOL_PAD + r * rb + j:POOL_PAD + r * rb + j + rb, cols]
                    acc = term if acc is None else acc + term
                cnt = (jnp.minimum(t - w // 2 + w, seq) - jnp.maximum(t - w // 2, 0)).astype(F32)
                y = acc / cnt - upad[POOL_PAD + r * rb:POOL_PAD + (r + 1) * rb, cols]
                ypool[r * rb:(r + 1) * rb, cols] = y.astype(BF16)

    lp = lamp_ref[...]
    lam = (jnp.exp(jnp.sum(lp[0:1, :] * lp[1:2, :], axis=-1, keepdims=True))
           - jnp.exp(jnp.sum(lp[2:3, :] * lp[3:4, :], axis=-1, keepdims=True)) + lam_init)

    lane = lax.broadcasted_iota(jnp.int32, (1, 128), 1)
    m1 = jnp.where(lane < ATT_DH, 1.0, 0.0).astype(BF16)
    m2 = jnp.where(lane >= ATT_DH, 1.0, 0.0).astype(BF16)
    nt = (((1,), (1,)), ((), ()))

    heads = []
    for h in range(ATT_HEADS):
        cols = slice(h * 128, (h + 1) * 128)
        qh = q_ref[:, cols]
        q2 = jnp.concatenate([qh * m1, qh * m2], axis=0)
        kh = k_ref[:, cols].astype(BF16)
        vh = v_ref[:, cols].astype(BF16)
        s = lax.dot_general(q2, kh, nt, preferred_element_type=F32)
        mx = jnp.max(s, axis=-1, keepdims=True)
        if n_cache:
            ckh = ck_ref[:, cols].astype(BF16)
            cvh = cv_ref[:, cols].astype(BF16)
            sc = lax.dot_general(q2, ckh, nt, preferred_element_type=F32)
            mx = jnp.maximum(mx, jnp.max(sc, axis=-1, keepdims=True))
        e = jnp.exp(s - mx)
        den = jnp.sum(e, axis=-1, keepdims=True)
        if n_cache:
            ec = jnp.exp(sc - mx)
            den = den + jnp.sum(ec, axis=-1, keepdims=True)
        inv = 1.0 / den
        c1, c2 = inv[:tq], lam * inv[tq:]
        a = (e[:tq] * c1 - e[tq:] * c2).astype(BF16)
        o = jnp.dot(a, vh, preferred_element_type=F32)
        if n_cache:
            ac = (ec[:tq] * c1 - ec[tq:] * c2).astype(BF16)
            o = o + jnp.dot(ac, cvh, preferred_element_type=F32)
        o = o * lax.rsqrt(jnp.mean(o * o, axis=-1, keepdims=True) + EPS)
        heads.append((o * subg_ref[:, cols]) * (1.0 - lam_init))
    o_att = jnp.concatenate(heads, axis=1).astype(BF16)

    yq = ypool[pl.ds(pl.multiple_of(qi * tq, tq), tq), :]
    pooled = [jnp.dot(yq[:, g * POOL_GROUP:(g + 1) * POOL_GROUP], wpool_ref[g],
                      preferred_element_type=F32) for g in range(len(POOL_WINDOWS))]
    p = (jnp.concatenate(pooled, axis=1) * pscale_ref[...]).astype(BF16)

    out = (jnp.dot(o_att, wout_ref[0:ATT_WIDTH, :], preferred_element_type=F32)
           + jnp.dot(p, wout_ref[ATT_WIDTH:, :], preferred_element_type=F32))
    o_ref[...] = x_ref[...] + mod_ref[2:3, :] * out


def _even_mix(x, q, k, v, u, cache, mod, layer, group_of_seq, lamp, subg, wpool_bf, pscale,
              wout_bf, seq, lam_init):
    n = x.shape[0]
    nseq = n // seq
    tq = 256
    nq = seq // tq
    blk = lambda b, qi: (b * nq + qi, 0)
    whole = lambda b, qi: (b, 0)
    in_specs = [pl.BlockSpec((tq, D_MODEL), blk), pl.BlockSpec((tq, 512), blk),
                pl.BlockSpec((seq, 512), whole), pl.BlockSpec((seq, 512), whole),
                pl.BlockSpec((seq, 512), whole)]
    args = [x, q, k, v, u]
    n_cache = 0
    if cache is not None:
        ck, cv, il = cache
        n_cache = ck.shape[2]
        cspec = pl.BlockSpec((None, None, n_cache, 512), lambda b, qi: (b, il, 0, 0))
        in_specs += [cspec, cspec]
        args += [ck, cv]
    in_specs += [_mod_spec(layer, lambda b, qi: group_of_seq(b)),
                 _resident((4, ATT_DH)), _resident((1, ATT_WIDTH)),
                 _resident((len(POOL_WINDOWS), POOL_GROUP, POOL_GROUP)),
                 _resident((1, POOL_WIDTH)), _resident((D_MODEL, D_MODEL))]
    args += [mod, lamp, subg, wpool_bf, pscale, wout_bf]
    return pl.pallas_call(
        functools.partial(_even_mix_kernel, seq=seq, tq=tq, n_cache=n_cache, lam_init=lam_init),
        grid=(nseq, nq),
        in_specs=in_specs,
        out_specs=pl.BlockSpec((tq, D_MODEL), blk),
        out_shape=jax.ShapeDtypeStruct((n, D_MODEL), F32),
        scratch_shapes=[pltpu.VMEM((seq + 2 * POOL_PAD, POOL_WIDTH), F32),
                        pltpu.VMEM((seq, POOL_WIDTH), BF16)],
        compiler_params=_params(2),
        name="even_mix",
    )(*args)


def _ffn_kernel(x_ref, mod_ref, g_ref, wa_ref, wb_ref, wo_ref, o_ref, acc_ref):
    x = x_ref[...]
    h = _norm_mod(x, g_ref[...], mod_ref[3:4, :], mod_ref[4:5, :]).astype(BF16)
    for c in range(FFN_HIDDEN // FFN_CHUNK):
        a = jnp.dot(h, wa_ref[c], preferred_element_type=F32)
        b = jnp.dot(h, wb_ref[c], preferred_element_type=F32)
        z = (_silu(a) * b).astype(BF16)
        part = jnp.dot(z, wo_ref[c], preferred_element_type=F32)
        if c == 0:
            acc_ref[...] = part
        else:
            acc_ref[...] += part
    o_ref[...] = x + mod_ref[5:6, :] * acc_ref[...]


def _ffn(x, mod, layer, group_of_step, g2, wa_bf, wb_bf, wo_bf):
    n = x.shape[0]
    tm = 512
    nc = FFN_HIDDEN // FFN_CHUNK
    row = lambda i: (i, 0)
    return pl.pallas_call(
        _ffn_kernel,
        grid=(n // tm,),
        in_specs=[pl.BlockSpec((tm, D_MODEL), row), _mod_spec(layer, group_of_step),
                  _resident((1, D_MODEL)), _resident((nc, D_MODEL, FFN_CHUNK)),
                  _resident((nc, D_MODEL, FFN_CHUNK)), _resident((nc, FFN_CHUNK, D_MODEL))],
        out_specs=pl.BlockSpec((tm, D_MODEL), row),
        out_shape=jax.ShapeDtypeStruct((n, D_MODEL), F32),
        scratch_shapes=[pltpu.VMEM((tm, D_MODEL), F32)],
        compiler_params=_params(1),
        name="ffn",
    )(x, mod, g2, wa_bf, wb_bf, wo_bf)


def _odd_in_kernel(x_ref, mod_ref, g_ref, w_ref, wgd_ref, wup_ref, bg_ref,
                   qk_ref, v_ref, r_ref, gate_ref, *, tm, rc):
    shift, scale, g = mod_ref[0:1, :], mod_ref[1:2, :], g_ref[...]

    def body(r, carry):
        rows = pl.ds(pl.multiple_of(r * rc, rc), rc)
        h = _norm_mod(x_ref[rows, :], g, shift, scale).astype(BF16)
        qk_ref[rows, :] = jnp.dot(h, w_ref[:, 0:1024], preferred_element_type=F32)
        v_ref[rows, :] = jnp.dot(h, w_ref[:, 1024:2048], preferred_element_type=F32).astype(BF16)
        r_ref[rows, :] = jnp.dot(h, w_ref[:, 2048:3072], preferred_element_type=F32)
        gd = jnp.dot(h, wgd_ref[...], preferred_element_type=F32).astype(BF16)
        logit = jnp.dot(gd, wup_ref[...], preferred_element_type=F32) + bg_ref[...]
        log_sig = jnp.minimum(logit, 0.0) - jnp.log1p(jnp.exp(-jnp.abs(logit)))
        gate_ref[rows, :] = log_sig / GATE_TEMP
        return carry

    lax.fori_loop(0, tm // rc, body, 0)


def _odd_in(x, mod, layer, group_of_step, g1, w_bf, wgd_bf, wup_bf, bg):
    n = x.shape[0]
    tm, rc = 512, 256
    row = lambda i: (i, 0)
    return pl.pallas_call(
        functools.partial(_odd_in_kernel, tm=tm, rc=rc),
        grid=(n // tm,),
        in_specs=[pl.BlockSpec((tm, D_MODEL), row), _mod_spec(layer, group_of_step),
                  _resident((1, D_MODEL)), _resident((D_MODEL, 3072)),
                  _resident((D_MODEL, 128)), _resident((128, D_MODEL)), _resident((1, D_MODEL))],
        out_specs=[pl.BlockSpec((tm, D_MODEL), row)] * 4,
        out_shape=[jax.ShapeDtypeStruct((n, D_MODEL), dt) for dt in (F32, BF16, F32, F32)],
        compiler_params=_params(1),
        name="odd_in",
    )(x, mod, g1, w_bf, wgd_bf, wup_bf, bg)


def _split3(x):
    hi = x.astype(BF16)
    r1 = x - hi.astype(F32)
    mid = r1.astype(BF16)
    lo = (r1 - mid.astype(F32)).astype(BF16)
    return hi, mid, lo


def _gla_kernel(*refs, tb, has_init, emit_state):
    refs = list(refs)
    ins = [refs.pop(0) for _ in range(6)]
    init_ref = refs.pop(0) if has_init else None
    o_refs = [refs.pop(0), refs.pop(0)]
    state_ref = refs.pop(0) if emit_state else None
    st = refs.pop(0)
    n = pl.program_id(1)

    @pl.when(n == 0)
    def _init():
        for d in range(2):
            for h in range(GLA_HEADS):
                if has_init:
                    st[d, h] = init_ref[d, h].T
                else:
                    st[d, h] = jnp.zeros((GLA_DV, GLA_DK), F32)

    ri = lax.broadcasted_iota(jnp.int32, (CHUNK, CHUNK), 0)
    ci = lax.broadcasted_iota(jnp.int32, (CHUNK, CHUNK), 1)
    nt = (((1,), (1,)), ((), ()))
    tn = (((0,), (0,)), ((), ()))
    nchunk = tb // CHUNK

    for d in range(2):
        qk_ref, v_ref, g_ref = ins[3 * d:3 * d + 3]
        o_ref = o_refs[d]
        mask = (ci <= ri) if d == 0 else (ci >= ri)
        tri = jnp.where(mask, 1.0, 0.0).astype(BF16)
        edge = CHUNK - 1 if d == 0 else 0

        def body(i, carry, d=d, qk_ref=qk_ref, v_ref=v_ref, g_ref=g_ref, o_ref=o_ref,
                 mask=mask, tri=tri, edge=edge):
            c = i if d == 0 else nchunk - 1 - i
            rows = pl.ds(pl.multiple_of(c * CHUNK, CHUNK), CHUNK)
            g = g_ref[rows, :]
            b = sum(jnp.dot(tri, part, preferred_element_type=F32) for part in _split3(g))
            b_edge = b[edge:edge + 1, :]
            grow, fall, tail, decay = jnp.exp(-b), jnp.exp(b), jnp.exp(b_edge - b), jnp.exp(b_edge)
            q = qk_ref[rows, 0:512] * (GLA_DK ** -0.5)
            k = qk_ref[rows, 512:1024]
            q_t = (q * fall).astype(BF16)
            k_t = (k * grow).astype(BF16)
            k_end = (k * tail).astype(BF16)
            for h in range(GLA_HEADS):
                kc = slice(h * GLA_DK, (h + 1) * GLA_DK)
                vc = slice(h * GLA_DV, (h + 1) * GLA_DV)
                vh = v_ref[rows, vc]
                a = lax.dot_general(q_t[:, kc], k_t[:, kc], nt, preferred_element_type=F32)
                a = jnp.where(mask, a, 0.0).astype(BF16)
                s_t = st[d, h]
                o = (jnp.dot(a, vh, preferred_element_type=F32)
                     + lax.dot_general(q_t[:, kc], s_t.astype(BF16), nt, preferred_element_type=F32))
                o_ref[rows, vc] = o
                st[d, h] = (s_t * decay[:, kc]
                            + lax.dot_general(vh, k_end[:, kc], tn, preferred_element_type=F32))
            return carry

        lax.fori_loop(0, nchunk, body, 0)

    if emit_state:
        @pl.when(n == pl.num_programs(1) - 1)
        def _emit():
            for d in range(2):
                for h in range(GLA_HEADS):
                    state_ref[d, h] = st[d, h].T


def _gla(qk, v, gate, init, seq, emit_state):
    n = qk.shape[0]
    nseq = n // seq
    tb = 256
    nsb = seq // tb
    fwd = lambda b, s: (b * nsb + s, 0)
    bwd = lambda b, s: (b * nsb + nsb - 1 - s, 0)
    bwd_gate = lambda b, s: (b * nsb + nsb - 1 - s, 1)
    in_specs = [pl.BlockSpec((tb, 1024), fwd), pl.BlockSpec((tb, 1024), fwd), pl.BlockSpec((tb, 512), fwd),
                pl.BlockSpec((tb, 1024), bwd), pl.BlockSpec((tb, 1024), bwd), pl.BlockSpec((tb, 512), bwd_gate)]
    args = [qk, v, gate, qk, v, gate]
    if init is not None:
        state, il = init
        in_specs.append(pl.BlockSpec((None, None, 2, GLA_HEADS, GLA_DK, GLA_DV),
                                     lambda b, s: (b, il, 0, 0, 0, 0)))
        args.append(state)
    out_specs = [pl.BlockSpec((tb, 1024), fwd), pl.BlockSpec((tb, 1024), bwd)]
    out_shape = [jax.ShapeDtypeStruct((n, 1024), F32)] * 2
    if emit_state:
        out_specs.append(pl.BlockSpec((None, 2, GLA_HEADS, GLA_DK, GLA_DV), lambda b, s: (b, 0, 0, 0, 0)))
        out_shape.append(jax.ShapeDtypeStruct((nseq, 2, GLA_HEADS, GLA_DK, GLA_DV), F32))
    return pl.pallas_call(
        functools.partial(_gla_kernel, tb=tb, has_init=init is not None, emit_state=emit_state),
        grid=(nseq, nsb),
        in_specs=in_specs,
        out_specs=out_specs,
        out_shape=out_shape,
        scratch_shapes=[pltpu.VMEM((2, GLA_HEADS, GLA_DV, GLA_DK), F32)],
        compiler_params=_params(2),
        name="gla",
    )(*args)


def _gla_out_kernel(x_ref, of_ref, ob_ref, r_ref, mod_ref, gn_ref, w_ref, o_ref):
    o = of_ref[...] + ob_ref[...]
    parts = []
    for h in range(GLA_HEADS):
        cols = slice(h * GLA_DV, (h + 1) * GLA_DV)
        oh = o[:, cols]
        oh = oh * lax.rsqrt(jnp.mean(oh * oh, axis=-1, keepdims=True) + EPS)
        parts.append(oh * gn_ref[:, cols])
    z = (jnp.concatenate(parts, axis=1) * _silu(r_ref[...])).astype(BF16)
    out = jnp.dot(z, w_ref[...], preferred_element_type=F32)
    o_ref[...] = x_ref[...] + mod_ref[2:3, :] * out


def _gla_out(x, o_f, o_b, r, mod, layer, group_of_step, gn, w_bf):
    n = x.shape[0]
    tm = 512
    row = lambda i: (i, 0)
    tile = pl.BlockSpec((tm, D_MODEL), row)
    return pl.pallas_call(
        _gla_out_kernel,
        grid=(n // tm,),
        in_specs=[tile, tile, tile, tile, _mod_spec(layer, group_of_step),
                  _resident((1, D_MODEL)), _resident((D_MODEL, D_MODEL))],
        out_specs=tile,
        out_shape=jax.ShapeDtypeStruct((n, D_MODEL), F32),
        compiler_params=_params(1),
        name="gla_out",
    )(x, o_f, o_b, r, mod, gn, w_bf)


def _final_kernel(x_ref, g_ref, o_ref):
    x = x_ref[...]
    o_ref[...] = (x * lax.rsqrt(jnp.mean(x * x, axis=-1, keepdims=True) + EPS)) * g_ref[...]


def _final_norm(x, g):
    n = x.shape[0]
    tm = 1024
    row = lambda i: (i, 0)
    return pl.pallas_call(
        _final_kernel,
        grid=(n // tm,),
        in_specs=[pl.BlockSpec((tm, D_MODEL), row), _resident((1, D_MODEL))],
        out_specs=pl.BlockSpec((tm, D_MODEL), row),
        out_shape=jax.ShapeDtypeStruct((n, D_MODEL), F32),
        compiler_params=_params(1),
        name="final_norm",
    )(x, g)


def _rope_tables(n_lat):
    half = ATT_DH // 2
    inv = 1.0 / (ROPE_BASE ** (jnp.arange(0, half, 2, dtype=F32) / half))
    t = jnp.arange(n_lat)
    rows = (t // GRID_W).astype(F32)
    cols = (t % GRID_W).astype(F32)

    def tab(pos):
        ang = pos[:, None] * inv[None, :]
        ang = jnp.concatenate([ang, ang], axis=-1)
        return jnp.cos(ang), jnp.sin(ang)

    (cr, sr), (cc, sc) = tab(rows), tab(cols)
    sign = jnp.concatenate([-jnp.ones((half // 2,), F32), jnp.ones((half // 2,), F32)])
    cos = jnp.concatenate([cr, cc], axis=-1)
    sin = jnp.concatenate([sr * sign, sc * sign], axis=-1)
    return jnp.tile(cos, (1, 2)), jnp.tile(sin, (1, 2))


def kernel(x_prompt, x_sample, c, cache_attn_k, cache_attn_v, state_gla, c_ctx, norm1_g, norm2_g, w_mod, b_mod, w_in_even, lam_params, subln_g, w_pool, pool_scale, w_out_even, w_in_odd, w_gate_up, b_gate, gla_norm_g, w_out_odd, w_ffn_in, w_ffn_out, final_g):
    bp, tp, _ = x_prompt.shape
    bs, ts, _ = x_sample.shape
    past = cache_attn_k.shape[2]
    n_even = cache_attn_k.shape[1]
    assert bs + 1 <= MOD_ROWS

    xp = x_prompt.reshape(bp * tp, D_MODEL)
    xs = x_sample.reshape(bs * ts, D_MODEL)
    ck = cache_attn_k.reshape(bs, n_even, past, ATT_WIDTH)
    cv = cache_attn_v.reshape(bs, n_even, past, ATT_WIDTH)

    cvec = jnp.concatenate([c_ctx[None, :], c, jnp.zeros((MOD_ROWS - 1 - bs, D_MODEL), F32)], axis=0)
    mod = _adaln(cvec, w_mod, b_mod)
    ctx_group = lambda i: 0

    def lat_group(tm):
        return lambda i: 1 + (i * tm) // ts

    rope = _rope_tables(ts)
    nchunk = FFN_HIDDEN // FFN_CHUNK

    ks_new, vs_new, ss_new = [], [], []
    for l in range(DEPTH):
        i = l // 2
        g1 = norm1_g[l].reshape(1, D_MODEL)
        g2 = norm2_g[l].reshape(1, D_MODEL)
        if l % 2 == 0:
            lam_init = 0.8 - 0.6 * math.exp(-0.3 * l)
            w_in = w_in_even[i].astype(BF16)
            w_out = w_out_even[i].astype(BF16)
            wpool = w_pool[i].astype(BF16)
            subg = subln_g[i].reshape(1, ATT_WIDTH)
            pscale = pool_scale[i].reshape(1, POOL_WIDTH)
            lamp = lam_params[i]
            q, k, v, u = _even_in(xp, mod, l, ctx_group, g1, w_in, None, tp, F32)
            ks_new.append(k.reshape(bp, tp, ATT_WIDTH))
            vs_new.append(v.reshape(bp, tp, ATT_WIDTH))
            xp = _even_mix(xp, q, k, v, u, None, mod, l, ctx_group, lamp, subg, wpool, pscale,
                           w_out, tp, lam_init)
            q, k, v, u = _even_in(xs, mod, l, lat_group(1024), g1, w_in, rope, ts, BF16)
            xs = _even_mix(xs, q, k, v, u, (ck, cv, i), mod, l, lambda b: 1 + b, lamp, subg, wpool,
                           pscale, w_out, ts, lam_init)
        else:
            w_in = w_in_odd[i]
            w_main = w_in[:, :3072].astype(BF16)
            wgd = jnp.pad(w_in[:, 3072:], ((0, 0), (0, 128 - 2 * GATE_RANK))).astype(BF16)
            wup = jnp.zeros((128, D_MODEL), F32)
            wup = wup.at[0:GATE_RANK, 0:512].set(w_gate_up[i, 0])
            wup = wup.at[GATE_RANK:2 * GATE_RANK, 512:1024].set(w_gate_up[i, 1]).astype(BF16)
            bg = b_gate[i].reshape(1, D_MODEL)
            w_out = w_out_odd[i].astype(BF16)
            gn = gla_norm_g[i].reshape(1, D_MODEL)
            qk, v, r, gate = _odd_in(xp, mod, l, ctx_group, g1, w_main, wgd, wup, bg)
            o_f, o_b, s_ctx = _gla(qk, v, gate, None, tp, True)
            ss_new.append(s_ctx)
            xp = _gla_out(xp, o_f, o_b, r, mod, l, ctx_group, gn, w_out)
            qk, v, r, gate = _odd_in(xs, mod, l, lat_group(512), g1, w_main, wgd, wup, bg)
            o_f, o_b = _gla(qk, v, gate, (state_gla, i), ts, False)
            xs = _gla_out(xs, o_f, o_b, r, mod, l, lat_group(512), gn, w_out)
        w_ffn = w_ffn_in[l].astype(BF16)
        wa = w_ffn[:, :FFN_HIDDEN].reshape(D_MODEL, nchunk, FFN_CHUNK).transpose(1, 0, 2)
        wb = w_ffn[:, FFN_HIDDEN:].reshape(D_MODEL, nchunk, FFN_CHUNK).transpose(1, 0, 2)
        wo = w_ffn_out[l].astype(BF16).reshape(nchunk, FFN_CHUNK, D_MODEL)
        xp = _ffn(xp, mod, l, ctx_group, g2, wa, wb, wo)
        xs = _ffn(xs, mod, l, lat_group(512), g2, wa, wb, wo)

    fg = final_g.reshape(1, D_MODEL)
    y_prompt = _final_norm(xp, fg).reshape(bp, tp, D_MODEL)
    y_sample = _final_norm(xs, fg).reshape(bs, ts, D_MODEL)
    new_attn_k = jnp.stack(ks_new, axis=1).reshape(bp, n_even, tp, ATT_HEADS, 2, ATT_DH)
    new_attn_v = jnp.stack(vs_new, axis=1).reshape(bp, n_even, tp, ATT_HEADS, ATT_DV)
    new_gla_state = jnp.stack(ss_new, axis=1)
    return (y_prompt, y_sample, new_attn_k, new_attn_v, new_gla_state)
```

```python
import functools
import math

import jax
import jax.numpy as jnp
from jax import lax
from jax.experimental import pallas as pl
from jax.experimental.pallas import tpu as pltpu

F32 = jnp.float32
BF16 = jnp.bfloat16

D_MODEL = 1024
DEPTH = 4
GRID_W = 64
ATT_WIDTH = 512
POOL_WIDTH = 512
ATT_HEADS = 4
ATT_DH = 64
ATT_DV = 128
ROPE_BASE = 10000.0
POOL_WINDOWS = (2, 4, 8, 16)
POOL_GROUP = 128
POOL_PAD = 8
GLA_HEADS = 4
GLA_DK = 128
GLA_DV = 256
GATE_RANK = 16
GATE_TEMP = 16.0
CHUNK = 64
FFN_HIDDEN = 2816
FFN_CHUNK = 256
EPS = 1e-6
Q_SCALE = ATT_DH ** -0.5 * math.log2(math.e)
MOD_ROWS = 16

V7X_VMEM_LIMIT = 56 * 1024 * 1024


def _params(n_axes, vmem_bytes=V7X_VMEM_LIMIT):
    return pltpu.CompilerParams(dimension_semantics=("arbitrary",) * n_axes,
                                vmem_limit_bytes=vmem_bytes)


def _resident(shape):
    nd = len(shape)
    return pl.BlockSpec(shape, lambda *_: (0,) * nd, pipeline_mode=pl.Buffered(1))


def _norm_mod(x, g, shift, scale):
    y = x * lax.rsqrt(jnp.mean(x * x, axis=-1, keepdims=True) + EPS)
    return (y * g) * (1.0 + scale) + shift


def _silu(x):
    return x * jax.nn.sigmoid(x)


def _adaln_kernel(cv_ref, w_ref, b_ref, o_ref):
    s = _silu(cv_ref[...]).astype(BF16)
    o_ref[...] = jnp.dot(s, w_ref[...].astype(BF16), preferred_element_type=F32) + b_ref[...]


def _adaln(cvec, w_mod, b_mod):
    tn = 1536
    out = pl.pallas_call(
        _adaln_kernel,
        grid=(DEPTH, 6 * D_MODEL // tn),
        in_specs=[pl.BlockSpec((MOD_ROWS, D_MODEL), lambda l, j: (0, 0)),
                  pl.BlockSpec((None, D_MODEL, tn), lambda l, j: (l, 0, j)),
                  pl.BlockSpec((None, 1, tn), lambda l, j: (l, 0, j))],
        out_specs=pl.BlockSpec((None, MOD_ROWS, tn), lambda l, j: (l, 0, j)),
        out_shape=jax.ShapeDtypeStruct((DEPTH, MOD_ROWS, 6 * D_MODEL), F32),
        compiler_params=_params(2),
        name="adaln",
    )(cvec, w_mod, b_mod.reshape(DEPTH, 1, 6 * D_MODEL))
    return out.reshape(DEPTH, MOD_ROWS, 6, D_MODEL)


def _mod_spec(layer, group_of_step):
    return pl.BlockSpec((None, None, 6, D_MODEL),
                        lambda *ids: (layer, group_of_step(*ids), 0, 0))


def _even_in_kernel(*refs, tm, rc, rope):
    if rope:
        x_ref, mod_ref, g_ref, w_ref, cos_ref, sin_ref, q_ref, k_ref, v_ref, u_ref = refs
    else:
        x_ref, mod_ref, g_ref, w_ref, q_ref, k_ref, v_ref, u_ref = refs
    shift, scale, g = mod_ref[0:1, :], mod_ref[1:2, :], g_ref[...]
    lo = (lax.broadcasted_iota(jnp.int32, (rc, 128), 1) % 32) < 16

    def rotary(y, cos, sin):
        parts = []
        for h in range(ATT_HEADS):
            yh = y[:, h * 128:(h + 1) * 128]
            fwd = pltpu.roll(yh, 128 - 16, axis=1)
            bwd = pltpu.roll(yh, 16, axis=1)
            parts.append(yh * cos + jnp.where(lo, fwd, bwd) * sin)
        return jnp.concatenate(parts, axis=1)

    def body(r, carry):
        rows = pl.ds(pl.multiple_of(r * rc, rc), rc)
        h = _norm_mod(x_ref[rows, :], g, shift, scale).astype(BF16)
        q = jnp.dot(h, w_ref[:, 0:512], preferred_element_type=F32)
        k = jnp.dot(h, w_ref[:, 512:1024], preferred_element_type=F32)
        if rope:
            cos, sin = cos_ref[rows, :], sin_ref[rows, :]
            q, k = rotary(q, cos, sin), rotary(k, cos, sin)
        q_ref[rows, :] = (q * Q_SCALE).astype(q_ref.dtype)
        k_ref[rows, :] = k.astype(k_ref.dtype)
        v_ref[rows, :] = jnp.dot(h, w_ref[:, 1024:1536], preferred_element_type=F32).astype(v_ref.dtype)
        u_ref[rows, :] = jnp.dot(h, w_ref[:, 1536:2048], preferred_element_type=F32)
        return carry

    lax.fori_loop(0, tm // rc, body, 0)


def _even_in(x, mod, layer, group_of_step, g1, w_bf, rope_tabs, seq_len, kv_dtype):
    n = x.shape[0]
    tm, rc = 1024, 512
    row = lambda i: (i, 0)
    in_specs = [pl.BlockSpec((tm, D_MODEL), row), _mod_spec(layer, group_of_step),
                _resident((1, D_MODEL)), _resident((D_MODEL, 2048))]
    args = [x, mod, g1, w_bf]
    if rope_tabs is not None:
        per_seq = seq_len // tm
        pos = lambda i: (i % per_seq, 0)
        in_specs += [pl.BlockSpec((tm, 128), pos), pl.BlockSpec((tm, 128), pos)]
        args += list(rope_tabs)
    out_dt = (BF16, kv_dtype, kv_dtype, F32)
    return pl.pallas_call(
        functools.partial(_even_in_kernel, tm=tm, rc=rc, rope=rope_tabs is not None),
        grid=(n // tm,),
        in_specs=in_specs,
        out_specs=[pl.BlockSpec((tm, 512), row)] * 4,
        out_shape=[jax.ShapeDtypeStruct((n, 512), dt) for dt in out_dt],
        compiler_params=_params(1),
        name="even_in",
    )(*args)


def _even_mix_kernel(*refs, seq, tq, n_cache, lam_init):
    if n_cache:
        (x_ref, q_ref, k_ref, v_ref, u_ref, ck_ref, cv_ref, mod_ref, lamp_ref, subg_ref,
         wpool_ref, pscale_ref, wout_ref, o_ref, upad, ypool) = refs
    else:
        (x_ref, q_ref, k_ref, v_ref, u_ref, mod_ref, lamp_ref, subg_ref,
         wpool_ref, pscale_ref, wout_ref, o_ref, upad, ypool) = refs
    qi = pl.program_id(1)

    @pl.when(qi == 0)
    def _pool():
        zeros = jnp.zeros((POOL_PAD, POOL_WIDTH), F32)
        upad[0:POOL_PAD, :] = zeros
        upad[POOL_PAD + seq:POOL_PAD + seq + POOL_PAD, :] = zeros
        upad[POOL_PAD:POOL_PAD + seq, :] = u_ref[...]
        n_pad = seq + 2 * POOL_PAD
        t = lax.broadcasted_iota(jnp.int32, (seq, POOL_GROUP), 0)
        for g, w in enumerate(POOL_WINDOWS):
            cols = slice(g * POOL_GROUP, (g + 1) * POOL_GROUP)
            xg = upad[:, cols]
            p = xg + pltpu.roll(xg, 1, axis=0)
            half = 1
            while 2 * half < w:
                p = pltpu.roll(p, half, axis=0) + pltpu.roll(p, n_pad - half, axis=0)
                half *= 2
            cnt = (jnp.minimum(t - w // 2 + w, seq) - jnp.maximum(t - w // 2, 0)).astype(F32)
            y = p[POOL_PAD:POOL_PAD + seq] / cnt - xg[POOL_PAD:POOL_PAD + seq]
            ypool[:, cols] = y.astype(BF16)

    lp = lamp_ref[...]
    lam = (jnp.exp(jnp.sum(lp[0:1, :] * lp[1:2, :], axis=-1, keepdims=True))
           - jnp.exp(jnp.sum(lp[2:3, :] * lp[3:4, :], axis=-1, keepdims=True)) + lam_init)

    lane = lax.broadcasted_iota(jnp.int32, (1, 128), 1)
    m1 = jnp.where(lane < ATT_DH, 1.0, 0.0).astype(BF16)
    m2 = jnp.where(lane >= ATT_DH, 1.0, 0.0).astype(BF16)
    nt = (((1,), (1,)), ((), ()))

    def scores(h):
        cols = slice(h * 128, (h + 1) * 128)
        qh = q_ref[:, cols]
        q2 = jnp.concatenate([qh * m1, qh * m2], axis=0)
        s = lax.dot_general(q2, k_ref[:, cols].astype(BF16), nt, preferred_element_type=F32)
        sc = None
        if n_cache:
            sc = lax.dot_general(q2, ck_ref[:, cols].astype(BF16), nt, preferred_element_type=F32)
        return s, sc

    heads = []
    pending = scores(0)
    for h in range(ATT_HEADS):
        cols = slice(h * 128, (h + 1) * 128)
        s, sc = pending
        if h + 1 < ATT_HEADS:
            pending = scores(h + 1)
        vh = v_ref[:, cols].astype(BF16)
        mx = jnp.max(s, axis=-1, keepdims=True)
        if n_cache:
            cvh = cv_ref[:, cols].astype(BF16)
            mx = jnp.maximum(mx, jnp.max(sc, axis=-1, keepdims=True))
        e = jnp.exp2(s - mx)
        den = jnp.sum(e, axis=-1, keepdims=True)
        if n_cache:
            ec = jnp.exp2(sc - mx)
            den = den + jnp.sum(ec, axis=-1, keepdims=True)
        inv1 = 1.0 / den[:tq]
        ratio = lam * den[:tq] / den[tq:]
        a = (e[:tq] - ratio * e[tq:]).astype(BF16)
        o = jnp.dot(a, vh, preferred_element_type=F32)
        if n_cache:
            ac = (ec[:tq] - ratio * ec[tq:]).astype(BF16)
            o = o + jnp.dot(ac, cvh, preferred_element_type=F32)
        o = o * inv1
        o = o * lax.rsqrt(jnp.mean(o * o, axis=-1, keepdims=True) + EPS)
        heads.append((o * subg_ref[:, cols]) * (1.0 - lam_init))
    o_att = jnp.concatenate(heads, axis=1).astype(BF16)

    yq = ypool[pl.ds(pl.multiple_of(qi * tq, tq), tq), :]
    pooled = [jnp.dot(yq[:, g * POOL_GROUP:(g + 1) * POOL_GROUP], wpool_ref[g],
                      preferred_element_type=F32) for g in range(len(POOL_WINDOWS))]
    p = (jnp.concatenate(pooled, axis=1) * pscale_ref[...]).astype(BF16)

    out = (jnp.dot(o_att, wout_ref[0:ATT_WIDTH, :], preferred_element_type=F32)
           + jnp.dot(p, wout_ref[ATT_WIDTH:, :], preferred_element_type=F32))
    o_ref[...] = x_ref[...] + mod_ref[2:3, :] * out


def _even_mix(x, q, k, v, u, cache, mod, layer, group_of_seq, lamp, subg, wpool_bf, pscale,
              wout_bf, seq, lam_init):
    n = x.shape[0]
    nseq = n // seq
    tq = 256
    nq = seq // tq
    blk = lambda b, qi: (b * nq + qi, 0)
    whole = lambda b, qi: (b, 0)
    in_specs = [pl.BlockSpec((tq, D_MODEL), blk), pl.BlockSpec((tq, 512), blk),
                pl.BlockSpec((seq, 512), whole), pl.BlockSpec((seq, 512), whole),
                pl.BlockSpec((seq, 512), whole)]
    args = [x, q, k, v, u]
    n_cache = 0
    if cache is not None:
        ck, cv, il = cache
        n_cache = ck.shape[2]
        cspec = pl.BlockSpec((None, None, n_cache, 512), lambda b, qi: (b, il, 0, 0))
        in_specs += [cspec, cspec]
        args += [ck, cv]
    in_specs += [_mod_spec(layer, lambda b, qi: group_of_seq(b)),
                 _resident((4, ATT_DH)), _resident((1, ATT_WIDTH)),
                 _resident((len(POOL_WINDOWS), POOL_GROUP, POOL_GROUP)),
                 _resident((1, POOL_WIDTH)), _resident((D_MODEL, D_MODEL))]
    args += [mod, lamp, subg, wpool_bf, pscale, wout_bf]
    return pl.pallas_call(
        functools.partial(_even_mix_kernel, seq=seq, tq=tq, n_cache=n_cache, lam_init=lam_init),
        grid=(nseq, nq),
        in_specs=in_specs,
        out_specs=pl.BlockSpec((tq, D_MODEL), blk),
        out_shape=jax.ShapeDtypeStruct((n, D_MODEL), F32),
        scratch_shapes=[pltpu.VMEM((seq + 2 * POOL_PAD, POOL_WIDTH), F32),
                        pltpu.VMEM((seq, POOL_WIDTH), BF16)],
        compiler_params=_params(2),
        name="even_mix",
    )(*args)


def _ffn_kernel(x_ref, mod_ref, g_ref, wi_ref, wo_ref, o_ref, acc_ref):
    x = x_ref[...]
    h = _norm_mod(x, g_ref[...], mod_ref[3:4, :], mod_ref[4:5, :]).astype(BF16)
    for c in range(FFN_HIDDEN // FFN_CHUNK):
        lo, hi = c * FFN_CHUNK, (c + 1) * FFN_CHUNK
        a = jnp.dot(h, wi_ref[:, lo:hi], preferred_element_type=F32)
        b = jnp.dot(h, wi_ref[:, FFN_HIDDEN + lo:FFN_HIDDEN + hi], preferred_element_type=F32)
        z = (_silu(a) * b).astype(BF16)
        part = jnp.dot(z, wo_ref[lo:hi, :], preferred_element_type=F32)
        if c == 0:
            acc_ref[...] = part
        else:
            acc_ref[...] += part
    o_ref[...] = x + mod_ref[5:6, :] * acc_ref[...]


def _ffn(x, mod, layer, group_of_step, g2, wi_bf, wo_bf):
    n = x.shape[0]
    tm = 512
    row = lambda i: (i, 0)
    return pl.pallas_call(
        _ffn_kernel,
        grid=(n // tm,),
        in_specs=[pl.BlockSpec((tm, D_MODEL), row), _mod_spec(layer, group_of_step),
                  _resident((1, D_MODEL)), _resident((D_MODEL, 2 * FFN_HIDDEN)),
                  _resident((FFN_HIDDEN, D_MODEL))],
        out_specs=pl.BlockSpec((tm, D_MODEL), row),
        out_shape=jax.ShapeDtypeStruct((n, D_MODEL), F32),
        scratch_shapes=[pltpu.VMEM((tm, D_MODEL), F32)],
        compiler_params=_params(1),
        name="ffn",
    )(x, mod, g2, wi_bf, wo_bf)


def _odd_in_kernel(x_ref, mod_ref, g_ref, w_ref, wgd_ref, wup_ref, bg_ref,
                   qk_ref, v_ref, r_ref, gate_ref):
    h = _norm_mod(x_ref[...], g_ref[...], mod_ref[0:1, :], mod_ref[1:2, :]).astype(BF16)
    qk_ref[...] = jnp.dot(h, w_ref[:, 0:1024], preferred_element_type=F32)
    v_ref[...] = jnp.dot(h, w_ref[:, 1024:2048], preferred_element_type=F32).astype(BF16)
    r_ref[...] = jnp.dot(h, w_ref[:, 2048:3072], preferred_element_type=F32)
    gd = jnp.dot(h, wgd_ref[...], preferred_element_type=F32).astype(BF16)
    logit = jnp.dot(gd, wup_ref[...], preferred_element_type=F32) + bg_ref[...]
    log_sig = jnp.minimum(logit, 0.0) - jnp.log(1.0 + jnp.exp(-jnp.abs(logit)))
    gate_ref[...] = log_sig / GATE_TEMP


def _odd_in(x, mod, layer, group_of_step, g1, w_bf, wgd_bf, wup_bf, bg):
    n = x.shape[0]
    tm = 512
    row = lambda i: (i, 0)
    return pl.pallas_call(
        _odd_in_kernel,
        grid=(n // tm,),
        in_specs=[pl.BlockSpec((tm, D_MODEL), row), _mod_spec(layer, group_of_step),
                  _resident((1, D_MODEL)), _resident(w_bf.shape),
                  _resident((D_MODEL, 128)), _resident((128, D_MODEL)), _resident((1, D_MODEL))],
        out_specs=[pl.BlockSpec((tm, D_MODEL), row)] * 4,
        out_shape=[jax.ShapeDtypeStruct((n, D_MODEL), dt) for dt in (F32, BF16, F32, F32)],
        compiler_params=_params(1),
        name="odd_in",
    )(x, mod, g1, w_bf, wgd_bf, wup_bf, bg)


def _split3(x):
    hi = x.astype(BF16)
    r1 = x - hi.astype(F32)
    mid = r1.astype(BF16)
    lo = (r1 - mid.astype(F32)).astype(BF16)
    return hi, mid, lo


def _gla_kernel(*refs, tb, has_init, emit_state):
    refs = list(refs)
    ins = [refs.pop(0) for _ in range(6)]
    init_ref = refs.pop(0) if has_init else None
    o_refs = [refs.pop(0), refs.pop(0)]
    state_ref = refs.pop(0) if emit_state else None
    st = refs.pop(0)
    n = pl.program_id(1)

    @pl.when(n == 0)
    def _init():
        for d in range(2):
            for h in range(GLA_HEADS):
                if has_init:
                    st[d, h] = init_ref[d, h].T
                else:
                    st[d, h] = jnp.zeros((GLA_DV, GLA_DK), F32)

    ri = lax.broadcasted_iota(jnp.int32, (CHUNK, CHUNK), 0)
    ci = lax.broadcasted_iota(jnp.int32, (CHUNK, CHUNK), 1)
    nt = (((1,), (1,)), ((), ()))
    tn = (((0,), (0,)), ((), ()))
    nchunk = tb // CHUNK

    def body(i, carry):
        for d in range(2):
            qk_ref, v_ref, g_ref = ins[3 * d:3 * d + 3]
            o_ref = o_refs[d]
            mask = (ci <= ri) if d == 0 else (ci >= ri)
            tri = jnp.where(mask, 1.0, 0.0).astype(BF16)
            edge = CHUNK - 1 if d == 0 else 0
            c = i if d == 0 else nchunk - 1 - i
            rows = pl.ds(pl.multiple_of(c * CHUNK, CHUNK), CHUNK)
            g = g_ref[rows, :]
            b = sum(jnp.dot(tri, part, preferred_element_type=F32) for part in _split3(g))
            b_edge = b[edge:edge + 1, :]
            grow, fall, tail, decay = jnp.exp(-b), jnp.exp(b), jnp.exp(b_edge - b), jnp.exp(b_edge)
            q = qk_ref[rows, 0:512] * (GLA_DK ** -0.5)
            k = qk_ref[rows, 512:1024]
            q_t = (q * fall).astype(BF16)
            k_t = (k * grow).astype(BF16)
            k_end = (k * tail).astype(BF16)
            for h in range(GLA_HEADS):
                kc = slice(h * GLA_DK, (h + 1) * GLA_DK)
                vc = slice(h * GLA_DV, (h + 1) * GLA_DV)
                vh = v_ref[rows, vc]
                a = lax.dot_general(q_t[:, kc], k_t[:, kc], nt, preferred_element_type=F32)
                a = jnp.where(mask, a, 0.0).astype(BF16)
                s_t = st[d, h]
                o = (jnp.dot(a, vh, preferred_element_type=F32)
                     + lax.dot_general(q_t[:, kc], s_t.astype(BF16), nt, preferred_element_type=F32))
                o_ref[rows, vc] = o
                st[d, h] = (s_t * decay[:, kc]
                            + lax.dot_general(vh, k_end[:, kc], tn, preferred_element_type=F32))
        return carry

    lax.fori_loop(0, nchunk, body, 0)

    if emit_state:
        @pl.when(n == pl.num_programs(1) - 1)
        def _emit():
            for d in range(2):
                for h in range(GLA_HEADS):
                    state_ref[d, h] = st[d, h].T


def _gla(qk, v, gate, init, seq, emit_state):
    n = qk.shape[0]
    nseq = n // seq
    tb = 256
    nsb = seq // tb
    fwd = lambda b, s: (b * nsb + s, 0)
    bwd = lambda b, s: (b * nsb + nsb - 1 - s, 0)
    bwd_gate = lambda b, s: (b * nsb + nsb - 1 - s, 1)
    in_specs = [pl.BlockSpec((tb, 1024), fwd), pl.BlockSpec((tb, 1024), fwd), pl.BlockSpec((tb, 512), fwd),
                pl.BlockSpec((tb, 1024), bwd), pl.BlockSpec((tb, 1024), bwd), pl.BlockSpec((tb, 512), bwd_gate)]
    args = [qk, v, gate, qk, v, gate]
    if init is not None:
        state, il = init
        in_specs.append(pl.BlockSpec((None, None, 2, GLA_HEADS, GLA_DK, GLA_DV),
                                     lambda b, s: (b, il, 0, 0, 0, 0)))
        args.append(state)
    out_specs = [pl.BlockSpec((tb, 1024), fwd), pl.BlockSpec((tb, 1024), bwd)]
    out_shape = [jax.ShapeDtypeStruct((n, 1024), F32)] * 2
    if emit_state:
        out_specs.append(pl.BlockSpec((None, 2, GLA_HEADS, GLA_DK, GLA_DV), lambda b, s: (b, 0, 0, 0, 0)))
        out_shape.append(jax.ShapeDtypeStruct((nseq, 2, GLA_HEADS, GLA_DK, GLA_DV), F32))
    return pl.pallas_call(
        functools.partial(_gla_kernel, tb=tb, has_init=init is not None, emit_state=emit_state),
        grid=(nseq, nsb),
        in_specs=in_specs,
        out_specs=out_specs,
        out_shape=out_shape,
        scratch_shapes=[pltpu.VMEM((2, GLA_HEADS, GLA_DV, GLA_DK), F32)],
        compiler_params=_params(2),
        name="gla",
    )(*args)


def _gla_out_kernel(x_ref, of_ref, ob_ref, r_ref, mod_ref, gn_ref, w_ref, o_ref):
    o = of_ref[...] + ob_ref[...]
    parts = []
    for h in range(GLA_HEADS):
        cols = slice(h * GLA_DV, (h + 1) * GLA_DV)
        oh = o[:, cols]
        oh = oh * lax.rsqrt(jnp.mean(oh * oh, axis=-1, keepdims=True) + EPS)
        parts.append(oh * gn_ref[:, cols])
    z = (jnp.concatenate(parts, axis=1) * _silu(r_ref[...])).astype(BF16)
    out = jnp.dot(z, w_ref[...], preferred_element_type=F32)
    o_ref[...] = x_ref[...] + mod_ref[2:3, :] * out


def _gla_out(x, o_f, o_b, r, mod, layer, group_of_step, gn, w_bf):
    n = x.shape[0]
    tm = 512
    row = lambda i: (i, 0)
    tile = pl.BlockSpec((tm, D_MODEL), row)
    return pl.pallas_call(
        _gla_out_kernel,
        grid=(n // tm,),
        in_specs=[tile, tile, tile, tile, _mod_spec(layer, group_of_step),
                  _resident((1, D_MODEL)), _resident((D_MODEL, D_MODEL))],
        out_specs=tile,
        out_shape=jax.ShapeDtypeStruct((n, D_MODEL), F32),
        compiler_params=_params(1),
        name="gla_out",
    )(x, o_f, o_b, r, mod, gn, w_bf)


def _final_kernel(x_ref, g_ref, o_ref):
    x = x_ref[...]
    o_ref[...] = (x * lax.rsqrt(jnp.mean(x * x, axis=-1, keepdims=True) + EPS)) * g_ref[...]


def _final_norm(x, g):
    n = x.shape[0]
    tm = 1024
    row = lambda i: (i, 0)
    return pl.pallas_call(
        _final_kernel,
        grid=(n // tm,),
        in_specs=[pl.BlockSpec((tm, D_MODEL), row), _resident((1, D_MODEL))],
        out_specs=pl.BlockSpec((tm, D_MODEL), row),
        out_shape=jax.ShapeDtypeStruct((n, D_MODEL), F32),
        compiler_params=_params(1),
        name="final_norm",
    )(x, g)


def _rope_tables(n_lat):
    half = ATT_DH // 2
    inv = 1.0 / (ROPE_BASE ** (jnp.arange(0, half, 2, dtype=F32) / half))
    t = jnp.arange(n_lat)
    rows = (t // GRID_W).astype(F32)
    cols = (t % GRID_W).astype(F32)

    def tab(pos):
        ang = pos[:, None] * inv[None, :]
        ang = jnp.concatenate([ang, ang], axis=-1)
        return jnp.cos(ang), jnp.sin(ang)

    (cr, sr), (cc, sc) = tab(rows), tab(cols)
    sign = jnp.concatenate([-jnp.ones((half // 2,), F32), jnp.ones((half // 2,), F32)])
    cos = jnp.concatenate([cr, cc], axis=-1)
    sin = jnp.concatenate([sr * sign, sc * sign], axis=-1)
    return jnp.tile(cos, (1, 2)), jnp.tile(sin, (1, 2))


def kernel(x_prompt, x_sample, c, cache_attn_k, cache_attn_v, state_gla, c_ctx, norm1_g, norm2_g, w_mod, b_mod, w_in_even, lam_params, subln_g, w_pool, pool_scale, w_out_even, w_in_odd, w_gate_up, b_gate, gla_norm_g, w_out_odd, w_ffn_in, w_ffn_out, final_g):
    bp, tp, _ = x_prompt.shape
    bs, ts, _ = x_sample.shape
    past = cache_attn_k.shape[2]
    n_even = cache_attn_k.shape[1]
    assert bs + 1 <= MOD_ROWS

    xp = x_prompt.reshape(bp * tp, D_MODEL)
    xs = x_sample.reshape(bs * ts, D_MODEL)
    ck = cache_attn_k.reshape(bs, n_even, past, ATT_WIDTH)
    cv = cache_attn_v.reshape(bs, n_even, past, ATT_WIDTH)

    cvec = jnp.concatenate([c_ctx[None, :], c, jnp.zeros((MOD_ROWS - 1 - bs, D_MODEL), F32)], axis=0)
    mod = _adaln(cvec, w_mod, b_mod)
    ctx_group = lambda i: 0

    def lat_group(tm):
        return lambda i: 1 + (i * tm) // ts

    rope = _rope_tables(ts)

    ks_new, vs_new, ss_new = [], [], []
    for l in range(DEPTH):
        i = l // 2
        g1 = norm1_g[l].reshape(1, D_MODEL)
        g2 = norm2_g[l].reshape(1, D_MODEL)
        if l % 2 == 0:
            lam_init = 0.8 - 0.6 * math.exp(-0.3 * l)
            w_in = w_in_even[i].astype(BF16)
            w_out = w_out_even[i].astype(BF16)
            wpool = w_pool[i].astype(BF16)
            subg = subln_g[i].reshape(1, ATT_WIDTH)
            pscale = pool_scale[i].reshape(1, POOL_WIDTH)
            lamp = lam_params[i]
            q, k, v, u = _even_in(xp, mod, l, ctx_group, g1, w_in, None, tp, F32)
            ks_new.append(k.reshape(bp, tp, ATT_WIDTH))
            vs_new.append(v.reshape(bp, tp, ATT_WIDTH))
            xp = _even_mix(xp, q, k, v, u, None, mod, l, ctx_group, lamp, subg, wpool, pscale,
                           w_out, tp, lam_init)
            q, k, v, u = _even_in(xs, mod, l, lat_group(1024), g1, w_in, rope, ts, BF16)
            xs = _even_mix(xs, q, k, v, u, (ck, cv, i), mod, l, lambda b: 1 + b, lamp, subg, wpool,
                           pscale, w_out, ts, lam_init)
        else:
            w_main = w_in_odd[i].astype(BF16)
            wgd = jnp.pad(w_in_odd[i][:, 3072:], ((0, 0), (0, 128 - 2 * GATE_RANK))).astype(BF16)
            wup = jnp.zeros((128, D_MODEL), F32)
            wup = wup.at[0:GATE_RANK, 0:512].set(w_gate_up[i, 0])
            wup = wup.at[GATE_RANK:2 * GATE_RANK, 512:1024].set(w_gate_up[i, 1]).astype(BF16)
            bg = b_gate[i].reshape(1, D_MODEL)
            w_out = w_out_odd[i].astype(BF16)
            gn = gla_norm_g[i].reshape(1, D_MODEL)
            qk, v, r, gate = _odd_in(xp, mod, l, ctx_group, g1, w_main, wgd, wup, bg)
            o_f, o_b, s_ctx = _gla(qk, v, gate, None, tp, True)
            ss_new.append(s_ctx)
            xp = _gla_out(xp, o_f, o_b, r, mod, l, ctx_group, gn, w_out)
            qk, v, r, gate = _odd_in(xs, mod, l, lat_group(512), g1, w_main, wgd, wup, bg)
            o_f, o_b = _gla(qk, v, gate, (state_gla, i), ts, False)
            xs = _gla_out(xs, o_f, o_b, r, mod, l, lat_group(512), gn, w_out)
        wi = w_ffn_in[l].astype(BF16)
        wo = w_ffn_out[l].astype(BF16)
        xp = _ffn(xp, mod, l, ctx_group, g2, wi, wo)
        xs = _ffn(xs, mod, l, lat_group(512), g2, wi, wo)

    fg = final_g.reshape(1, D_MODEL)
    y_prompt = _final_norm(xp, fg).reshape(bp, tp, D_MODEL)
    y_sample = _final_norm(xs, fg).reshape(bs, ts, D_MODEL)
    new_attn_k = jnp.stack(ks_new, axis=1).reshape(bp, n_even, tp, ATT_HEADS, 2, ATT_DH)
    new_attn_v = jnp.stack(vs_new, axis=1).reshape(bp, n_even, tp, ATT_HEADS, ATT_DV)
    new_gla_state = jnp.stack(ss_new, axis=1)
    return (y_prompt, y_sample, new_attn_k, new_attn_v, new_gla_state)
```

```python
import functools
import math

import jax
import jax.numpy as jnp
from jax import lax
from jax.experimental import pallas as pl
from jax.experimental.pallas import tpu as pltpu

F32 = jnp.float32
BF16 = jnp.bfloat16

D_MODEL = 1024
DEPTH = 4
GRID_W = 64
ATT_WIDTH = 512
POOL_WIDTH = 512
ATT_HEADS = 4
ATT_DH = 64
ATT_DV = 128
ROPE_BASE = 10000.0
POOL_WINDOWS = (2, 4, 8, 16)
POOL_GROUP = 128
POOL_PAD = 8
GLA_HEADS = 4
GLA_DK = 128
GLA_DV = 256
GATE_RANK = 16
GATE_TEMP = 16.0
CHUNK = 64
FFN_HIDDEN = 2816
FFN_CHUNK = 256
EPS = 1e-6
Q_SCALE = ATT_DH ** -0.5 * math.log2(math.e)
MOD_ROWS = 16

V7X_VMEM_LIMIT = 56 * 1024 * 1024


def _params(n_axes, vmem_bytes=V7X_VMEM_LIMIT):
    return pltpu.CompilerParams(dimension_semantics=("arbitrary",) * n_axes,
                                vmem_limit_bytes=vmem_bytes)


def _resident(shape):
    nd = len(shape)
    return pl.BlockSpec(shape, lambda *_: (0,) * nd, pipeline_mode=pl.Buffered(1))


def _norm_mod(x, g, shift, scale):
    y = x * lax.rsqrt(jnp.mean(x * x, axis=-1, keepdims=True) + EPS)
    return (y * g) * (1.0 + scale) + shift


def _silu(x):
    return x * jax.nn.sigmoid(x)


def _adaln_kernel(cv_ref, w_ref, b_ref, o_ref):
    s = _silu(cv_ref[...]).astype(BF16)
    o_ref[...] = jnp.dot(s, w_ref[...].astype(BF16), preferred_element_type=F32) + b_ref[...]


def _adaln(cvec, w_mod, b_mod):
    tn = 1536
    out = pl.pallas_call(
        _adaln_kernel,
        grid=(DEPTH, 6 * D_MODEL // tn),
        in_specs=[pl.BlockSpec((MOD_ROWS, D_MODEL), lambda l, j: (0, 0)),
                  pl.BlockSpec((None, D_MODEL, tn), lambda l, j: (l, 0, j)),
                  pl.BlockSpec((None, 1, tn), lambda l, j: (l, 0, j))],
        out_specs=pl.BlockSpec((None, MOD_ROWS, tn), lambda l, j: (l, 0, j)),
        out_shape=jax.ShapeDtypeStruct((DEPTH, MOD_ROWS, 6 * D_MODEL), F32),
        compiler_params=_params(2),
        name="adaln",
    )(cvec, w_mod, b_mod.reshape(DEPTH, 1, 6 * D_MODEL))
    return out.reshape(DEPTH, MOD_ROWS, 6, D_MODEL)


def _mod_spec(layer, group_of_step):
    return pl.BlockSpec((None, None, 6, D_MODEL),
                        lambda *ids: (layer, group_of_step(*ids), 0, 0))


def _even_in_kernel(*refs, tm, rc, rope):
    if rope:
        x_ref, mod_ref, g_ref, w_ref, cos_ref, sin_ref, q_ref, k_ref, v_ref, u_ref = refs
    else:
        x_ref, mod_ref, g_ref, w_ref, q_ref, k_ref, v_ref, u_ref = refs
    shift, scale, g = mod_ref[0:1, :], mod_ref[1:2, :], g_ref[...]
    lo = (lax.broadcasted_iota(jnp.int32, (rc, 128), 1) % 32) < 16

    def rotary(y, cos, sin):
        parts = []
        for h in range(ATT_HEADS):
            yh = y[:, h * 128:(h + 1) * 128]
            fwd = pltpu.roll(yh, 128 - 16, axis=1)
            bwd = pltpu.roll(yh, 16, axis=1)
            parts.append(yh * cos + jnp.where(lo, fwd, bwd) * sin)
        return jnp.concatenate(parts, axis=1)

    for r in range(tm // rc):
        rows = slice(r * rc, (r + 1) * rc)
        h = _norm_mod(x_ref[rows, :], g, shift, scale).astype(BF16)
        q = jnp.dot(h, w_ref[:, 0:512], preferred_element_type=F32)
        k = jnp.dot(h, w_ref[:, 512:1024], preferred_element_type=F32)
        if rope:
            cos, sin = cos_ref[rows, :], sin_ref[rows, :]
            q, k = rotary(q, cos, sin), rotary(k, cos, sin)
        q_ref[rows, :] = (q * Q_SCALE).astype(q_ref.dtype)
        k_ref[rows, :] = k.astype(k_ref.dtype)
        v_ref[rows, :] = jnp.dot(h, w_ref[:, 1024:1536], preferred_element_type=F32).astype(v_ref.dtype)
        u_ref[rows, :] = jnp.dot(h, w_ref[:, 1536:2048], preferred_element_type=F32)


def _even_in(x, mod, layer, group_of_step, g1, w_bf, rope_tabs, seq_len, kv_dtype):
    n = x.shape[0]
    tm, rc = 1024, 512
    row = lambda i: (i, 0)
    in_specs = [pl.BlockSpec((tm, D_MODEL), row), _mod_spec(layer, group_of_step),
                _resident((1, D_MODEL)), _resident((D_MODEL, 2048))]
    args = [x, mod, g1, w_bf]
    if rope_tabs is not None:
        per_seq = seq_len // tm
        pos = lambda i: (i % per_seq, 0)
        in_specs += [pl.BlockSpec((tm, 128), pos), pl.BlockSpec((tm, 128), pos)]
        args += list(rope_tabs)
    out_dt = (BF16, kv_dtype, kv_dtype, F32)
    return pl.pallas_call(
        functools.partial(_even_in_kernel, tm=tm, rc=rc, rope=rope_tabs is not None),
        grid=(n // tm,),
        in_specs=in_specs,
        out_specs=[pl.BlockSpec((tm, 512), row)] * 4,
        out_shape=[jax.ShapeDtypeStruct((n, 512), dt) for dt in out_dt],
        compiler_params=_params(1),
        name="even_in",
    )(*args)


def _even_mix_kernel(*refs, seq, tq, n_cache, lam_init):
    if n_cache:
        (x_ref, q_ref, k_ref, v_ref, u_ref, ck_ref, cv_ref, mod_ref, lamp_ref, subg_ref,
         wpool_ref, pscale_ref, wout_ref, o_ref, upad, ypool) = refs
    else:
        (x_ref, q_ref, k_ref, v_ref, u_ref, mod_ref, lamp_ref, subg_ref,
         wpool_ref, pscale_ref, wout_ref, o_ref, upad, ypool) = refs
    qi = pl.program_id(1)

    @pl.when(qi == 0)
    def _pool():
        zeros = jnp.zeros((POOL_PAD, POOL_WIDTH), F32)
        upad[0:POOL_PAD, :] = zeros
        upad[POOL_PAD + seq:POOL_PAD + seq + POOL_PAD, :] = zeros
        upad[POOL_PAD:POOL_PAD + seq, :] = u_ref[...]
        n_pad = seq + 2 * POOL_PAD
        t = lax.broadcasted_iota(jnp.int32, (seq, POOL_GROUP), 0)
        for g, w in enumerate(POOL_WINDOWS):
            cols = slice(g * POOL_GROUP, (g + 1) * POOL_GROUP)
            xg = upad[:, cols]
            p = xg + pltpu.roll(xg, 1, axis=0)
            half = 1
            while 2 * half < w:
                p = pltpu.roll(p, half, axis=0) + pltpu.roll(p, n_pad - half, axis=0)
                half *= 2
            cnt = (jnp.minimum(t - w // 2 + w, seq) - jnp.maximum(t - w // 2, 0)).astype(F32)
            y = p[POOL_PAD:POOL_PAD + seq] / cnt - xg[POOL_PAD:POOL_PAD + seq]
            ypool[:, cols] = y.astype(BF16)

    lp = lamp_ref[...]
    lam = (jnp.exp(jnp.sum(lp[0:1, :] * lp[1:2, :], axis=-1, keepdims=True))
           - jnp.exp(jnp.sum(lp[2:3, :] * lp[3:4, :], axis=-1, keepdims=True)) + lam_init)

    lane = lax.broadcasted_iota(jnp.int32, (1, 128), 1)
    m1 = jnp.where(lane < ATT_DH, 1.0, 0.0).astype(BF16)
    m2 = jnp.where(lane >= ATT_DH, 1.0, 0.0).astype(BF16)
    nt = (((1,), (1,)), ((), ()))

    def scores(h):
        cols = slice(h * 128, (h + 1) * 128)
        qh = q_ref[:, cols]
        q2 = jnp.concatenate([qh * m1, qh * m2], axis=0)
        s = lax.dot_general(q2, k_ref[:, cols].astype(BF16), nt, preferred_element_type=F32)
        sc = None
        if n_cache:
            sc = lax.dot_general(q2, ck_ref[:, cols].astype(BF16), nt, preferred_element_type=F32)
        return s, sc

    heads = []
    pending = scores(0)
    for h in range(ATT_HEADS):
        cols = slice(h * 128, (h + 1) * 128)
        s, sc = pending
        if h + 1 < ATT_HEADS:
            pending = scores(h + 1)
        vh = v_ref[:, cols].astype(BF16)
        mx = jnp.max(s, axis=-1, keepdims=True)
        if n_cache:
            cvh = cv_ref[:, cols].astype(BF16)
            mx = jnp.maximum(mx, jnp.max(sc, axis=-1, keepdims=True))
        e = jnp.exp2(s - mx)
        den = jnp.sum(e, axis=-1, keepdims=True)
        if n_cache:
            ec = jnp.exp2(sc - mx)
            den = den + jnp.sum(ec, axis=-1, keepdims=True)
        inv1 = 1.0 / den[:tq]
        ratio = lam * den[:tq] / den[tq:]
        a = (e[:tq] - ratio * e[tq:]).astype(BF16)
        o = jnp.dot(a, vh, preferred_element_type=F32)
        if n_cache:
            ac = (ec[:tq] - ratio * ec[tq:]).astype(BF16)
            o = o + jnp.dot(ac, cvh, preferred_element_type=F32)
        o = o * inv1
        o = o * lax.rsqrt(jnp.mean(o * o, axis=-1, keepdims=True) + EPS)
        heads.append((o * subg_ref[:, cols]) * (1.0 - lam_init))
    o_att = jnp.concatenate(heads, axis=1).astype(BF16)

    yq = ypool[pl.ds(pl.multiple_of(qi * tq, tq), tq), :]
    pooled = [jnp.dot(yq[:, g * POOL_GROUP:(g + 1) * POOL_GROUP], wpool_ref[g],
                      preferred_element_type=F32) for g in range(len(POOL_WINDOWS))]
    p = (jnp.concatenate(pooled, axis=1) * pscale_ref[...]).astype(BF16)

    out = (jnp.dot(o_att, wout_ref[0:ATT_WIDTH, :], preferred_element_type=F32)
           + jnp.dot(p, wout_ref[ATT_WIDTH:, :], preferred_element_type=F32))
    o_ref[...] = x_ref[...] + mod_ref[2:3, :] * out


def _even_mix(x, q, k, v, u, cache, mod, layer, group_of_seq, lamp, subg, wpool_bf, pscale,
              wout_bf, seq, lam_init):
    n = x.shape[0]
    nseq = n // seq
    tq = 256
    nq = seq // tq
    blk = lambda b, qi: (b * nq + qi, 0)
    whole = lambda b, qi: (b, 0)
    in_specs = [pl.BlockSpec((tq, D_MODEL), blk), pl.BlockSpec((tq, 512), blk),
                pl.BlockSpec((seq, 512), whole), pl.BlockSpec((seq, 512), whole),
                pl.BlockSpec((seq, 512), whole)]
    args = [x, q, k, v, u]
    n_cache = 0
    if cache is not None:
        ck, cv, il = cache
        n_cache = ck.shape[2]
        cspec = pl.BlockSpec((None, None, n_cache, 512), lambda b, qi: (b, il, 0, 0))
        in_specs += [cspec, cspec]
        args += [ck, cv]
    in_specs += [_mod_spec(layer, lambda b, qi: group_of_seq(b)),
                 _resident((4, ATT_DH)), _resident((1, ATT_WIDTH)),
                 _resident((len(POOL_WINDOWS), POOL_GROUP, POOL_GROUP)),
                 _resident((1, POOL_WIDTH)), _resident((D_MODEL, D_MODEL))]
    args += [mod, lamp, subg, wpool_bf, pscale, wout_bf]
    return pl.pallas_call(
        functools.partial(_even_mix_kernel, seq=seq, tq=tq, n_cache=n_cache, lam_init=lam_init),
        grid=(nseq, nq),
        in_specs=in_specs,
        out_specs=pl.BlockSpec((tq, D_MODEL), blk),
        out_shape=jax.ShapeDtypeStruct((n, D_MODEL), F32),
        scratch_shapes=[pltpu.VMEM((seq + 2 * POOL_PAD, POOL_WIDTH), F32),
                        pltpu.VMEM((seq, POOL_WIDTH), BF16)],
        compiler_params=_params(2),
        name="even_mix",
    )(*args)


def _ffn_kernel(*refs, gla_pre, final):
    refs = list(refs)
    x_ref, mod_ref = refs.pop(0), refs.pop(0)
    if gla_pre:
        og_ref, r_ref, gn_ref, wmix_ref = (refs.pop(0) for _ in range(4))
    g_ref, wi_ref, wo_ref = (refs.pop(0) for _ in range(3))
    fg_ref = refs.pop(0) if final else None
    o_ref, acc_ref = refs
    x = x_ref[...]
    if gla_pre:
        parts = []
        for h in range(GLA_HEADS):
            cols = slice(h * GLA_DV, (h + 1) * GLA_DV)
            oh = og_ref[:, cols]
            oh = oh * lax.rsqrt(jnp.mean(oh * oh, axis=-1, keepdims=True) + EPS)
            parts.append(oh * gn_ref[:, cols])
        z = (jnp.concatenate(parts, axis=1) * _silu(r_ref[...])).astype(BF16)
        x = x + mod_ref[2:3, :] * jnp.dot(z, wmix_ref[...], preferred_element_type=F32)
    h = _norm_mod(x, g_ref[...], mod_ref[3:4, :], mod_ref[4:5, :]).astype(BF16)
    for c in range(FFN_HIDDEN // FFN_CHUNK):
        lo, hi = c * FFN_CHUNK, (c + 1) * FFN_CHUNK
        a = jnp.dot(h, wi_ref[:, lo:hi], preferred_element_type=F32)
        b = jnp.dot(h, wi_ref[:, FFN_HIDDEN + lo:FFN_HIDDEN + hi], preferred_element_type=F32)
        z = (_silu(a) * b).astype(BF16)
        part = jnp.dot(z, wo_ref[lo:hi, :], preferred_element_type=F32)
        if c == 0:
            acc_ref[...] = part
        else:
            acc_ref[...] += part
    y = x + mod_ref[5:6, :] * acc_ref[...]
    if final:
        y = (y * lax.rsqrt(jnp.mean(y * y, axis=-1, keepdims=True) + EPS)) * fg_ref[...]
    o_ref[...] = y


def _ffn(x, mod, layer, group_of_step, g2, wi_bf, wo_bf, gla_pre=None, final_g=None):
    n = x.shape[0]
    tm = 512
    row = lambda i: (i, 0)
    tile = pl.BlockSpec((tm, D_MODEL), row)
    in_specs = [tile, _mod_spec(layer, group_of_step)]
    args = [x, mod]
    if gla_pre is not None:
        o, r, gn, wmix_bf = gla_pre
        in_specs += [tile, tile, _resident((1, D_MODEL)), _resident((D_MODEL, D_MODEL))]
        args += [o, r, gn, wmix_bf]
    in_specs += [_resident((1, D_MODEL)), _resident((D_MODEL, 2 * FFN_HIDDEN)),
                 _resident((FFN_HIDDEN, D_MODEL))]
    args += [g2, wi_bf, wo_bf]
    if final_g is not None:
        in_specs.append(_resident((1, D_MODEL)))
        args.append(final_g)
    return pl.pallas_call(
        functools.partial(_ffn_kernel, gla_pre=gla_pre is not None, final=final_g is not None),
        grid=(n // tm,),
        in_specs=in_specs,
        out_specs=tile,
        out_shape=jax.ShapeDtypeStruct((n, D_MODEL), F32),
        scratch_shapes=[pltpu.VMEM((tm, D_MODEL), F32)],
        compiler_params=_params(1),
        name="ffn",
    )(*args)


def _chunk_scan(x, reverse):
    n = x.shape[0]
    pos = lax.broadcasted_iota(jnp.int32, x.shape, 0) % CHUNK
    step = 1
    while step < CHUNK:
        if reverse:
            x = x + jnp.where(pos < CHUNK - step, pltpu.roll(x, n - step, axis=0), 0.0)
        else:
            x = x + jnp.where(pos >= step, pltpu.roll(x, step, axis=0), 0.0)
        step *= 2
    return x


def _odd_in_kernel(x_ref, mod_ref, g_ref, w_ref, wgd_ref, wup_ref, bg_ref,
                   qk_ref, v_ref, r_ref, gate_ref, *, tm, rc):
    shift, scale, g = mod_ref[0:1, :], mod_ref[1:2, :], g_ref[...]
    for c in range(tm // rc):
        rows = slice(c * rc, (c + 1) * rc)
        h = _norm_mod(x_ref[rows, :], g, shift, scale).astype(BF16)
        gd = jnp.dot(h, wgd_ref[...], preferred_element_type=F32).astype(BF16)
        logit = jnp.dot(gd, wup_ref[...], preferred_element_type=F32) + bg_ref[...]
        qk_ref[rows, :] = jnp.dot(h, w_ref[:, 0:1024], preferred_element_type=F32)
        v_ref[rows, :] = jnp.dot(h, w_ref[:, 1024:2048], preferred_element_type=F32).astype(BF16)
        r_ref[rows, :] = jnp.dot(h, w_ref[:, 2048:3072], preferred_element_type=F32)
        log_sig = jnp.minimum(logit, 0.0) - jnp.log(1.0 + jnp.exp(-jnp.abs(logit)))
        gate_ref[rows, :] = log_sig / GATE_TEMP


def _odd_in(x, mod, layer, group_of_step, g1, w_bf, wgd_bf, wup_bf, bg):
    n = x.shape[0]
    tm, rc = 1024, 512
    row = lambda i: (i, 0)
    return pl.pallas_call(
        functools.partial(_odd_in_kernel, tm=tm, rc=rc),
        grid=(n // tm,),
        in_specs=[pl.BlockSpec((tm, D_MODEL), row), _mod_spec(layer, group_of_step),
                  _resident((1, D_MODEL)), _resident(w_bf.shape),
                  _resident((D_MODEL, 128)), _resident((128, D_MODEL)), _resident((1, D_MODEL))],
        out_specs=[pl.BlockSpec((tm, D_MODEL), row)] * 4,
        out_shape=[jax.ShapeDtypeStruct((n, D_MODEL), dt) for dt in (F32, BF16, F32, F32)],
        compiler_params=_params(1),
        name="odd_in",
    )(x, mod, g1, w_bf, wgd_bf, wup_bf, bg)


def _gla_kernel(*refs, tb, nsb, has_init, emit_state):
    refs = list(refs)
    ins = [refs.pop(0) for _ in range(6)]
    init_ref = refs.pop(0) if has_init else None
    o_ref = refs.pop(0)
    state_ref = refs.pop(0) if emit_state else None
    st = refs.pop(0)
    n = pl.program_id(1)

    @pl.when(n == 0)
    def _init():
        o_ref[...] = jnp.zeros(o_ref.shape, F32)
        for d in range(2):
            for h in range(GLA_HEADS):
                if has_init:
                    st[d, h] = init_ref[d, h].T
                else:
                    st[d, h] = jnp.zeros((GLA_DV, GLA_DK), F32)

    ri = lax.broadcasted_iota(jnp.int32, (CHUNK, CHUNK), 0)
    ci = lax.broadcasted_iota(jnp.int32, (CHUNK, CHUNK), 1)
    nt = (((1,), (1,)), ((), ()))
    tn = (((0,), (0,)), ((), ()))
    nchunk = tb // CHUNK

    def body(i, carry):
        for d in range(2):
            qk_ref, v_ref, g_ref = ins[3 * d:3 * d + 3]
            mask = (ci <= ri) if d == 0 else (ci >= ri)
            edge = CHUNK - 1 if d == 0 else 0
            c = i if d == 0 else nchunk - 1 - i
            block = n if d == 0 else nsb - 1 - n
            rows = pl.ds(pl.multiple_of(c * CHUNK, CHUNK), CHUNK)
            orows = pl.ds(pl.multiple_of(block * tb + c * CHUNK, CHUNK), CHUNK)
            b = _chunk_scan(g_ref[rows, :], reverse=d == 1)
            b_edge = b[edge:edge + 1, :]
            grow, fall, tail, decay = jnp.exp(-b), jnp.exp(b), jnp.exp(b_edge - b), jnp.exp(b_edge)
            q = qk_ref[rows, 0:512] * (GLA_DK ** -0.5)
            k = qk_ref[rows, 512:1024]
            q_t = (q * fall).astype(BF16)
            k_t = (k * grow).astype(BF16)
            k_end = (k * tail).astype(BF16)
            for h in range(GLA_HEADS):
                kc = slice(h * GLA_DK, (h + 1) * GLA_DK)
                vc = slice(h * GLA_DV, (h + 1) * GLA_DV)
                vh = v_ref[rows, vc]
                a = lax.dot_general(q_t[:, kc], k_t[:, kc], nt, preferred_element_type=F32)
                a = jnp.where(mask, a, 0.0).astype(BF16)
                s_t = st[d, h]
                o = (jnp.dot(a, vh, preferred_element_type=F32)
                     + lax.dot_general(q_t[:, kc], s_t.astype(BF16), nt, preferred_element_type=F32))
                o_ref[orows, vc] += o
                st[d, h] = (s_t * decay[:, kc]
                            + lax.dot_general(vh, k_end[:, kc], tn, preferred_element_type=F32))
        return carry

    lax.fori_loop(0, nchunk, body, 0)

    if emit_state:
        @pl.when(n == pl.num_programs(1) - 1)
        def _emit():
            for d in range(2):
                for h in range(GLA_HEADS):
                    state_ref[d, h] = st[d, h].T


def _gla(qk, v, gate, init, seq, emit_state):
    n = qk.shape[0]
    nseq = n // seq
    tb = 256
    nsb = seq // tb
    fwd = lambda b, s: (b * nsb + s, 0)
    bwd = lambda b, s: (b * nsb + nsb - 1 - s, 0)
    bwd_gate = lambda b, s: (b * nsb + nsb - 1 - s, 1)
    in_specs = [pl.BlockSpec((tb, 1024), fwd), pl.BlockSpec((tb, 1024), fwd), pl.BlockSpec((tb, 512), fwd),
                pl.BlockSpec((tb, 1024), bwd), pl.BlockSpec((tb, 1024), bwd), pl.BlockSpec((tb, 512), bwd_gate)]
    args = [qk, v, gate, qk, v, gate]
    if init is not None:
        state, il = init
        in_specs.append(pl.BlockSpec((None, None, 2, GLA_HEADS, GLA_DK, GLA_DV),
                                     lambda b, s: (b, il, 0, 0, 0, 0)))
        args.append(state)
    out_specs = [pl.BlockSpec((seq, 1024), lambda b, s: (b, 0))]
    out_shape = [jax.ShapeDtypeStruct((n, 1024), F32)]
    if emit_state:
        out_specs.append(pl.BlockSpec((None, 2, GLA_HEADS, GLA_DK, GLA_DV), lambda b, s: (b, 0, 0, 0, 0)))
        out_shape.append(jax.ShapeDtypeStruct((nseq, 2, GLA_HEADS, GLA_DK, GLA_DV), F32))
    return pl.pallas_call(
        functools.partial(_gla_kernel, tb=tb, nsb=nsb, has_init=init is not None, emit_state=emit_state),
        grid=(nseq, nsb),
        in_specs=in_specs,
        out_specs=out_specs,
        out_shape=out_shape,
        scratch_shapes=[pltpu.VMEM((2, GLA_HEADS, GLA_DV, GLA_DK), F32)],
        compiler_params=_params(2),
        name="gla",
    )(*args)


def _rope_tables(n_lat):
    half = ATT_DH // 2
    inv = 1.0 / (ROPE_BASE ** (jnp.arange(0, half, 2, dtype=F32) / half))
    t = jnp.arange(n_lat)
    rows = (t // GRID_W).astype(F32)
    cols = (t % GRID_W).astype(F32)

    def tab(pos):
        ang = pos[:, None] * inv[None, :]
        ang = jnp.concatenate([ang, ang], axis=-1)
        return jnp.cos(ang), jnp.sin(ang)

    (cr, sr), (cc, sc) = tab(rows), tab(cols)
    sign = jnp.concatenate([-jnp.ones((half // 2,), F32), jnp.ones((half // 2,), F32)])
    cos = jnp.concatenate([cr, cc], axis=-1)
    sin = jnp.concatenate([sr * sign, sc * sign], axis=-1)
    return jnp.tile(cos, (1, 2)), jnp.tile(sin, (1, 2))


def kernel(x_prompt, x_sample, c, cache_attn_k, cache_attn_v, state_gla, c_ctx, norm1_g, norm2_g, w_mod, b_mod, w_in_even, lam_params, subln_g, w_pool, pool_scale, w_out_even, w_in_odd, w_gate_up, b_gate, gla_norm_g, w_out_odd, w_ffn_in, w_ffn_out, final_g):
    bp, tp, _ = x_prompt.shape
    bs, ts, _ = x_sample.shape
    past = cache_attn_k.shape[2]
    n_even = cache_attn_k.shape[1]
    assert bs + 1 <= MOD_ROWS

    xp = x_prompt.reshape(bp * tp, D_MODEL)
    xs = x_sample.reshape(bs * ts, D_MODEL)
    ck = cache_attn_k.reshape(bs, n_even, past, ATT_WIDTH)
    cv = cache_attn_v.reshape(bs, n_even, past, ATT_WIDTH)

    cvec = jnp.concatenate([c_ctx[None, :], c, jnp.zeros((MOD_ROWS - 1 - bs, D_MODEL), F32)], axis=0)
    mod = _adaln(cvec, w_mod, b_mod)
    ctx_group = lambda i: 0

    def lat_group(tm):
        return lambda i: 1 + (i * tm) // ts

    rope = _rope_tables(ts)

    fg = final_g.reshape(1, D_MODEL)
    ks_new, vs_new, ss_new = [], [], []
    for l in range(DEPTH):
        i = l // 2
        g1 = norm1_g[l].reshape(1, D_MODEL)
        g2 = norm2_g[l].reshape(1, D_MODEL)
        last = fg if l == DEPTH - 1 else None
        pre_p = pre_s = None
        if l % 2 == 0:
            lam_init = 0.8 - 0.6 * math.exp(-0.3 * l)
            w_in = w_in_even[i].astype(BF16)
            w_out = w_out_even[i].astype(BF16)
            wpool = w_pool[i].astype(BF16)
            subg = subln_g[i].reshape(1, ATT_WIDTH)
            pscale = pool_scale[i].reshape(1, POOL_WIDTH)
            lamp = lam_params[i]
            q, k, v, u = _even_in(xp, mod, l, ctx_group, g1, w_in, None, tp, F32)
            ks_new.append(k.reshape(bp, tp, ATT_WIDTH))
            vs_new.append(v.reshape(bp, tp, ATT_WIDTH))
            xp = _even_mix(xp, q, k, v, u, None, mod, l, ctx_group, lamp, subg, wpool, pscale,
                           w_out, tp, lam_init)
            q, k, v, u = _even_in(xs, mod, l, lat_group(1024), g1, w_in, rope, ts, BF16)
            xs = _even_mix(xs, q, k, v, u, (ck, cv, i), mod, l, lambda b: 1 + b, lamp, subg, wpool,
                           pscale, w_out, ts, lam_init)
        else:
            w_main = w_in_odd[i].astype(BF16)
            wgd = jnp.pad(w_in_odd[i][:, 3072:], ((0, 0), (0, 128 - 2 * GATE_RANK))).astype(BF16)
            wup = jnp.zeros((128, D_MODEL), F32)
            wup = wup.at[0:GATE_RANK, 0:512].set(w_gate_up[i, 0])
            wup = wup.at[GATE_RANK:2 * GATE_RANK, 512:1024].set(w_gate_up[i, 1]).astype(BF16)
            bg = b_gate[i].reshape(1, D_MODEL)
            w_out = w_out_odd[i].astype(BF16)
            gn = gla_norm_g[i].reshape(1, D_MODEL)
            qk, v, r, gate = _odd_in(xp, mod, l, ctx_group, g1, w_main, wgd, wup, bg)
            o, s_ctx = _gla(qk, v, gate, None, tp, True)
            ss_new.append(s_ctx)
            pre_p = (o, r, gn, w_out)
            qk, v, r, gate = _odd_in(xs, mod, l, lat_group(1024), g1, w_main, wgd, wup, bg)
            (o,) = _gla(qk, v, gate, (state_gla, i), ts, False)
            pre_s = (o, r, gn, w_out)
        wi = w_ffn_in[l].astype(BF16)
        wo = w_ffn_out[l].astype(BF16)
        xp = _ffn(xp, mod, l, ctx_group, g2, wi, wo, pre_p, last)
        xs = _ffn(xs, mod, l, lat_group(512), g2, wi, wo, pre_s, last)

    y_prompt = xp.reshape(bp, tp, D_MODEL)
    y_sample = xs.reshape(bs, ts, D_MODEL)
    new_attn_k = jnp.stack(ks_new, axis=1).reshape(bp, n_even, tp, ATT_HEADS, 2, ATT_DH)
    new_attn_v = jnp.stack(vs_new, axis=1).reshape(bp, n_even, tp, ATT_HEADS, ATT_DV)
    new_gla_state = jnp.stack(ss_new, axis=1)
    return (y_prompt, y_sample, new_attn_k, new_attn_v, new_gla_state)
```

```python
import functools
import math

import jax
import jax.numpy as jnp
from jax import lax
from jax.experimental import pallas as pl
from jax.experimental.pallas import tpu as pltpu

F32 = jnp.float32
BF16 = jnp.bfloat16

D_MODEL = 1024
DEPTH = 4
GRID_W = 64
ATT_WIDTH = 512
POOL_WIDTH = 512
ATT_HEADS = 4
ATT_DH = 64
ATT_DV = 128
ROPE_BASE = 10000.0
POOL_WINDOWS = (2, 4, 8, 16)
POOL_GROUP = 128
POOL_PAD = 8
GLA_HEADS = 4
GLA_DK = 128
GLA_DV = 256
GATE_RANK = 16
GATE_TEMP = 16.0
CHUNK = 64
GLA_STEP = 2 * CHUNK
FFN_HIDDEN = 2816
FFN_CHUNK = 256
FFN_TM = 512
EPS = 1e-6
Q_SCALE = ATT_DH ** -0.5 * math.log2(math.e)
MOD_ROWS = 16

V7X_VMEM_LIMIT = 56 * 1024 * 1024


def _params(n_axes, vmem_bytes=V7X_VMEM_LIMIT):
    return pltpu.CompilerParams(dimension_semantics=("arbitrary",) * n_axes,
                                vmem_limit_bytes=vmem_bytes)


def _resident(shape):
    nd = len(shape)
    return pl.BlockSpec(shape, lambda *_: (0,) * nd, pipeline_mode=pl.Buffered(1))


def _layer_of(stack, idx):
    nd = stack.ndim - 1
    return pl.BlockSpec((None,) + stack.shape[1:], lambda *_: (idx,) + (0,) * nd,
                        pipeline_mode=pl.Buffered(1))


def _norm_mod(x, g, shift, scale):
    y = x * lax.rsqrt(jnp.mean(x * x, axis=-1, keepdims=True) + EPS)
    return (y * g) * (1.0 + scale) + shift


def _silu(x):
    return x * jax.nn.sigmoid(x)


def _adaln_kernel(cv_ref, w_ref, b_ref, o_ref):
    s = _silu(cv_ref[...]).astype(BF16)
    o_ref[...] = jnp.dot(s, w_ref[...].astype(BF16), preferred_element_type=F32) + b_ref[...]


def _adaln(cvec, w_mod, b_mod):
    tn = 1536
    out = pl.pallas_call(
        _adaln_kernel,
        grid=(DEPTH, 6 * D_MODEL // tn),
        in_specs=[pl.BlockSpec((MOD_ROWS, D_MODEL), lambda l, j: (0, 0)),
                  pl.BlockSpec((None, D_MODEL, tn), lambda l, j: (l, 0, j)),
                  pl.BlockSpec((None, 1, tn), lambda l, j: (l, 0, j))],
        out_specs=pl.BlockSpec((None, MOD_ROWS, tn), lambda l, j: (l, 0, j)),
        out_shape=jax.ShapeDtypeStruct((DEPTH, MOD_ROWS, 6 * D_MODEL), F32),
        compiler_params=_params(2),
        name="adaln",
    )(cvec, w_mod, b_mod.reshape(DEPTH, 1, 6 * D_MODEL))
    return out.reshape(DEPTH, MOD_ROWS, 6, D_MODEL)


def _mod_spec(layer, group_of_step):
    return pl.BlockSpec((None, None, 6, D_MODEL),
                        lambda *ids: (layer, group_of_step(*ids), 0, 0))


def _even_in_kernel(*refs, tm, rc, rope):
    if rope:
        x_ref, mod_ref, g_ref, w_ref, cos_ref, sin_ref, q_ref, k_ref, v_ref, u_ref = refs
    else:
        x_ref, mod_ref, g_ref, w_ref, q_ref, k_ref, v_ref, u_ref = refs
    shift, scale, g = mod_ref[0:1, :], mod_ref[1:2, :], g_ref[...]
    lo = (lax.broadcasted_iota(jnp.int32, (rc, 128), 1) % 32) < 16

    def rotary(y, cos, sin):
        parts = []
        for h in range(ATT_HEADS):
            yh = y[:, h * 128:(h + 1) * 128]
            fwd = pltpu.roll(yh, 128 - 16, axis=1)
            bwd = pltpu.roll(yh, 16, axis=1)
            parts.append(yh * cos + jnp.where(lo, fwd, bwd) * sin)
        return jnp.concatenate(parts, axis=1)

    for r in range(tm // rc):
        rows = slice(r * rc, (r + 1) * rc)
        h = _norm_mod(x_ref[rows, :], g, shift, scale).astype(BF16)
        q = jnp.dot(h, w_ref[:, 0:512], preferred_element_type=F32)
        k = jnp.dot(h, w_ref[:, 512:1024], preferred_element_type=F32)
        if rope:
            cos, sin = cos_ref[rows, :], sin_ref[rows, :]
            q, k = rotary(q, cos, sin), rotary(k, cos, sin)
        q_ref[rows, :] = (q * Q_SCALE).astype(q_ref.dtype)
        k_ref[rows, :] = k.astype(k_ref.dtype)
        v_ref[rows, :] = jnp.dot(h, w_ref[:, 1024:1536], preferred_element_type=F32).astype(v_ref.dtype)
        u_ref[rows, :] = jnp.dot(h, w_ref[:, 1536:2048], preferred_element_type=F32)


def _even_in(x, mod, layer, group_of_step, g1, w_bf, rope_tabs, seq_len, kv_dtype):
    n = x.shape[0]
    tm, rc = 1024, 512
    row = lambda i: (i, 0)
    in_specs = [pl.BlockSpec((tm, D_MODEL), row), _mod_spec(layer, group_of_step),
                _layer_of(*g1), _layer_of(*w_bf)]
    args = [x, mod, g1[0], w_bf[0]]
    if rope_tabs is not None:
        per_seq = seq_len // tm
        pos = lambda i: (i % per_seq, 0)
        in_specs += [pl.BlockSpec((tm, 128), pos), pl.BlockSpec((tm, 128), pos)]
        args += list(rope_tabs)
    out_dt = (BF16, kv_dtype, kv_dtype, F32)
    return pl.pallas_call(
        functools.partial(_even_in_kernel, tm=tm, rc=rc, rope=rope_tabs is not None),
        grid=(n // tm,),
        in_specs=in_specs,
        out_specs=[pl.BlockSpec((tm, 512), row)] * 4,
        out_shape=[jax.ShapeDtypeStruct((n, 512), dt) for dt in out_dt],
        compiler_params=_params(1),
        name="even_in",
    )(*args)


def _even_mix_kernel(*refs, seq, tq, n_cache, lam_init):
    if n_cache:
        (x_ref, q_ref, k_ref, v_ref, u_ref, ck_ref, cv_ref, mod_ref, lamp_ref, subg_ref,
         wpool_ref, pscale_ref, wout_ref, o_ref, upad, ypool) = refs
    else:
        (x_ref, q_ref, k_ref, v_ref, u_ref, mod_ref, lamp_ref, subg_ref,
         wpool_ref, pscale_ref, wout_ref, o_ref, upad, ypool) = refs
    qi = pl.program_id(1)

    @pl.when(qi == 0)
    def _pool():
        zeros = jnp.zeros((POOL_PAD, POOL_WIDTH), F32)
        upad[0:POOL_PAD, :] = zeros
        upad[POOL_PAD + seq:POOL_PAD + seq + POOL_PAD, :] = zeros
        upad[POOL_PAD:POOL_PAD + seq, :] = u_ref[...]
        n_pad = seq + 2 * POOL_PAD
        t = lax.broadcasted_iota(jnp.int32, (seq, POOL_GROUP), 0)
        for g, w in enumerate(POOL_WINDOWS):
            cols = slice(g * POOL_GROUP, (g + 1) * POOL_GROUP)
            xg = upad[:, cols]
            p = xg + pltpu.roll(xg, 1, axis=0)
            half = 1
            while 2 * half < w:
                p = pltpu.roll(p, half, axis=0) + pltpu.roll(p, n_pad - half, axis=0)
                half *= 2
            cnt = (jnp.minimum(t - w // 2 + w, seq) - jnp.maximum(t - w // 2, 0)).astype(F32)
            y = p[POOL_PAD:POOL_PAD + seq] / cnt - xg[POOL_PAD:POOL_PAD + seq]
            ypool[:, cols] = y.astype(BF16)

    lp = lamp_ref[...]
    lam = (jnp.exp(jnp.sum(lp[0:1, :] * lp[1:2, :], axis=-1, keepdims=True))
           - jnp.exp(jnp.sum(lp[2:3, :] * lp[3:4, :], axis=-1, keepdims=True)) + lam_init)

    lane = lax.broadcasted_iota(jnp.int32, (1, 128), 1)
    m1 = jnp.where(lane < ATT_DH, 1.0, 0.0).astype(BF16)
    m2 = jnp.where(lane >= ATT_DH, 1.0, 0.0).astype(BF16)
    nt = (((1,), (1,)), ((), ()))

    def scores(h):
        cols = slice(h * 128, (h + 1) * 128)
        qh = q_ref[:, cols]
        q2 = jnp.concatenate([qh * m1, qh * m2], axis=0)
        s = lax.dot_general(q2, k_ref[:, cols].astype(BF16), nt, preferred_element_type=F32)
        sc = None
        if n_cache:
            sc = lax.dot_general(q2, ck_ref[:, cols].astype(BF16), nt, preferred_element_type=F32)
        return s, sc

    heads = []
    pending = scores(0)
    for h in range(ATT_HEADS):
        cols = slice(h * 128, (h + 1) * 128)
        s, sc = pending
        if h + 1 < ATT_HEADS:
            pending = scores(h + 1)
        vh = v_ref[:, cols].astype(BF16)
        mx = jnp.max(s, axis=-1, keepdims=True)
        if n_cache:
            cvh = cv_ref[:, cols].astype(BF16)
            mx = jnp.maximum(mx, jnp.max(sc, axis=-1, keepdims=True))
        e = jnp.exp2(s - mx)
        den = jnp.sum(e, axis=-1, keepdims=True)
        if n_cache:
            ec = jnp.exp2(sc - mx)
            den = den + jnp.sum(ec, axis=-1, keepdims=True)
        inv1 = 1.0 / den[:tq]
        ratio = lam * den[:tq] / den[tq:]
        a = (e[:tq] - ratio * e[tq:]).astype(BF16)
        o = jnp.dot(a, vh, preferred_element_type=F32)
        if n_cache:
            ac = (ec[:tq] - ratio * ec[tq:]).astype(BF16)
            o = o + jnp.dot(ac, cvh, preferred_element_type=F32)
        o = o * inv1
        o = o * lax.rsqrt(jnp.mean(o * o, axis=-1, keepdims=True) + EPS)
        heads.append((o * subg_ref[:, cols]) * (1.0 - lam_init))
    o_att = jnp.concatenate(heads, axis=1).astype(BF16)

    yq = ypool[pl.ds(pl.multiple_of(qi * tq, tq), tq), :]
    pooled = [jnp.dot(yq[:, g * POOL_GROUP:(g + 1) * POOL_GROUP], wpool_ref[g],
                      preferred_element_type=F32) for g in range(len(POOL_WINDOWS))]
    p = (jnp.concatenate(pooled, axis=1) * pscale_ref[...]).astype(BF16)

    out = (jnp.dot(o_att, wout_ref[0:ATT_WIDTH, :], preferred_element_type=F32)
           + jnp.dot(p, wout_ref[ATT_WIDTH:, :], preferred_element_type=F32))
    o_ref[...] = x_ref[...] + mod_ref[2:3, :] * out


def _even_mix(x, q, k, v, u, cache, mod, layer, group_of_seq, lamp, subg, wpool_bf, pscale,
              wout_bf, seq, lam_init):
    n = x.shape[0]
    nseq = n // seq
    tq = 256
    nq = seq // tq
    blk = lambda b, qi: (b * nq + qi, 0)
    whole = lambda b, qi: (b, 0)
    in_specs = [pl.BlockSpec((tq, D_MODEL), blk), pl.BlockSpec((tq, 512), blk),
                pl.BlockSpec((seq, 512), whole), pl.BlockSpec((seq, 512), whole),
                pl.BlockSpec((seq, 512), whole)]
    args = [x, q, k, v, u]
    n_cache = 0
    if cache is not None:
        ck, cv, il = cache
        n_cache = ck.shape[2]
        cspec = pl.BlockSpec((None, None, n_cache, 512), lambda b, qi: (b, il, 0, 0))
        in_specs += [cspec, cspec]
        args += [ck, cv]
    weights = [lamp, subg, wpool_bf, pscale, wout_bf]
    in_specs += [_mod_spec(layer, lambda b, qi: group_of_seq(b))] + [_layer_of(*w) for w in weights]
    args += [mod] + [w[0] for w in weights]
    return pl.pallas_call(
        functools.partial(_even_mix_kernel, seq=seq, tq=tq, n_cache=n_cache, lam_init=lam_init),
        grid=(nseq, nq),
        in_specs=in_specs,
        out_specs=pl.BlockSpec((tq, D_MODEL), blk),
        out_shape=jax.ShapeDtypeStruct((n, D_MODEL), F32),
        scratch_shapes=[pltpu.VMEM((seq + 2 * POOL_PAD, POOL_WIDTH), F32),
                        pltpu.VMEM((seq, POOL_WIDTH), BF16)],
        compiler_params=_params(2),
        name="even_mix",
    )(*args)


def _ffn_kernel(*refs, gla_pre, final):
    refs = list(refs)
    x_ref, mod_ref = refs.pop(0), refs.pop(0)
    if gla_pre:
        og_ref, r_ref, gn_ref, wmix_ref = (refs.pop(0) for _ in range(4))
    g_ref, wi_ref, wo_ref = (refs.pop(0) for _ in range(3))
    fg_ref = refs.pop(0) if final else None
    o_ref, acc_ref = refs
    x = x_ref[...]
    if gla_pre:
        parts = []
        for h in range(GLA_HEADS):
            cols = slice(h * GLA_DV, (h + 1) * GLA_DV)
            oh = og_ref[:, cols]
            oh = oh * lax.rsqrt(jnp.mean(oh * oh, axis=-1, keepdims=True) + EPS)
            parts.append(oh * gn_ref[:, cols])
        z = (jnp.concatenate(parts, axis=1) * _silu(r_ref[...])).astype(BF16)
        x = x + mod_ref[2:3, :] * jnp.dot(z, wmix_ref[...], preferred_element_type=F32)
    h = _norm_mod(x, g_ref[...], mod_ref[3:4, :], mod_ref[4:5, :]).astype(BF16)
    for c in range(FFN_HIDDEN // FFN_CHUNK):
        lo, hi = c * FFN_CHUNK, (c + 1) * FFN_CHUNK
        a = jnp.dot(h, wi_ref[:, lo:hi], preferred_element_type=F32)
        b = jnp.dot(h, wi_ref[:, FFN_HIDDEN + lo:FFN_HIDDEN + hi], preferred_element_type=F32)
        z = (_silu(a) * b).astype(BF16)
        part = jnp.dot(z, wo_ref[lo:hi, :], preferred_element_type=F32)
        if c == 0:
            acc_ref[...] = part
        else:
            acc_ref[...] += part
    y = x + mod_ref[5:6, :] * acc_ref[...]
    if final:
        y = (y * lax.rsqrt(jnp.mean(y * y, axis=-1, keepdims=True) + EPS)) * fg_ref[...]
    o_ref[...] = y


def _ffn(x, mod, layer, group_of_step, g2, wi_bf, wo_bf, gla_pre=None, final_g=None):
    n = x.shape[0]
    tm = FFN_TM
    row = lambda i: (i, 0)
    tile = pl.BlockSpec((tm, D_MODEL), row)
    in_specs = [tile, _mod_spec(layer, group_of_step)]
    args = [x, mod]
    if gla_pre is not None:
        o, r, gn, wmix_bf = gla_pre
        in_specs += [tile, tile, _layer_of(*gn), _layer_of(*wmix_bf)]
        args += [o, r, gn[0], wmix_bf[0]]
    in_specs += [_layer_of(*g2), _layer_of(*wi_bf), _layer_of(*wo_bf)]
    args += [g2[0], wi_bf[0], wo_bf[0]]
    if final_g is not None:
        in_specs.append(_resident((1, D_MODEL)))
        args.append(final_g)
    return pl.pallas_call(
        functools.partial(_ffn_kernel, gla_pre=gla_pre is not None, final=final_g is not None),
        grid=(n // tm,),
        in_specs=in_specs,
        out_specs=tile,
        out_shape=jax.ShapeDtypeStruct((n, D_MODEL), F32),
        scratch_shapes=[pltpu.VMEM((tm, D_MODEL), F32)],
        compiler_params=_params(1),
        name="ffn",
    )(*args)


def _chunk_scan(x, reverse):
    n = x.shape[0]
    pos = lax.broadcasted_iota(jnp.int32, x.shape, 0) % CHUNK
    step = 1
    while step < CHUNK:
        if reverse:
            x = x + jnp.where(pos < CHUNK - step, pltpu.roll(x, n - step, axis=0), 0.0)
        else:
            x = x + jnp.where(pos >= step, pltpu.roll(x, step, axis=0), 0.0)
        step *= 2
    return x


def _odd_in_kernel(x_ref, mod_ref, g_ref, w_ref, wgd_ref, wup_ref, bg_ref,
                   qk_ref, v_ref, r_ref, gate_ref, *, tm, rc):
    shift, scale, g = mod_ref[0:1, :], mod_ref[1:2, :], g_ref[...]
    for c in range(tm // rc):
        rows = slice(c * rc, (c + 1) * rc)
        h = _norm_mod(x_ref[rows, :], g, shift, scale).astype(BF16)
        gd = jnp.dot(h, wgd_ref[...], preferred_element_type=F32).astype(BF16)
        logit = jnp.dot(gd, wup_ref[...], preferred_element_type=F32) + bg_ref[...]
        qk_ref[rows, :] = jnp.dot(h, w_ref[:, 0:1024], preferred_element_type=F32)
        v_ref[rows, :] = jnp.dot(h, w_ref[:, 1024:2048], preferred_element_type=F32).astype(BF16)
        r_ref[rows, :] = jnp.dot(h, w_ref[:, 2048:3072], preferred_element_type=F32)
        log_sig = jnp.minimum(logit, 0.0) - jnp.log(1.0 + jnp.exp(-jnp.abs(logit)))
        gate_ref[rows, :] = log_sig / GATE_TEMP


def _odd_in(x, mod, layer, group_of_step, g1, w_bf, wgd_bf, wup_bf, bg):
    n = x.shape[0]
    tm, rc = 1024, 512
    row = lambda i: (i, 0)
    return pl.pallas_call(
        functools.partial(_odd_in_kernel, tm=tm, rc=rc),
        grid=(n // tm,),
        in_specs=[pl.BlockSpec((tm, D_MODEL), row), _mod_spec(layer, group_of_step),
                  _layer_of(*g1), _layer_of(*w_bf),
                  _resident((D_MODEL, 128)), _resident((128, D_MODEL)), _layer_of(*bg)],
        out_specs=[pl.BlockSpec((tm, D_MODEL), row)] * 4,
        out_shape=[jax.ShapeDtypeStruct((n, D_MODEL), dt) for dt in (F32, BF16, F32, F32)],
        compiler_params=_params(1),
        name="odd_in",
    )(x, mod, g1[0], w_bf[0], wgd_bf, wup_bf, bg[0])


def _gla_kernel(*refs, tb, nsb, has_init, emit_state):
    refs = list(refs)
    ins = [refs.pop(0) for _ in range(6)]
    init_ref = refs.pop(0) if has_init else None
    o_ref = refs.pop(0)
    state_ref = refs.pop(0) if emit_state else None
    st = refs.pop(0)
    n = pl.program_id(1)

    @pl.when(n == 0)
    def _init():
        o_ref[...] = jnp.zeros(o_ref.shape, F32)
        if has_init:
            st[...] = init_ref[...]
        else:
            st[...] = jnp.zeros(st.shape, F32)

    ri = lax.broadcasted_iota(jnp.int32, (GLA_STEP, GLA_STEP), 0)
    ci = lax.broadcasted_iota(jnp.int32, (GLA_STEP, GLA_STEP), 1)
    first = lax.broadcasted_iota(jnp.int32, (GLA_STEP, 512), 0) < CHUNK
    nt = (((1,), (1,)), ((), ()))
    nstep = tb // GLA_STEP

    def body(i, carry):
        for d in range(2):
            qk_ref, v_ref, g_ref = ins[3 * d:3 * d + 3]
            c = i if d == 0 else nstep - 1 - i
            block = n if d == 0 else nsb - 1 - n
            rows = pl.ds(pl.multiple_of(c * GLA_STEP, GLA_STEP), GLA_STEP)
            orows = pl.ds(pl.multiple_of(block * tb + c * GLA_STEP, GLA_STEP), GLA_STEP)
            p = _chunk_scan(g_ref[rows, :], reverse=d == 1)
            if d == 0:
                mask = ci <= ri
                tot_a, tot_b = p[CHUNK - 1:CHUNK, :], p[GLA_STEP - 1:GLA_STEP, :]
                b_mid = jnp.where(first, p - tot_a, p)
                b_abs = jnp.where(first, p, p + tot_a)
            else:
                mask = ci >= ri
                tot_a, tot_b = p[0:1, :], p[CHUNK:CHUNK + 1, :]
                b_mid = jnp.where(first, p, p - tot_b)
                b_abs = jnp.where(first, p + tot_b, p)
            tot = tot_a + tot_b
            q = qk_ref[rows, 0:512] * (GLA_DK ** -0.5)
            k = qk_ref[rows, 512:1024]
            q_t = (q * jnp.exp(b_mid)).astype(BF16)
            k_t = (k * jnp.exp(-b_mid)).astype(BF16)
            q_in = (q * jnp.exp(b_abs)).astype(BF16)
            k_end = k * jnp.exp(tot - b_abs)
            decay = jnp.exp(jnp.broadcast_to(tot, (8, 512)).T[:, 0:1])
            for h in range(GLA_HEADS):
                kc = slice(h * GLA_DK, (h + 1) * GLA_DK)
                vc = slice(h * GLA_DV, (h + 1) * GLA_DV)
                vh = v_ref[rows, vc]
                a = lax.dot_general(q_t[:, kc], k_t[:, kc], nt, preferred_element_type=F32)
                a = jnp.where(mask, a, 0.0).astype(BF16)
                lhs = jnp.concatenate([a, k_end[:, kc].T.astype(BF16)], axis=0)
                both = jnp.dot(lhs, vh, preferred_element_type=F32)
                s = st[d, h]
                o = both[:GLA_STEP] + jnp.dot(q_in[:, kc], s.astype(BF16), preferred_element_type=F32)
                o_ref[orows, vc] += o
                st[d, h] = s * decay[kc, :] + both[GLA_STEP:]
        return carry

    lax.fori_loop(0, nstep, body, 0)

    if emit_state:
        @pl.when(n == pl.num_programs(1) - 1)
        def _emit():
            state_ref[...] = st[...]


def _gla(qk, v, gate, init, seq, emit_state):
    n = qk.shape[0]
    nseq = n // seq
    tb = 256
    nsb = seq // tb
    fwd = lambda b, s: (b * nsb + s, 0)
    bwd = lambda b, s: (b * nsb + nsb - 1 - s, 0)
    bwd_gate = lambda b, s: (b * nsb + nsb - 1 - s, 1)
    in_specs = [pl.BlockSpec((tb, 1024), fwd), pl.BlockSpec((tb, 1024), fwd), pl.BlockSpec((tb, 512), fwd),
                pl.BlockSpec((tb, 1024), bwd), pl.BlockSpec((tb, 1024), bwd), pl.BlockSpec((tb, 512), bwd_gate)]
    args = [qk, v, gate, qk, v, gate]
    if init is not None:
        state, il = init
        in_specs.append(pl.BlockSpec((None, None, 2, GLA_HEADS, GLA_DK, GLA_DV),
                                     lambda b, s: (b, il, 0, 0, 0, 0)))
        args.append(state)
    out_specs = [pl.BlockSpec((seq, 1024), lambda b, s: (b, 0))]
    out_shape = [jax.ShapeDtypeStruct((n, 1024), F32)]
    if emit_state:
        out_specs.append(pl.BlockSpec((None, 2, GLA_HEADS, GLA_DK, GLA_DV), lambda b, s: (b, 0, 0, 0, 0)))
        out_shape.append(jax.ShapeDtypeStruct((nseq, 2, GLA_HEADS, GLA_DK, GLA_DV), F32))
    return pl.pallas_call(
        functools.partial(_gla_kernel, tb=tb, nsb=nsb, has_init=init is not None, emit_state=emit_state),
        grid=(nseq, nsb),
        in_specs=in_specs,
        out_specs=out_specs,
        out_shape=out_shape,
        scratch_shapes=[pltpu.VMEM((2, GLA_HEADS, GLA_DK, GLA_DV), F32)],
        compiler_params=_params(2),
        name="gla",
    )(*args)


def _rope_tables(n_lat):
    half = ATT_DH // 2
    inv = 1.0 / (ROPE_BASE ** (jnp.arange(0, half, 2, dtype=F32) / half))
    t = jnp.arange(n_lat)
    rows = (t // GRID_W).astype(F32)
    cols = (t % GRID_W).astype(F32)

    def tab(pos):
        ang = pos[:, None] * inv[None, :]
        ang = jnp.concatenate([ang, ang], axis=-1)
        return jnp.cos(ang), jnp.sin(ang)

    (cr, sr), (cc, sc) = tab(rows), tab(cols)
    sign = jnp.concatenate([-jnp.ones((half // 2,), F32), jnp.ones((half // 2,), F32)])
    cos = jnp.concatenate([cr, cc], axis=-1)
    sin = jnp.concatenate([sr * sign, sc * sign], axis=-1)
    return jnp.tile(cos, (1, 2)), jnp.tile(sin, (1, 2))


def kernel(x_prompt, x_sample, c, cache_attn_k, cache_attn_v, state_gla, c_ctx, norm1_g, norm2_g, w_mod, b_mod, w_in_even, lam_params, subln_g, w_pool, pool_scale, w_out_even, w_in_odd, w_gate_up, b_gate, gla_norm_g, w_out_odd, w_ffn_in, w_ffn_out, final_g):
    bp, tp, _ = x_prompt.shape
    bs, ts, _ = x_sample.shape
    past = cache_attn_k.shape[2]
    n_even = cache_attn_k.shape[1]
    assert bs + 1 <= MOD_ROWS

    xp = x_prompt.reshape(bp * tp, D_MODEL)
    xs = x_sample.reshape(bs * ts, D_MODEL)
    ck = cache_attn_k.reshape(bs, n_even, past, ATT_WIDTH)
    cv = cache_attn_v.reshape(bs, n_even, past, ATT_WIDTH)

    cvec = jnp.concatenate([c_ctx[None, :], c, jnp.zeros((MOD_ROWS - 1 - bs, D_MODEL), F32)], axis=0)
    mod = _adaln(cvec, w_mod, b_mod)
    ctx_group = lambda i: 0

    def lat_group(tm):
        return lambda i: 1 + (i * tm) // ts

    rope = _rope_tables(ts)

    row3 = lambda a: a.reshape(a.shape[0], 1, a.shape[1])
    n1, n2 = row3(norm1_g), row3(norm2_g)
    w_in_even_bf, w_out_even_bf, w_pool_bf = (w.astype(BF16) for w in (w_in_even, w_out_even, w_pool))
    w_in_odd_bf, w_out_odd_bf = w_in_odd.astype(BF16), w_out_odd.astype(BF16)
    w_ffn_in_bf, w_ffn_out_bf = w_ffn_in.astype(BF16), w_ffn_out.astype(BF16)
    subg3, pscale3, gn3 = row3(subln_g), row3(pool_scale), row3(gla_norm_g)
    bg3 = b_gate.reshape(b_gate.shape[0], 1, D_MODEL)

    fg = final_g.reshape(1, D_MODEL)
    ks_new, vs_new, ss_new = [], [], []
    for l in range(DEPTH):
        i = l // 2
        g1, g2 = (n1, l), (n2, l)
        last = fg if l == DEPTH - 1 else None
        pre_p = pre_s = None
        if l % 2 == 0:
            lam_init = 0.8 - 0.6 * math.exp(-0.3 * l)
            w_in, w_out, wpool = (w_in_even_bf, i), (w_out_even_bf, i), (w_pool_bf, i)
            subg, pscale, lamp = (subg3, i), (pscale3, i), (lam_params, i)
            q, k, v, u = _even_in(xp, mod, l, ctx_group, g1, w_in, None, tp, F32)
            ks_new.append(k.reshape(bp, tp, ATT_WIDTH))
            vs_new.append(v.reshape(bp, tp, ATT_WIDTH))
            xp = _even_mix(xp, q, k, v, u, None, mod, l, ctx_group, lamp, subg, wpool, pscale,
                           w_out, tp, lam_init)
            q, k, v, u = _even_in(xs, mod, l, lat_group(1024), g1, w_in, rope, ts, BF16)
            xs = _even_mix(xs, q, k, v, u, (ck, cv, i), mod, l, lambda b: 1 + b, lamp, subg, wpool,
                           pscale, w_out, ts, lam_init)
        else:
            w_main, w_out, gn, bg = (w_in_odd_bf, i), (w_out_odd_bf, i), (gn3, i), (bg3, i)
            wgd = jnp.pad(w_in_odd[i][:, 3072:], ((0, 0), (0, 128 - 2 * GATE_RANK))).astype(BF16)
            wup = jnp.zeros((128, D_MODEL), F32)
            wup = wup.at[0:GATE_RANK, 0:512].set(w_gate_up[i, 0])
            wup = wup.at[GATE_RANK:2 * GATE_RANK, 512:1024].set(w_gate_up[i, 1]).astype(BF16)
            qk, v, r, gate = _odd_in(xp, mod, l, ctx_group, g1, w_main, wgd, wup, bg)
            o, s_ctx = _gla(qk, v, gate, None, tp, True)
            ss_new.append(s_ctx)
            pre_p = (o, r, gn, w_out)
            qk, v, r, gate = _odd_in(xs, mod, l, lat_group(1024), g1, w_main, wgd, wup, bg)
            (o,) = _gla(qk, v, gate, (state_gla, i), ts, False)
            pre_s = (o, r, gn, w_out)
        wi, wo = (w_ffn_in_bf, l), (w_ffn_out_bf, l)
        xp = _ffn(xp, mod, l, ctx_group, g2, wi, wo, pre_p, last)
        xs = _ffn(xs, mod, l, lat_group(FFN_TM), g2, wi, wo, pre_s, last)

    y_prompt = xp.reshape(bp, tp, D_MODEL)
    y_sample = xs.reshape(bs, ts, D_MODEL)
    new_attn_k = jnp.stack(ks_new, axis=1).reshape(bp, n_even, tp, ATT_HEADS, 2, ATT_DH)
    new_attn_v = jnp.stack(vs_new, axis=1).reshape(bp, n_even, tp, ATT_HEADS, ATT_DV)
    new_gla_state = jnp.stack(ss_new, axis=1)
    return (y_prompt, y_sample, new_attn_k, new_attn_v, new_gla_state)
```

```python
import functools
import math

import jax
import jax.numpy as jnp
from jax import lax
from jax.experimental import pallas as pl
from jax.experimental.pallas import tpu as pltpu

F32 = jnp.float32
BF16 = jnp.bfloat16

D_MODEL = 1024
DEPTH = 4
GRID_W = 64
ATT_WIDTH = 512
POOL_WIDTH = 512
ATT_HEADS = 4
ATT_DH = 64
ATT_DV = 128
ROPE_BASE = 10000.0
POOL_WINDOWS = (2, 4, 8, 16)
POOL_GROUP = 128
POOL_PAD = 8
GLA_HEADS = 4
GLA_DK = 128
GLA_DV = 256
GATE_RANK = 16
GATE_TEMP = 16.0
CHUNK = 64
GLA_STEP = 2 * CHUNK
FFN_HIDDEN = 2816
FFN_CHUNK = 256
FFN_TM = 512
EPS = 1e-6
Q_SCALE = ATT_DH ** -0.5 * math.log2(math.e)
SAFE_SHIFT = 50.0
NORM_SLACK = 1.05
NORM_LANES = 16
MOD_ROWS = 16

V7X_VMEM_LIMIT = 56 * 1024 * 1024


def _params(n_axes, vmem_bytes=V7X_VMEM_LIMIT):
    return pltpu.CompilerParams(dimension_semantics=("arbitrary",) * n_axes,
                                vmem_limit_bytes=vmem_bytes)


def _resident(shape):
    nd = len(shape)
    return pl.BlockSpec(shape, lambda *_: (0,) * nd, pipeline_mode=pl.Buffered(1))


def _layer_of(stack, idx):
    nd = stack.ndim - 1
    return pl.BlockSpec((None,) + stack.shape[1:], lambda *_: (idx,) + (0,) * nd,
                        pipeline_mode=pl.Buffered(1))


def _norm_mod(x, g, shift, scale):
    y = x * lax.rsqrt(jnp.mean(x * x, axis=-1, keepdims=True) + EPS)
    return (y * g) * (1.0 + scale) + shift


def _silu(x):
    return x * jax.nn.sigmoid(x)


def _adaln_kernel(cv_ref, w_ref, b_ref, o_ref):
    s = _silu(cv_ref[...]).astype(BF16)
    o_ref[...] = jnp.dot(s, w_ref[...].astype(BF16), preferred_element_type=F32) + b_ref[...]


def _adaln(cvec, w_mod, b_mod):
    tn = 1536
    out = pl.pallas_call(
        _adaln_kernel,
        grid=(DEPTH, 6 * D_MODEL // tn),
        in_specs=[pl.BlockSpec((MOD_ROWS, D_MODEL), lambda l, j: (0, 0)),
                  pl.BlockSpec((None, D_MODEL, tn), lambda l, j: (l, 0, j)),
                  pl.BlockSpec((None, 1, tn), lambda l, j: (l, 0, j))],
        out_specs=pl.BlockSpec((None, MOD_ROWS, tn), lambda l, j: (l, 0, j)),
        out_shape=jax.ShapeDtypeStruct((DEPTH, MOD_ROWS, 6 * D_MODEL), F32),
        compiler_params=_params(2),
        name="adaln",
    )(cvec, w_mod, b_mod.reshape(DEPTH, 1, 6 * D_MODEL))
    return out.reshape(DEPTH, MOD_ROWS, 6, D_MODEL)


def _mod_spec(layer, group_of_step):
    return pl.BlockSpec((None, None, 6, D_MODEL),
                        lambda *ids: (layer, group_of_step(*ids), 0, 0))


def _even_in_kernel(*refs, tm, rc, rope):
    if rope:
        x_ref, mod_ref, g_ref, w_ref, cos_ref, sin_ref, q_ref, k_ref, v_ref, u_ref = refs
    else:
        x_ref, mod_ref, g_ref, w_ref, q_ref, k_ref, v_ref, u_ref = refs
    shift, scale, g = mod_ref[0:1, :], mod_ref[1:2, :], g_ref[...]
    lo = (lax.broadcasted_iota(jnp.int32, (rc, 128), 1) % 32) < 16

    def rotary(y, cos, sin):
        parts = []
        for h in range(ATT_HEADS):
            yh = y[:, h * 128:(h + 1) * 128]
            fwd = pltpu.roll(yh, 128 - 16, axis=1)
            bwd = pltpu.roll(yh, 16, axis=1)
            parts.append(yh * cos + jnp.where(lo, fwd, bwd) * sin)
        return jnp.concatenate(parts, axis=1)

    for r in range(tm // rc):
        rows = slice(r * rc, (r + 1) * rc)
        h = _norm_mod(x_ref[rows, :], g, shift, scale).astype(BF16)
        q = jnp.dot(h, w_ref[:, 0:512], preferred_element_type=F32)
        k = jnp.dot(h, w_ref[:, 512:1024], preferred_element_type=F32)
        if rope:
            cos, sin = cos_ref[rows, :], sin_ref[rows, :]
            q, k = rotary(q, cos, sin), rotary(k, cos, sin)
        q_ref[rows, :] = (q * Q_SCALE).astype(q_ref.dtype)
        k_ref[rows, :] = k.astype(k_ref.dtype)
        v_ref[rows, :] = jnp.dot(h, w_ref[:, 1024:1536], preferred_element_type=F32).astype(v_ref.dtype)
        u_ref[rows, :] = jnp.dot(h, w_ref[:, 1536:2048], preferred_element_type=F32)


def _even_in(x, mod, layer, group_of_step, g1, w_bf, rope_tabs, seq_len, kv_dtype):
    n = x.shape[0]
    tm, rc = 1024, 512
    row = lambda i: (i, 0)
    in_specs = [pl.BlockSpec((tm, D_MODEL), row), _mod_spec(layer, group_of_step),
                _layer_of(*g1), _layer_of(*w_bf)]
    args = [x, mod, g1[0], w_bf[0]]
    if rope_tabs is not None:
        per_seq = seq_len // tm
        pos = lambda i: (i % per_seq, 0)
        in_specs += [pl.BlockSpec((tm, 128), pos), pl.BlockSpec((tm, 128), pos)]
        args += list(rope_tabs)
    out_dt = (BF16, kv_dtype, kv_dtype, F32)
    return pl.pallas_call(
        functools.partial(_even_in_kernel, tm=tm, rc=rc, rope=rope_tabs is not None),
        grid=(n // tm,),
        in_specs=in_specs,
        out_specs=[pl.BlockSpec((tm, 512), row)] * 4,
        out_shape=[jax.ShapeDtypeStruct((n, 512), dt) for dt in out_dt],
        compiler_params=_params(1),
        name="even_in",
    )(*args)


def _even_mix_kernel(*refs, seq, tq, n_cache, lam_init):
    if n_cache:
        (x_ref, q_ref, k_ref, v_ref, u_ref, ck_ref, cv_ref, mod_ref, pairsum_ref, lamp_ref, subg_ref,
         wpool_ref, pscale_ref, wout_ref, o_ref, upad, ypool, knorm) = refs
    else:
        (x_ref, q_ref, k_ref, v_ref, u_ref, mod_ref, pairsum_ref, lamp_ref, subg_ref,
         wpool_ref, pscale_ref, wout_ref, o_ref, upad, ypool, knorm) = refs
    qi = pl.program_id(1)
    lane = lax.broadcasted_iota(jnp.int32, (1, 128), 1)

    def pair_norms2(rows):
        xf = rows.astype(F32)
        return jnp.dot((xf * xf).astype(BF16), pairsum_ref[...], preferred_element_type=F32)

    @pl.when(qi == 0)
    def _key_norms():
        n2 = jnp.max(pair_norms2(k_ref[...]), axis=0, keepdims=True)
        if n_cache:
            n2 = jnp.maximum(n2, jnp.max(pair_norms2(ck_ref[...]), axis=0, keepdims=True))
        knorm[0:1, :] = n2 * NORM_SLACK

    @pl.when(qi == 0)
    def _pool():
        zeros = jnp.zeros((POOL_PAD, POOL_WIDTH), F32)
        upad[0:POOL_PAD, :] = zeros
        upad[POOL_PAD + seq:POOL_PAD + seq + POOL_PAD, :] = zeros
        upad[POOL_PAD:POOL_PAD + seq, :] = u_ref[...]
        rb = 256
        n_blk = rb + 2 * POOL_PAD

        def pool_rows(r, carry):
            base = pl.multiple_of(r * rb, rb)
            t = lax.broadcasted_iota(jnp.int32, (rb, 1), 0) + r * rb
            for g, w in enumerate(POOL_WINDOWS):
                cols = slice(g * POOL_GROUP, (g + 1) * POOL_GROUP)
                xg = upad[pl.ds(base, n_blk), cols]
                p = xg + pltpu.roll(xg, 1, axis=0)
                half = 1
                while 2 * half < w:
                    p = pltpu.roll(p, half, axis=0) + pltpu.roll(p, n_blk - half, axis=0)
                    half *= 2
                cnt = (jnp.minimum(t - w // 2 + w, seq) - jnp.maximum(t - w // 2, 0)).astype(F32)
                y = p[POOL_PAD:POOL_PAD + rb] / cnt - xg[POOL_PAD:POOL_PAD + rb]
                ypool[pl.ds(base, rb), cols] = y.astype(BF16)
            return carry

        lax.fori_loop(0, seq // rb, pool_rows, 0)

    lp = lamp_ref[...]
    lam = (jnp.exp(jnp.sum(lp[0:1, :] * lp[1:2, :], axis=-1, keepdims=True))
           - jnp.exp(jnp.sum(lp[2:3, :] * lp[3:4, :], axis=-1, keepdims=True)) + lam_init)

    m1 = jnp.where(lane < ATT_DH, 1.0, 0.0).astype(BF16)
    m2 = jnp.where(lane >= ATT_DH, 1.0, 0.0).astype(BF16)
    nt = (((1,), (1,)), ((), ()))

    bound2 = pair_norms2(q_ref[...]) * knorm[0:1, :]
    worst2 = jnp.max(bound2)

    def scores(h):
        cols = slice(h * 128, (h + 1) * 128)
        qh = q_ref[:, cols]
        q2 = jnp.concatenate([qh * m1, qh * m2], axis=0)
        s = lax.dot_general(q2, k_ref[:, cols].astype(BF16), nt, preferred_element_type=F32)
        sc = None
        if n_cache:
            sc = lax.dot_general(q2, ck_ref[:, cols].astype(BF16), nt, preferred_element_type=F32)
        return s, sc

    def mix(exact_max):
        heads = []
        pending = scores(0)
        for h in range(ATT_HEADS):
            cols = slice(h * 128, (h + 1) * 128)
            s, sc = pending
            if h + 1 < ATT_HEADS:
                pending = scores(h + 1)
            vh = v_ref[:, cols].astype(BF16)
            if n_cache:
                cvh = cv_ref[:, cols].astype(BF16)
            if exact_max:
                mx = jnp.max(s, axis=-1, keepdims=True)
                if n_cache:
                    mx = jnp.maximum(mx, jnp.max(sc, axis=-1, keepdims=True))
            else:
                l1, l2 = NORM_LANES * 2 * h, NORM_LANES * (2 * h + 1)
                mx = jnp.sqrt(jnp.concatenate([bound2[:, l1:l1 + 1], bound2[:, l2:l2 + 1]], axis=0))
            e = jnp.exp2(s - mx)
            den = jnp.sum(e, axis=-1, keepdims=True)
            if n_cache:
                ec = jnp.exp2(sc - mx)
                den = den + jnp.sum(ec, axis=-1, keepdims=True)
            inv1 = 1.0 / den[:tq]
            ratio = lam * den[:tq] / den[tq:]
            a = (e[:tq] - ratio * e[tq:]).astype(BF16)
            o = jnp.dot(a, vh, preferred_element_type=F32)
            if n_cache:
                ac = (ec[:tq] - ratio * ec[tq:]).astype(BF16)
                o = o + jnp.dot(ac, cvh, preferred_element_type=F32)
            o = o * inv1
            o = o * lax.rsqrt(jnp.mean(o * o, axis=-1, keepdims=True) + EPS)
            heads.append((o * subg_ref[:, cols]) * (1.0 - lam_init))
        o_att = jnp.concatenate(heads, axis=1).astype(BF16)

        yq = ypool[pl.ds(pl.multiple_of(qi * tq, tq), tq), :]
        pooled = [jnp.dot(yq[:, g * POOL_GROUP:(g + 1) * POOL_GROUP], wpool_ref[g],
                          preferred_element_type=F32) for g in range(len(POOL_WINDOWS))]
        p = (jnp.concatenate(pooled, axis=1) * pscale_ref[...]).astype(BF16)

        out = (jnp.dot(o_att, wout_ref[0:ATT_WIDTH, :], preferred_element_type=F32)
               + jnp.dot(p, wout_ref[ATT_WIDTH:, :], preferred_element_type=F32))
        o_ref[...] = x_ref[...] + mod_ref[2:3, :] * out

    lax.cond(worst2 <= SAFE_SHIFT * SAFE_SHIFT, lambda: mix(False), lambda: mix(True))


def _even_mix(x, q, k, v, u, cache, mod, layer, group_of_seq, lamp, subg, wpool_bf, pscale,
              wout_bf, seq, lam_init):
    n = x.shape[0]
    nseq = n // seq
    tq = 256
    nq = seq // tq
    blk = lambda b, qi: (b * nq + qi, 0)
    whole = lambda b, qi: (b, 0)
    in_specs = [pl.BlockSpec((tq, D_MODEL), blk), pl.BlockSpec((tq, 512), blk),
                pl.BlockSpec((seq, 512), whole), pl.BlockSpec((seq, 512), whole),
                pl.BlockSpec((seq, 512), whole)]
    args = [x, q, k, v, u]
    n_cache = 0
    if cache is not None:
        ck, cv, il = cache
        n_cache = ck.shape[2]
        cspec = pl.BlockSpec((None, None, n_cache, 512), lambda b, qi: (b, il, 0, 0))
        in_specs += [cspec, cspec]
        args += [ck, cv]
    pairsum = (jnp.arange(ATT_WIDTH)[:, None] // ATT_DH == jnp.arange(128)[None, :] // NORM_LANES).astype(BF16)
    weights = [lamp, subg, wpool_bf, pscale, wout_bf]
    in_specs += ([_mod_spec(layer, lambda b, qi: group_of_seq(b)), _resident((ATT_WIDTH, 128))]
                 + [_layer_of(*w) for w in weights])
    args += [mod, pairsum] + [w[0] for w in weights]
    return pl.pallas_call(
        functools.partial(_even_mix_kernel, seq=seq, tq=tq, n_cache=n_cache, lam_init=lam_init),
        grid=(nseq, nq),
        in_specs=in_specs,
        out_specs=pl.BlockSpec((tq, D_MODEL), blk),
        out_shape=jax.ShapeDtypeStruct((n, D_MODEL), F32),
        scratch_shapes=[pltpu.VMEM((seq + 2 * POOL_PAD, POOL_WIDTH), F32),
                        pltpu.VMEM((seq, POOL_WIDTH), BF16),
                        pltpu.VMEM((8, 128), F32)],
        compiler_params=_params(2),
        name="even_mix",
    )(*args)


def _ffn_kernel(*refs, gla_pre, final):
    refs = list(refs)
    x_ref, mod_ref = refs.pop(0), refs.pop(0)
    if gla_pre:
        og_ref, r_ref, gn_ref, wmix_ref = (refs.pop(0) for _ in range(4))
    g_ref, wi_ref, wo_ref = (refs.pop(0) for _ in range(3))
    fg_ref = refs.pop(0) if final else None
    o_ref, acc_ref = refs
    x = x_ref[...]
    if gla_pre:
        parts = []
        for h in range(GLA_HEADS):
            cols = slice(h * GLA_DV, (h + 1) * GLA_DV)
            oh = og_ref[:, cols]
            oh = oh * lax.rsqrt(jnp.mean(oh * oh, axis=-1, keepdims=True) + EPS)
            parts.append(oh * gn_ref[:, cols])
        z = (jnp.concatenate(parts, axis=1) * _silu(r_ref[...])).astype(BF16)
        x = x + mod_ref[2:3, :] * jnp.dot(z, wmix_ref[...], preferred_element_type=F32)
    h = _norm_mod(x, g_ref[...], mod_ref[3:4, :], mod_ref[4:5, :]).astype(BF16)
    for c in range(FFN_HIDDEN // FFN_CHUNK):
        lo, hi = c * FFN_CHUNK, (c + 1) * FFN_CHUNK
        a = jnp.dot(h, wi_ref[:, lo:hi], preferred_element_type=F32)
        b = jnp.dot(h, wi_ref[:, FFN_HIDDEN + lo:FFN_HIDDEN + hi], preferred_element_type=F32)
        z = (_silu(a) * b).astype(BF16)
        part = jnp.dot(z, wo_ref[lo:hi, :], preferred_element_type=F32)
        if c == 0:
            acc_ref[...] = part
        else:
            acc_ref[...] += part
    y = x + mod_ref[5:6, :] * acc_ref[...]
    if final:
        y = (y * lax.rsqrt(jnp.mean(y * y, axis=-1, keepdims=True) + EPS)) * fg_ref[...]
    o_ref[...] = y


def _ffn(x, mod, layer, group_of_step, g2, wi_bf, wo_bf, gla_pre=None, final_g=None):
    n = x.shape[0]
    tm = FFN_TM
    row = lambda i: (i, 0)
    tile = pl.BlockSpec((tm, D_MODEL), row)
    in_specs = [tile, _mod_spec(layer, group_of_step)]
    args = [x, mod]
    if gla_pre is not None:
        o, r, gn, wmix_bf = gla_pre
        in_specs += [tile, tile, _layer_of(*gn), _layer_of(*wmix_bf)]
        args += [o, r, gn[0], wmix_bf[0]]
    in_specs += [_layer_of(*g2), _layer_of(*wi_bf), _layer_of(*wo_bf)]
    args += [g2[0], wi_bf[0], wo_bf[0]]
    if final_g is not None:
        in_specs.append(_resident((1, D_MODEL)))
        args.append(final_g)
    return pl.pallas_call(
        functools.partial(_ffn_kernel, gla_pre=gla_pre is not None, final=final_g is not None),
        grid=(n // tm,),
        in_specs=in_specs,
        out_specs=tile,
        out_shape=jax.ShapeDtypeStruct((n, D_MODEL), F32),
        scratch_shapes=[pltpu.VMEM((tm, D_MODEL), F32)],
        compiler_params=_params(1),
        name="ffn",
    )(*args)


def _chunk_scan(x, reverse):
    n = x.shape[0]
    pos = lax.broadcasted_iota(jnp.int32, x.shape, 0) % CHUNK
    step = 1
    while step < CHUNK:
        if reverse:
            x = x + jnp.where(pos < CHUNK - step, pltpu.roll(x, n - step, axis=0), 0.0)
        else:
            x = x + jnp.where(pos >= step, pltpu.roll(x, step, axis=0), 0.0)
        step *= 2
    return x


def _odd_in_kernel(x_ref, mod_ref, g_ref, w_ref, wgd_ref, wup_ref, bg_ref,
                   qk_ref, v_ref, r_ref, gate_ref, *, tm, rc):
    shift, scale, g = mod_ref[0:1, :], mod_ref[1:2, :], g_ref[...]
    for c in range(tm // rc):
        rows = slice(c * rc, (c + 1) * rc)
        h = _norm_mod(x_ref[rows, :], g, shift, scale).astype(BF16)
        gd = jnp.dot(h, wgd_ref[...], preferred_element_type=F32).astype(BF16)
        logit = jnp.dot(gd, wup_ref[...], preferred_element_type=F32) + bg_ref[...]
        qk_ref[rows, :] = jnp.dot(h, w_ref[:, 0:1024], preferred_element_type=F32)
        v_ref[rows, :] = jnp.dot(h, w_ref[:, 1024:2048], preferred_element_type=F32).astype(BF16)
        r_ref[rows, :] = jnp.dot(h, w_ref[:, 2048:3072], preferred_element_type=F32)
        log_sig = jnp.minimum(logit, 0.0) - jnp.log(1.0 + jnp.exp(-jnp.abs(logit)))
        gate_ref[rows, :] = log_sig / GATE_TEMP


def _odd_in(x, mod, layer, group_of_step, g1, w_bf, wgd_bf, wup_bf, bg):
    n = x.shape[0]
    tm, rc = 1024, 512
    row = lambda i: (i, 0)
    return pl.pallas_call(
        functools.partial(_odd_in_kernel, tm=tm, rc=rc),
        grid=(n // tm,),
        in_specs=[pl.BlockSpec((tm, D_MODEL), row), _mod_spec(layer, group_of_step),
                  _layer_of(*g1), _layer_of(*w_bf),
                  _resident((D_MODEL, 128)), _resident((128, D_MODEL)), _layer_of(*bg)],
        out_specs=[pl.BlockSpec((tm, D_MODEL), row)] * 4,
        out_shape=[jax.ShapeDtypeStruct((n, D_MODEL), dt) for dt in (F32, BF16, F32, F32)],
        compiler_params=_params(1),
        name="odd_in",
    )(x, mod, g1[0], w_bf[0], wgd_bf, wup_bf, bg[0])


def _gla_kernel(*refs, tb, nsb, has_init, emit_state):
    refs = list(refs)
    ins = [refs.pop(0) for _ in range(6)]
    init_ref = refs.pop(0) if has_init else None
    o_ref = refs.pop(0)
    state_ref = refs.pop(0) if emit_state else None
    st = refs.pop(0)
    n = pl.program_id(1)

    @pl.when(n == 0)
    def _init():
        o_ref[...] = jnp.zeros(o_ref.shape, F32)
        if has_init:
            st[...] = init_ref[...]
        else:
            st[...] = jnp.zeros(st.shape, F32)

    ri = lax.broadcasted_iota(jnp.int32, (GLA_STEP, GLA_STEP), 0)
    ci = lax.broadcasted_iota(jnp.int32, (GLA_STEP, GLA_STEP), 1)
    first = lax.broadcasted_iota(jnp.int32, (GLA_STEP, 512), 0) < CHUNK
    nt = (((1,), (1,)), ((), ()))
    nstep = tb // GLA_STEP

    def body(i, carry):
        for d in range(2):
            qk_ref, v_ref, g_ref = ins[3 * d:3 * d + 3]
            c = i if d == 0 else nstep - 1 - i
            block = n if d == 0 else nsb - 1 - n
            rows = pl.ds(pl.multiple_of(c * GLA_STEP, GLA_STEP), GLA_STEP)
            orows = pl.ds(pl.multiple_of(block * tb + c * GLA_STEP, GLA_STEP), GLA_STEP)
            p = _chunk_scan(g_ref[rows, :], reverse=d == 1)
            if d == 0:
                mask = ci <= ri
                tot_a, tot_b = p[CHUNK - 1:CHUNK, :], p[GLA_STEP - 1:GLA_STEP, :]
                b_mid = jnp.where(first, p - tot_a, p)
                b_abs = jnp.where(first, p, p + tot_a)
            else:
                mask = ci >= ri
                tot_a, tot_b = p[0:1, :], p[CHUNK:CHUNK + 1, :]
                b_mid = jnp.where(first, p, p - tot_b)
                b_abs = jnp.where(first, p + tot_b, p)
            tot = tot_a + tot_b
            q = qk_ref[rows, 0:512] * (GLA_DK ** -0.5)
            k = qk_ref[rows, 512:1024]
            q_t = (q * jnp.exp(b_mid)).astype(BF16)
            k_t = (k * jnp.exp(-b_mid)).astype(BF16)
            q_in = (q * jnp.exp(b_abs)).astype(BF16)
            k_end = k * jnp.exp(tot - b_abs)
            decay = jnp.exp(jnp.broadcast_to(tot, (8, 512)).T[:, 0:1])
            for h in range(GLA_HEADS):
                kc = slice(h * GLA_DK, (h + 1) * GLA_DK)
                vc = slice(h * GLA_DV, (h + 1) * GLA_DV)
                vh = v_ref[rows, vc]
                a = lax.dot_general(q_t[:, kc], k_t[:, kc], nt, preferred_element_type=F32)
                a = jnp.where(mask, a, 0.0).astype(BF16)
                lhs = jnp.concatenate([a, k_end[:, kc].T.astype(BF16)], axis=0)
                both = jnp.dot(lhs, vh, preferred_element_type=F32)
                s = st[d, h]
                o = both[:GLA_STEP] + jnp.dot(q_in[:, kc], s.astype(BF16), preferred_element_type=F32)
                o_ref[orows, vc] += o
                st[d, h] = s * decay[kc, :] + both[GLA_STEP:]
        return carry

    lax.fori_loop(0, nstep, body, 0)

    if emit_state:
        @pl.when(n == pl.num_programs(1) - 1)
        def _emit():
            state_ref[...] = st[...]


def _gla(qk, v, gate, init, seq, emit_state):
    n = qk.shape[0]
    nseq = n // seq
    tb = 256
    nsb = seq // tb
    fwd = lambda b, s: (b * nsb + s, 0)
    bwd = lambda b, s: (b * nsb + nsb - 1 - s, 0)
    bwd_gate = lambda b, s: (b * nsb + nsb - 1 - s, 1)
    in_specs = [pl.BlockSpec((tb, 1024), fwd), pl.BlockSpec((tb, 1024), fwd), pl.BlockSpec((tb, 512), fwd),
                pl.BlockSpec((tb, 1024), bwd), pl.BlockSpec((tb, 1024), bwd), pl.BlockSpec((tb, 512), bwd_gate)]
    args = [qk, v, gate, qk, v, gate]
    if init is not None:
        state, il = init
        in_specs.append(pl.BlockSpec((None, None, 2, GLA_HEADS, GLA_DK, GLA_DV),
                                     lambda b, s: (b, il, 0, 0, 0, 0)))
        args.append(state)
    out_specs = [pl.BlockSpec((seq, 1024), lambda b, s: (b, 0))]
    out_shape = [jax.ShapeDtypeStruct((n, 1024), F32)]
    if emit_state:
        out_specs.append(pl.BlockSpec((None, 2, GLA_HEADS, GLA_DK, GLA_DV), lambda b, s: (b, 0, 0, 0, 0)))
        out_shape.append(jax.ShapeDtypeStruct((nseq, 2, GLA_HEADS, GLA_DK, GLA_DV), F32))
    return pl.pallas_call(
        functools.partial(_gla_kernel, tb=tb, nsb=nsb, has_init=init is not None, emit_state=emit_state),
        grid=(nseq, nsb),
        in_specs=in_specs,
        out_specs=out_specs,
        out_shape=out_shape,
        scratch_shapes=[pltpu.VMEM((2, GLA_HEADS, GLA_DK, GLA_DV), F32)],
        compiler_params=_params(2),
        name="gla",
    )(*args)


def _rope_tables(n_lat):
    half = ATT_DH // 2
    inv = 1.0 / (ROPE_BASE ** (jnp.arange(0, half, 2, dtype=F32) / half))
    t = jnp.arange(n_lat)
    rows = (t // GRID_W).astype(F32)
    cols = (t % GRID_W).astype(F32)

    def tab(pos):
        ang = pos[:, None] * inv[None, :]
        ang = jnp.concatenate([ang, ang], axis=-1)
        return jnp.cos(ang), jnp.sin(ang)

    (cr, sr), (cc, sc) = tab(rows), tab(cols)
    sign = jnp.concatenate([-jnp.ones((half // 2,), F32), jnp.ones((half // 2,), F32)])
    cos = jnp.concatenate([cr, cc], axis=-1)
    sin = jnp.concatenate([sr * sign, sc * sign], axis=-1)
    return jnp.tile(cos, (1, 2)), jnp.tile(sin, (1, 2))


def kernel(x_prompt, x_sample, c, cache_attn_k, cache_attn_v, state_gla, c_ctx, norm1_g, norm2_g, w_mod, b_mod, w_in_even, lam_params, subln_g, w_pool, pool_scale, w_out_even, w_in_odd, w_gate_up, b_gate, gla_norm_g, w_out_odd, w_ffn_in, w_ffn_out, final_g):
    bp, tp, _ = x_prompt.shape
    bs, ts, _ = x_sample.shape
    past = cache_attn_k.shape[2]
    n_even = cache_attn_k.shape[1]
    assert bs + 1 <= MOD_ROWS

    xp = x_prompt.reshape(bp * tp, D_MODEL)
    xs = x_sample.reshape(bs * ts, D_MODEL)
    ck = cache_attn_k.reshape(bs, n_even, past, ATT_WIDTH)
    cv = cache_attn_v.reshape(bs, n_even, past, ATT_WIDTH)

    cvec = jnp.concatenate([c_ctx[None, :], c, jnp.zeros((MOD_ROWS - 1 - bs, D_MODEL), F32)], axis=0)
    mod = _adaln(cvec, w_mod, b_mod)
    ctx_group = lambda i: 0

    def lat_group(tm):
        return lambda i: 1 + (i * tm) // ts

    rope = _rope_tables(ts)

    row3 = lambda a: a.reshape(a.shape[0], 1, a.shape[1])
    n1, n2 = row3(norm1_g), row3(norm2_g)
    w_in_even_bf, w_out_even_bf, w_pool_bf = (w.astype(BF16) for w in (w_in_even, w_out_even, w_pool))
    w_in_odd_bf, w_out_odd_bf = w_in_odd.astype(BF16), w_out_odd.astype(BF16)
    w_ffn_in_bf, w_ffn_out_bf = w_ffn_in.astype(BF16), w_ffn_out.astype(BF16)
    subg3, pscale3, gn3 = row3(subln_g), row3(pool_scale), row3(gla_norm_g)
    bg3 = b_gate.reshape(b_gate.shape[0], 1, D_MODEL)

    fg = final_g.reshape(1, D_MODEL)
    ks_new, vs_new, ss_new = [], [], []
    for l in range(DEPTH):
        i = l // 2
        g1, g2 = (n1, l), (n2, l)
        last = fg if l == DEPTH - 1 else None
        pre_p = pre_s = None
        if l % 2 == 0:
            lam_init = 0.8 - 0.6 * math.exp(-0.3 * l)
            w_in, w_out, wpool = (w_in_even_bf, i), (w_out_even_bf, i), (w_pool_bf, i)
            subg, pscale, lamp = (subg3, i), (pscale3, i), (lam_params, i)
            q, k, v, u = _even_in(xp, mod, l, ctx_group, g1, w_in, None, tp, F32)
            ks_new.append(k.reshape(bp, tp, ATT_WIDTH))
            vs_new.append(v.reshape(bp, tp, ATT_WIDTH))
            xp = _even_mix(xp, q, k, v, u, None, mod, l, ctx_group, lamp, subg, wpool, pscale,
                           w_out, tp, lam_init)
            q, k, v, u = _even_in(xs, mod, l, lat_group(1024), g1, w_in, rope, ts, BF16)
            xs = _even_mix(xs, q, k, v, u, (ck, cv, i), mod, l, lambda b: 1 + b, lamp, subg, wpool,
                           pscale, w_out, ts, lam_init)
        else:
            w_main, w_out, gn, bg = (w_in_odd_bf, i), (w_out_odd_bf, i), (gn3, i), (bg3, i)
            wgd = jnp.pad(w_in_odd[i][:, 3072:], ((0, 0), (0, 128 - 2 * GATE_RANK))).astype(BF16)
            wup = jnp.zeros((128, D_MODEL), F32)
            wup = wup.at[0:GATE_RANK, 0:512].set(w_gate_up[i, 0])
            wup = wup.at[GATE_RANK:2 * GATE_RANK, 512:1024].set(w_gate_up[i, 1]).astype(BF16)
            qk, v, r, gate = _odd_in(xp, mod, l, ctx_group, g1, w_main, wgd, wup, bg)
            o, s_ctx = _gla(qk, v, gate, None, tp, True)
            ss_new.append(s_ctx)
            pre_p = (o, r, gn, w_out)
            qk, v, r, gate = _odd_in(xs, mod, l, lat_group(1024), g1, w_main, wgd, wup, bg)
            (o,) = _gla(qk, v, gate, (state_gla, i), ts, False)
            pre_s = (o, r, gn, w_out)
        wi, wo = (w_ffn_in_bf, l), (w_ffn_out_bf, l)
        xp = _ffn(xp, mod, l, ctx_group, g2, wi, wo, pre_p, last)
        xs = _ffn(xs, mod, l, lat_group(FFN_TM), g2, wi, wo, pre_s, last)

    y_prompt = xp.reshape(bp, tp, D_MODEL)
    y_sample = xs.reshape(bs, ts, D_MODEL)
    new_attn_k = jnp.stack(ks_new, axis=1).reshape(bp, n_even, tp, ATT_HEADS, 2, ATT_DH)
    new_attn_v = jnp.stack(vs_new, axis=1).reshape(bp, n_even, tp, ATT_HEADS, ATT_DV)
    new_gla_state = jnp.stack(ss_new, axis=1)
    return (y_prompt, y_sample, new_attn_k, new_attn_v, new_gla_state)
```

```python
import functools
import math

import jax
import jax.numpy as jnp
from jax import lax
from jax.experimental import pallas as pl
from jax.experimental.pallas import tpu as pltpu

F32 = jnp.float32
BF16 = jnp.bfloat16

D_MODEL = 1024
DEPTH = 4
GRID_W = 64
ATT_WIDTH = 512
POOL_WIDTH = 512
ATT_HEADS = 4
ATT_DH = 64
ATT_DV = 128
ROPE_BASE = 10000.0
POOL_WINDOWS = (2, 4, 8, 16)
POOL_GROUP = 128
POOL_PAD = 8
GLA_HEADS = 4
GLA_DK = 128
GLA_DV = 256
GATE_RANK = 16
GATE_TEMP = 16.0
CHUNK = 64
GLA_STEP = 2 * CHUNK
FFN_HIDDEN = 2816
FFN_CHUNK = 256
FFN_TM = 512
EPS = 1e-6
Q_SCALE = ATT_DH ** -0.5 * math.log2(math.e)
SAFE_SHIFT = 50.0
NORM_SLACK = 1.05
NORM_LANES = 16
MOD_ROWS = 16

V7X_VMEM_LIMIT = 56 * 1024 * 1024


def _params(n_axes, vmem_bytes=V7X_VMEM_LIMIT):
    return pltpu.CompilerParams(dimension_semantics=("arbitrary",) * n_axes,
                                vmem_limit_bytes=vmem_bytes)


def _resident(shape):
    nd = len(shape)
    return pl.BlockSpec(shape, lambda *_: (0,) * nd, pipeline_mode=pl.Buffered(1))


def _layer_of(stack, idx):
    nd = stack.ndim - 1
    return pl.BlockSpec((None,) + stack.shape[1:], lambda *_: (idx,) + (0,) * nd,
                        pipeline_mode=pl.Buffered(1))


def _norm_mod(x, g, shift, scale):
    y = x * lax.rsqrt(jnp.mean(x * x, axis=-1, keepdims=True) + EPS)
    return (y * g) * (1.0 + scale) + shift


def _silu(x):
    return x * jax.nn.sigmoid(x)


def _adaln_kernel(cv_ref, w_ref, b_ref, o_ref):
    s = _silu(cv_ref[...]).astype(BF16)
    o_ref[...] = jnp.dot(s, w_ref[...].astype(BF16), preferred_element_type=F32) + b_ref[...]


def _adaln(cvec, w_mod, b_mod):
    tn = 1536
    out = pl.pallas_call(
        _adaln_kernel,
        grid=(DEPTH, 6 * D_MODEL // tn),
        in_specs=[pl.BlockSpec((MOD_ROWS, D_MODEL), lambda l, j: (0, 0)),
                  pl.BlockSpec((None, D_MODEL, tn), lambda l, j: (l, 0, j)),
                  pl.BlockSpec((None, 1, tn), lambda l, j: (l, 0, j))],
        out_specs=pl.BlockSpec((None, MOD_ROWS, tn), lambda l, j: (l, 0, j)),
        out_shape=jax.ShapeDtypeStruct((DEPTH, MOD_ROWS, 6 * D_MODEL), F32),
        compiler_params=_params(2),
        name="adaln",
    )(cvec, w_mod, b_mod.reshape(DEPTH, 1, 6 * D_MODEL))
    return out.reshape(DEPTH, MOD_ROWS, 6, D_MODEL)


def _mod_spec(layer, group_of_step):
    return pl.BlockSpec((None, None, 6, D_MODEL),
                        lambda *ids: (layer, group_of_step(*ids), 0, 0))


def _even_in_kernel(*refs, tm, rc, rope):
    if rope:
        x_ref, mod_ref, g_ref, w_ref, cos_ref, sin_ref, q_ref, k_ref, v_ref, u_ref = refs
    else:
        x_ref, mod_ref, g_ref, w_ref, q_ref, k_ref, v_ref, u_ref = refs
    shift, scale, g = mod_ref[0:1, :], mod_ref[1:2, :], g_ref[...]
    lo = (lax.broadcasted_iota(jnp.int32, (rc, 128), 1) % 32) < 16

    def rotary(y, cos, sin):
        parts = []
        for h in range(ATT_HEADS):
            yh = y[:, h * 128:(h + 1) * 128]
            fwd = pltpu.roll(yh, 128 - 16, axis=1)
            bwd = pltpu.roll(yh, 16, axis=1)
            parts.append(yh * cos + jnp.where(lo, fwd, bwd) * sin)
        return jnp.concatenate(parts, axis=1)

    for r in range(tm // rc):
        rows = slice(r * rc, (r + 1) * rc)
        h = _norm_mod(x_ref[rows, :], g, shift, scale).astype(BF16)
        q = jnp.dot(h, w_ref[:, 0:512], preferred_element_type=F32)
        k = jnp.dot(h, w_ref[:, 512:1024], preferred_element_type=F32)
        if rope:
            cos, sin = cos_ref[rows, :], sin_ref[rows, :]
            q, k = rotary(q, cos, sin), rotary(k, cos, sin)
        q_ref[rows, :] = (q * Q_SCALE).astype(q_ref.dtype)
        k_ref[rows, :] = k.astype(k_ref.dtype)
        v_ref[rows, :] = jnp.dot(h, w_ref[:, 1024:1536], preferred_element_type=F32).astype(v_ref.dtype)
        u_ref[rows, :] = jnp.dot(h, w_ref[:, 1536:2048], preferred_element_type=F32)


def _even_in(x, mod, layer, group_of_step, g1, w_bf, rope_tabs, seq_len, kv_dtype):
    n = x.shape[0]
    tm, rc = 1024, 512
    row = lambda i: (i, 0)
    in_specs = [pl.BlockSpec((tm, D_MODEL), row), _mod_spec(layer, group_of_step),
                _layer_of(*g1), _layer_of(*w_bf)]
    args = [x, mod, g1[0], w_bf[0]]
    if rope_tabs is not None:
        per_seq = seq_len // tm
        pos = lambda i: (i % per_seq, 0)
        in_specs += [pl.BlockSpec((tm, 128), pos), pl.BlockSpec((tm, 128), pos)]
        args += list(rope_tabs)
    out_dt = (BF16, kv_dtype, kv_dtype, F32)
    return pl.pallas_call(
        functools.partial(_even_in_kernel, tm=tm, rc=rc, rope=rope_tabs is not None),
        grid=(n // tm,),
        in_specs=in_specs,
        out_specs=[pl.BlockSpec((tm, 512), row)] * 4,
        out_shape=[jax.ShapeDtypeStruct((n, 512), dt) for dt in out_dt],
        compiler_params=_params(1),
        name="even_in",
    )(*args)


def _even_mix_kernel(*refs, seq, tq, n_cache, lam_init):
    if n_cache:
        (x_ref, q_ref, qseq_ref, k_ref, v_ref, u_ref, ck_ref, cv_ref, mod_ref, pairsum_ref, lamp_ref,
         subg_ref, wpool_ref, pscale_ref, wout_ref, o_ref, upad, bound) = refs
    else:
        (x_ref, q_ref, qseq_ref, k_ref, v_ref, u_ref, mod_ref, pairsum_ref, lamp_ref,
         subg_ref, wpool_ref, pscale_ref, wout_ref, o_ref, upad, bound) = refs
    qi = pl.program_id(1)
    lane = lax.broadcasted_iota(jnp.int32, (1, 128), 1)

    def max_pair_norm2(rows):
        xf = rows.astype(F32)
        n2 = jnp.dot((xf * xf).astype(BF16), pairsum_ref[...], preferred_element_type=F32)
        return jnp.max(n2, axis=0, keepdims=True)

    @pl.when(qi == 0)
    def _per_sequence():
        k2 = max_pair_norm2(k_ref[...])
        if n_cache:
            k2 = jnp.maximum(k2, max_pair_norm2(ck_ref[...]))
        bound[0:1, :] = max_pair_norm2(qseq_ref[...]) * k2 * NORM_SLACK
        zeros = jnp.zeros((POOL_PAD, POOL_WIDTH), F32)
        upad[0:POOL_PAD, :] = zeros
        upad[POOL_PAD + seq:POOL_PAD + seq + POOL_PAD, :] = zeros
        upad[POOL_PAD:POOL_PAD + seq, :] = u_ref[...]

    def pooled_rows():
        base = pl.multiple_of(qi * tq, tq)
        n_blk = tq + 2 * POOL_PAD
        t = lax.broadcasted_iota(jnp.int32, (tq, 1), 0) + qi * tq
        ys = []
        for g, w in enumerate(POOL_WINDOWS):
            cols = slice(g * POOL_GROUP, (g + 1) * POOL_GROUP)
            xg = upad[pl.ds(base, n_blk), cols]
            p = xg + pltpu.roll(xg, 1, axis=0)
            half = 1
            while 2 * half < w:
                p = pltpu.roll(p, half, axis=0) + pltpu.roll(p, n_blk - half, axis=0)
                half *= 2
            cnt = (jnp.minimum(t - w // 2 + w, seq) - jnp.maximum(t - w // 2, 0)).astype(F32)
            ys.append((p[POOL_PAD:POOL_PAD + tq] / cnt - xg[POOL_PAD:POOL_PAD + tq]).astype(BF16))
        return ys

    lp = lamp_ref[...]
    lam = (jnp.exp(jnp.sum(lp[0:1, :] * lp[1:2, :], axis=-1, keepdims=True))
           - jnp.exp(jnp.sum(lp[2:3, :] * lp[3:4, :], axis=-1, keepdims=True)) + lam_init)

    m1 = jnp.where(lane < ATT_DH, 1.0, 0.0).astype(BF16)
    m2 = jnp.where(lane >= ATT_DH, 1.0, 0.0).astype(BF16)
    nt = (((1,), (1,)), ((), ()))

    bound2 = bound[0:1, :]
    worst2 = jnp.max(bound2)

    def scores(h):
        cols = slice(h * 128, (h + 1) * 128)
        qh = q_ref[:, cols]
        q2 = jnp.concatenate([qh * m1, qh * m2], axis=0)
        s = lax.dot_general(q2, k_ref[:, cols].astype(BF16), nt, preferred_element_type=F32)
        sc = None
        if n_cache:
            sc = lax.dot_general(q2, ck_ref[:, cols].astype(BF16), nt, preferred_element_type=F32)
        return s, sc

    def mix(exact_max):
        heads = []
        pending = scores(0)
        ys = pooled_rows()
        for h in range(ATT_HEADS):
            cols = slice(h * 128, (h + 1) * 128)
            s, sc = pending
            if h + 1 < ATT_HEADS:
                pending = scores(h + 1)
            vh = v_ref[:, cols].astype(BF16)
            if n_cache:
                cvh = cv_ref[:, cols].astype(BF16)
            if exact_max:
                mx = jnp.max(s, axis=-1, keepdims=True)
                if n_cache:
                    mx = jnp.maximum(mx, jnp.max(sc, axis=-1, keepdims=True))
            else:
                l1, l2 = NORM_LANES * 2 * h, NORM_LANES * (2 * h + 1)
                mx = jnp.sqrt(jnp.concatenate([jnp.broadcast_to(bound2[:, l1:l1 + 1], (tq, 1)),
                                               jnp.broadcast_to(bound2[:, l2:l2 + 1], (tq, 1))], axis=0))
            e = jnp.exp2(s - mx)
            den = jnp.sum(e, axis=-1, keepdims=True)
            if n_cache:
                ec = jnp.exp2(sc - mx)
                den = den + jnp.sum(ec, axis=-1, keepdims=True)
            inv1 = 1.0 / den[:tq]
            ratio = lam * den[:tq] / den[tq:]
            a = (e[:tq] - ratio * e[tq:]).astype(BF16)
            o = jnp.dot(a, vh, preferred_element_type=F32)
            if n_cache:
                ac = (ec[:tq] - ratio * ec[tq:]).astype(BF16)
                o = o + jnp.dot(ac, cvh, preferred_element_type=F32)
            o = o * inv1
            o = o * lax.rsqrt(jnp.mean(o * o, axis=-1, keepdims=True) + EPS)
            heads.append((o * subg_ref[:, cols]) * (1.0 - lam_init))
        o_att = jnp.concatenate(heads, axis=1).astype(BF16)

        pooled = [jnp.dot(y, wpool_ref[g], preferred_element_type=F32) for g, y in enumerate(ys)]
        p = (jnp.concatenate(pooled, axis=1) * pscale_ref[...]).astype(BF16)

        out = (jnp.dot(o_att, wout_ref[0:ATT_WIDTH, :], preferred_element_type=F32)
               + jnp.dot(p, wout_ref[ATT_WIDTH:, :], preferred_element_type=F32))
        o_ref[...] = x_ref[...] + mod_ref[2:3, :] * out

    lax.cond(worst2 <= SAFE_SHIFT * SAFE_SHIFT, lambda: mix(False), lambda: mix(True))


def _even_mix(x, q, k, v, u, cache, mod, layer, group_of_seq, lamp, subg, wpool_bf, pscale,
              wout_bf, seq, lam_init):
    n = x.shape[0]
    nseq = n // seq
    tq = 256
    nq = seq // tq
    blk = lambda b, qi: (b * nq + qi, 0)
    whole = lambda b, qi: (b, 0)
    in_specs = [pl.BlockSpec((tq, D_MODEL), blk), pl.BlockSpec((tq, 512), blk),
                pl.BlockSpec((seq, 512), whole), pl.BlockSpec((seq, 512), whole),
                pl.BlockSpec((seq, 512), whole), pl.BlockSpec((seq, 512), whole)]
    args = [x, q, q, k, v, u]
    n_cache = 0
    if cache is not None:
        ck, cv, il = cache
        n_cache = ck.shape[2]
        cspec = pl.BlockSpec((None, None, n_cache, 512), lambda b, qi: (b, il, 0, 0))
        in_specs += [cspec, cspec]
        args += [ck, cv]
    pairsum = (jnp.arange(ATT_WIDTH)[:, None] // ATT_DH == jnp.arange(128)[None, :] // NORM_LANES).astype(BF16)
    weights = [lamp, subg, wpool_bf, pscale, wout_bf]
    in_specs += ([_mod_spec(layer, lambda b, qi: group_of_seq(b)), _resident((ATT_WIDTH, 128))]
                 + [_layer_of(*w) for w in weights])
    args += [mod, pairsum] + [w[0] for w in weights]
    return pl.pallas_call(
        functools.partial(_even_mix_kernel, seq=seq, tq=tq, n_cache=n_cache, lam_init=lam_init),
        grid=(nseq, nq),
        in_specs=in_specs,
        out_specs=pl.BlockSpec((tq, D_MODEL), blk),
        out_shape=jax.ShapeDtypeStruct((n, D_MODEL), F32),
        scratch_shapes=[pltpu.VMEM((seq + 2 * POOL_PAD, POOL_WIDTH), F32),
                        pltpu.VMEM((8, 128), F32)],
        compiler_params=_params(2),
        name="even_mix",
    )(*args)


def _ffn_kernel(*refs, gla_pre, final):
    refs = list(refs)
    x_ref, mod_ref = refs.pop(0), refs.pop(0)
    if gla_pre:
        og_ref, r_ref, gn_ref, wmix_ref = (refs.pop(0) for _ in range(4))
    g_ref, wi_ref, wo_ref = (refs.pop(0) for _ in range(3))
    fg_ref = refs.pop(0) if final else None
    o_ref, acc_ref = refs
    x = x_ref[...]
    if gla_pre:
        parts = []
        for h in range(GLA_HEADS):
            cols = slice(h * GLA_DV, (h + 1) * GLA_DV)
            oh = og_ref[:, cols]
            oh = oh * lax.rsqrt(jnp.mean(oh * oh, axis=-1, keepdims=True) + EPS)
            parts.append(oh * gn_ref[:, cols])
        z = (jnp.concatenate(parts, axis=1) * _silu(r_ref[...])).astype(BF16)
        x = x + mod_ref[2:3, :] * jnp.dot(z, wmix_ref[...], preferred_element_type=F32)
    h = _norm_mod(x, g_ref[...], mod_ref[3:4, :], mod_ref[4:5, :]).astype(BF16)
    for c in range(FFN_HIDDEN // FFN_CHUNK):
        lo, hi = c * FFN_CHUNK, (c + 1) * FFN_CHUNK
        a = jnp.dot(h, wi_ref[:, lo:hi], preferred_element_type=F32)
        b = jnp.dot(h, wi_ref[:, FFN_HIDDEN + lo:FFN_HIDDEN + hi], preferred_element_type=F32)
        z = (_silu(a) * b).astype(BF16)
        part = jnp.dot(z, wo_ref[lo:hi, :], preferred_element_type=F32)
        if c == 0:
            acc_ref[...] = part
        else:
            acc_ref[...] += part
    y = x + mod_ref[5:6, :] * acc_ref[...]
    if final:
        y = (y * lax.rsqrt(jnp.mean(y * y, axis=-1, keepdims=True) + EPS)) * fg_ref[...]
    o_ref[...] = y


def _ffn(x, mod, layer, group_of_step, g2, wi_bf, wo_bf, gla_pre=None, final_g=None):
    n = x.shape[0]
    tm = FFN_TM
    row = lambda i: (i, 0)
    tile = pl.BlockSpec((tm, D_MODEL), row)
    in_specs = [tile, _mod_spec(layer, group_of_step)]
    args = [x, mod]
    if gla_pre is not None:
        o, r, gn, wmix_bf = gla_pre
        in_specs += [tile, tile, _layer_of(*gn), _layer_of(*wmix_bf)]
        args += [o, r, gn[0], wmix_bf[0]]
    in_specs += [_layer_of(*g2), _layer_of(*wi_bf), _layer_of(*wo_bf)]
    args += [g2[0], wi_bf[0], wo_bf[0]]
    if final_g is not None:
        in_specs.append(_resident((1, D_MODEL)))
        args.append(final_g)
    return pl.pallas_call(
        functools.partial(_ffn_kernel, gla_pre=gla_pre is not None, final=final_g is not None),
        grid=(n // tm,),
        in_specs=in_specs,
        out_specs=tile,
        out_shape=jax.ShapeDtypeStruct((n, D_MODEL), F32),
        scratch_shapes=[pltpu.VMEM((tm, D_MODEL), F32)],
        compiler_params=_params(1),
        name="ffn",
    )(*args)


def _chunk_scan(x, reverse):
    n = x.shape[0]
    pos = lax.broadcasted_iota(jnp.int32, x.shape, 0) % CHUNK
    step = 1
    while step < CHUNK:
        if reverse:
            x = x + jnp.where(pos < CHUNK - step, pltpu.roll(x, n - step, axis=0), 0.0)
        else:
            x = x + jnp.where(pos >= step, pltpu.roll(x, step, axis=0), 0.0)
        step *= 2
    return x


def _odd_in_kernel(x_ref, mod_ref, g_ref, w_ref, wgd_ref, wup_ref, bg_ref,
                   qk_ref, v_ref, r_ref, gate_ref, *, tm, rc):
    shift, scale, g = mod_ref[0:1, :], mod_ref[1:2, :], g_ref[...]
    for c in range(tm // rc):
        rows = slice(c * rc, (c + 1) * rc)
        h = _norm_mod(x_ref[rows, :], g, shift, scale).astype(BF16)
        gd = jnp.dot(h, wgd_ref[...], preferred_element_type=F32).astype(BF16)
        logit = jnp.dot(gd, wup_ref[...], preferred_element_type=F32) + bg_ref[...]
        qk_ref[rows, :] = jnp.dot(h, w_ref[:, 0:1024], preferred_element_type=F32)
        v_ref[rows, :] = jnp.dot(h, w_ref[:, 1024:2048], preferred_element_type=F32).astype(BF16)
        r_ref[rows, :] = jnp.dot(h, w_ref[:, 2048:3072], preferred_element_type=F32)
        log_sig = jnp.minimum(logit, 0.0) - jnp.log(1.0 + jnp.exp(-jnp.abs(logit)))
        gate_ref[rows, :] = log_sig / GATE_TEMP


def _odd_in(x, mod, layer, group_of_step, g1, w_bf, wgd_bf, wup_bf, bg):
    n = x.shape[0]
    tm, rc = 1024, 512
    row = lambda i: (i, 0)
    return pl.pallas_call(
        functools.partial(_odd_in_kernel, tm=tm, rc=rc),
        grid=(n // tm,),
        in_specs=[pl.BlockSpec((tm, D_MODEL), row), _mod_spec(layer, group_of_step),
                  _layer_of(*g1), _layer_of(*w_bf),
                  _resident((D_MODEL, 128)), _resident((128, D_MODEL)), _layer_of(*bg)],
        out_specs=[pl.BlockSpec((tm, D_MODEL), row)] * 4,
        out_shape=[jax.ShapeDtypeStruct((n, D_MODEL), dt) for dt in (F32, BF16, F32, F32)],
        compiler_params=_params(1),
        name="odd_in",
    )(x, mod, g1[0], w_bf[0], wgd_bf, wup_bf, bg[0])


def _gla_kernel(*refs, tb, nsb, has_init, emit_state):
    refs = list(refs)
    ins = [refs.pop(0) for _ in range(6)]
    init_ref = refs.pop(0) if has_init else None
    o_ref = refs.pop(0)
    state_ref = refs.pop(0) if emit_state else None
    st = refs.pop(0)
    n = pl.program_id(1)

    @pl.when(n == 0)
    def _init():
        o_ref[...] = jnp.zeros(o_ref.shape, F32)
        if has_init:
            st[...] = init_ref[...]
        else:
            st[...] = jnp.zeros(st.shape, F32)

    ri = lax.broadcasted_iota(jnp.int32, (GLA_STEP, GLA_STEP), 0)
    ci = lax.broadcasted_iota(jnp.int32, (GLA_STEP, GLA_STEP), 1)
    first = lax.broadcasted_iota(jnp.int32, (GLA_STEP, 512), 0) < CHUNK
    nt = (((1,), (1,)), ((), ()))
    nstep = tb // GLA_STEP

    def body(i, carry):
        for d in range(2):
            qk_ref, v_ref, g_ref = ins[3 * d:3 * d + 3]
            c = i if d == 0 else nstep - 1 - i
            block = n if d == 0 else nsb - 1 - n
            rows = pl.ds(pl.multiple_of(c * GLA_STEP, GLA_STEP), GLA_STEP)
            orows = pl.ds(pl.multiple_of(block * tb + c * GLA_STEP, GLA_STEP), GLA_STEP)
            p = _chunk_scan(g_ref[rows, :], reverse=d == 1)
            if d == 0:
                mask = ci <= ri
                tot_a, tot_b = p[CHUNK - 1:CHUNK, :], p[GLA_STEP - 1:GLA_STEP, :]
                b_mid = jnp.where(first, p - tot_a, p)
                b_abs = jnp.where(first, p, p + tot_a)
            else:
                mask = ci >= ri
                tot_a, tot_b = p[0:1, :], p[CHUNK:CHUNK + 1, :]
                b_mid = jnp.where(first, p, p - tot_b)
                b_abs = jnp.where(first, p + tot_b, p)
            tot = tot_a + tot_b
            q = qk_ref[rows, 0:512] * (GLA_DK ** -0.5)
            k = qk_ref[rows, 512:1024]
            q_t = (q * jnp.exp(b_mid)).astype(BF16)
            k_t = (k * jnp.exp(-b_mid)).astype(BF16)
            q_in = (q * jnp.exp(b_abs)).astype(BF16)
            k_end = k * jnp.exp(tot - b_abs)
            decay = jnp.exp(jnp.broadcast_to(tot, (8, 512)).T[:, 0:1])
            for h in range(GLA_HEADS):
                kc = slice(h * GLA_DK, (h + 1) * GLA_DK)
                vc = slice(h * GLA_DV, (h + 1) * GLA_DV)
                vh = v_ref[rows, vc]
                a = lax.dot_general(q_t[:, kc], k_t[:, kc], nt, preferred_element_type=F32)
                a = jnp.where(mask, a, 0.0).astype(BF16)
                lhs = jnp.concatenate([a, k_end[:, kc].T.astype(BF16)], axis=0)
                both = jnp.dot(lhs, vh, preferred_element_type=F32)
                s = st[d, h]
                o = both[:GLA_STEP] + jnp.dot(q_in[:, kc], s.astype(BF16), preferred_element_type=F32)
                o_ref[orows, vc] += o
                st[d, h] = s * decay[kc, :] + both[GLA_STEP:]
        return carry

    lax.fori_loop(0, nstep, body, 0)

    if emit_state:
        @pl.when(n == pl.num_programs(1) - 1)
        def _emit():
            state_ref[...] = st[...]


def _gla(qk, v, gate, init, seq, emit_state):
    n = qk.shape[0]
    nseq = n // seq
    tb = min(seq, 512)
    nsb = seq // tb
    fwd = lambda b, s: (b * nsb + s, 0)
    bwd = lambda b, s: (b * nsb + nsb - 1 - s, 0)
    bwd_gate = lambda b, s: (b * nsb + nsb - 1 - s, 1)
    in_specs = [pl.BlockSpec((tb, 1024), fwd), pl.BlockSpec((tb, 1024), fwd), pl.BlockSpec((tb, 512), fwd),
                pl.BlockSpec((tb, 1024), bwd), pl.BlockSpec((tb, 1024), bwd), pl.BlockSpec((tb, 512), bwd_gate)]
    args = [qk, v, gate, qk, v, gate]
    if init is not None:
        state, il = init
        in_specs.append(pl.BlockSpec((None, None, 2, GLA_HEADS, GLA_DK, GLA_DV),
                                     lambda b, s: (b, il, 0, 0, 0, 0)))
        args.append(state)
    out_specs = [pl.BlockSpec((seq, 1024), lambda b, s: (b, 0))]
    out_shape = [jax.ShapeDtypeStruct((n, 1024), F32)]
    if emit_state:
        out_specs.append(pl.BlockSpec((None, 2, GLA_HEADS, GLA_DK, GLA_DV), lambda b, s: (b, 0, 0, 0, 0)))
        out_shape.append(jax.ShapeDtypeStruct((nseq, 2, GLA_HEADS, GLA_DK, GLA_DV), F32))
    return pl.pallas_call(
        functools.partial(_gla_kernel, tb=tb, nsb=nsb, has_init=init is not None, emit_state=emit_state),
        grid=(nseq, nsb),
        in_specs=in_specs,
        out_specs=out_specs,
        out_shape=out_shape,
        scratch_shapes=[pltpu.VMEM((2, GLA_HEADS, GLA_DK, GLA_DV), F32)],
        compiler_params=_params(2),
        name="gla",
    )(*args)


def _rope_tables(n_lat):
    half = ATT_DH // 2
    inv = 1.0 / (ROPE_BASE ** (jnp.arange(0, half, 2, dtype=F32) / half))
    t = jnp.arange(n_lat)
    rows = (t // GRID_W).astype(F32)
    cols = (t % GRID_W).astype(F32)

    def tab(pos):
        ang = pos[:, None] * inv[None, :]
        ang = jnp.concatenate([ang, ang], axis=-1)
        return jnp.cos(ang), jnp.sin(ang)

    (cr, sr), (cc, sc) = tab(rows), tab(cols)
    sign = jnp.concatenate([-jnp.ones((half // 2,), F32), jnp.ones((half // 2,), F32)])
    cos = jnp.concatenate([cr, cc], axis=-1)
    sin = jnp.concatenate([sr * sign, sc * sign], axis=-1)
    return jnp.tile(cos, (1, 2)), jnp.tile(sin, (1, 2))


def kernel(x_prompt, x_sample, c, cache_attn_k, cache_attn_v, state_gla, c_ctx, norm1_g, norm2_g, w_mod, b_mod, w_in_even, lam_params, subln_g, w_pool, pool_scale, w_out_even, w_in_odd, w_gate_up, b_gate, gla_norm_g, w_out_odd, w_ffn_in, w_ffn_out, final_g):
    bp, tp, _ = x_prompt.shape
    bs, ts, _ = x_sample.shape
    past = cache_attn_k.shape[2]
    n_even = cache_attn_k.shape[1]
    assert bs + 1 <= MOD_ROWS

    xp = x_prompt.reshape(bp * tp, D_MODEL)
    xs = x_sample.reshape(bs * ts, D_MODEL)
    ck = cache_attn_k.reshape(bs, n_even, past, ATT_WIDTH)
    cv = cache_attn_v.reshape(bs, n_even, past, ATT_WIDTH)

    cvec = jnp.concatenate([c_ctx[None, :], c, jnp.zeros((MOD_ROWS - 1 - bs, D_MODEL), F32)], axis=0)
    mod = _adaln(cvec, w_mod, b_mod)
    ctx_group = lambda i: 0

    def lat_group(tm):
        return lambda i: 1 + (i * tm) // ts

    rope = _rope_tables(ts)

    row3 = lambda a: a.reshape(a.shape[0], 1, a.shape[1])
    n1, n2 = row3(norm1_g), row3(norm2_g)
    w_in_even_bf, w_out_even_bf, w_pool_bf = (w.astype(BF16) for w in (w_in_even, w_out_even, w_pool))
    w_in_odd_bf, w_out_odd_bf = w_in_odd.astype(BF16), w_out_odd.astype(BF16)
    w_ffn_in_bf, w_ffn_out_bf = w_ffn_in.astype(BF16), w_ffn_out.astype(BF16)
    subg3, pscale3, gn3 = row3(subln_g), row3(pool_scale), row3(gla_norm_g)
    bg3 = b_gate.reshape(b_gate.shape[0], 1, D_MODEL)

    fg = final_g.reshape(1, D_MODEL)
    ks_new, vs_new, ss_new = [], [], []
    for l in range(DEPTH):
        i = l // 2
        g1, g2 = (n1, l), (n2, l)
        last = fg if l == DEPTH - 1 else None
        pre_p = pre_s = None
        if l % 2 == 0:
            lam_init = 0.8 - 0.6 * math.exp(-0.3 * l)
            w_in, w_out, wpool = (w_in_even_bf, i), (w_out_even_bf, i), (w_pool_bf, i)
            subg, pscale, lamp = (subg3, i), (pscale3, i), (lam_params, i)
            q, k, v, u = _even_in(xp, mod, l, ctx_group, g1, w_in, None, tp, F32)
            ks_new.append(k.reshape(bp, tp, ATT_WIDTH))
            vs_new.append(v.reshape(bp, tp, ATT_WIDTH))
            xp = _even_mix(xp, q, k, v, u, None, mod, l, ctx_group, lamp, subg, wpool, pscale,
                           w_out, tp, lam_init)
            q, k, v, u = _even_in(xs, mod, l, lat_group(1024), g1, w_in, rope, ts, BF16)
            xs = _even_mix(xs, q, k, v, u, (ck, cv, i), mod, l, lambda b: 1 + b, lamp, subg, wpool,
                           pscale, w_out, ts, lam_init)
        else:
            w_main, w_out, gn, bg = (w_in_odd_bf, i), (w_out_odd_bf, i), (gn3, i), (bg3, i)
            wgd = jnp.pad(w_in_odd[i][:, 3072:], ((0, 0), (0, 128 - 2 * GATE_RANK))).astype(BF16)
            wup = jnp.zeros((128, D_MODEL), F32)
            wup = wup.at[0:GATE_RANK, 0:512].set(w_gate_up[i, 0])
            wup = wup.at[GATE_RANK:2 * GATE_RANK, 512:1024].set(w_gate_up[i, 1]).astype(BF16)
            qk, v, r, gate = _odd_in(xp, mod, l, ctx_group, g1, w_main, wgd, wup, bg)
            o, s_ctx = _gla(qk, v, gate, None, tp, True)
            ss_new.append(s_ctx)
            pre_p = (o, r, gn, w_out)
            qk, v, r, gate = _odd_in(xs, mod, l, lat_group(1024), g1, w_main, wgd, wup, bg)
            (o,) = _gla(qk, v, gate, (state_gla, i), ts, False)
            pre_s = (o, r, gn, w_out)
        wi, wo = (w_ffn_in_bf, l), (w_ffn_out_bf, l)
        xp = _ffn(xp, mod, l, ctx_group, g2, wi, wo, pre_p, last)
        xs = _ffn(xs, mod, l, lat_group(FFN_TM), g2, wi, wo, pre_s, last)

    y_prompt = xp.reshape(bp, tp, D_MODEL)
    y_sample = xs.reshape(bs, ts, D_MODEL)
    new_attn_k = jnp.stack(ks_new, axis=1).reshape(bp, n_even, tp, ATT_HEADS, 2, ATT_DH)
    new_attn_v = jnp.stack(vs_new, axis=1).reshape(bp, n_even, tp, ATT_HEADS, ATT_DV)
    new_gla_state = jnp.stack(ss_new, axis=1)
    return (y_prompt, y_sample, new_attn_k, new_attn_v, new_gla_state)
```

```python
import functools
import math

import jax
import jax.numpy as jnp
from jax import lax
from jax.experimental import pallas as pl
from jax.experimental.pallas import tpu as pltpu

F32 = jnp.float32
BF16 = jnp.bfloat16

D_MODEL = 1024
DEPTH = 4
GRID_W = 64
ATT_WIDTH = 512
POOL_WIDTH = 512
ATT_HEADS = 4
ATT_DH = 64
ATT_DV = 128
ROPE_BASE = 10000.0
POOL_WINDOWS = (2, 4, 8, 16)
POOL_GROUP = 128
POOL_PAD = 8
GLA_HEADS = 4
GLA_DK = 128
GLA_DV = 256
GATE_RANK = 16
GATE_TEMP = 16.0
CHUNK = 64
GLA_STEP = 2 * CHUNK
FFN_HIDDEN = 2816
FFN_CHUNK = 256
FFN_TM = 512
EPS = 1e-6
Q_SCALE = ATT_DH ** -0.5 * math.log2(math.e)
SAFE_SHIFT = 50.0
NORM_SLACK = 1.05
NORM_LANES = 16
MOD_ROWS = 16

V7X_VMEM_LIMIT = 56 * 1024 * 1024


def _params(n_axes, vmem_bytes=V7X_VMEM_LIMIT):
    return pltpu.CompilerParams(dimension_semantics=("arbitrary",) * n_axes,
                                vmem_limit_bytes=vmem_bytes)


def _resident(shape):
    nd = len(shape)
    return pl.BlockSpec(shape, lambda *_: (0,) * nd, pipeline_mode=pl.Buffered(1))


def _layer_of(stack, idx):
    nd = stack.ndim - 1
    return pl.BlockSpec((None,) + stack.shape[1:], lambda *_: (idx,) + (0,) * nd,
                        pipeline_mode=pl.Buffered(1))


def _norm_mod(x, g, shift, scale):
    y = x * lax.rsqrt(jnp.mean(x * x, axis=-1, keepdims=True) + EPS)
    return (y * g) * (1.0 + scale) + shift


def _silu(x):
    return x * jax.nn.sigmoid(x)


def _adaln_kernel(cv_ref, w_ref, b_ref, o_ref):
    s = _silu(cv_ref[...]).astype(BF16)
    o_ref[...] = jnp.dot(s, w_ref[...].astype(BF16), preferred_element_type=F32) + b_ref[...]


def _adaln(cvec, w_mod, b_mod):
    tn = 1536
    out = pl.pallas_call(
        _adaln_kernel,
        grid=(DEPTH, 6 * D_MODEL // tn),
        in_specs=[pl.BlockSpec((MOD_ROWS, D_MODEL), lambda l, j: (0, 0)),
                  pl.BlockSpec((None, D_MODEL, tn), lambda l, j: (l, 0, j)),
                  pl.BlockSpec((None, 1, tn), lambda l, j: (l, 0, j))],
        out_specs=pl.BlockSpec((None, MOD_ROWS, tn), lambda l, j: (l, 0, j)),
        out_shape=jax.ShapeDtypeStruct((DEPTH, MOD_ROWS, 6 * D_MODEL), F32),
        compiler_params=_params(2),
        name="adaln",
    )(cvec, w_mod, b_mod.reshape(DEPTH, 1, 6 * D_MODEL))
    return out.reshape(DEPTH, MOD_ROWS, 6, D_MODEL)


def _mod_spec(layer, group_of_step):
    return pl.BlockSpec((None, None, 6, D_MODEL),
                        lambda *ids: (layer, group_of_step(*ids), 0, 0))


def _even_in_kernel(*refs, tm, rc, rope):
    if rope:
        x_ref, mod_ref, g_ref, w_ref, cos_ref, sin_ref, q_ref, k_ref, v_ref, u_ref = refs
    else:
        x_ref, mod_ref, g_ref, w_ref, q_ref, k_ref, v_ref, u_ref = refs
    shift, scale, g = mod_ref[0:1, :], mod_ref[1:2, :], g_ref[...]
    lo = (lax.broadcasted_iota(jnp.int32, (rc, 128), 1) % 32) < 16

    def rotary(y, cos, sin):
        parts = []
        for h in range(ATT_HEADS):
            yh = y[:, h * 128:(h + 1) * 128]
            fwd = pltpu.roll(yh, 128 - 16, axis=1)
            bwd = pltpu.roll(yh, 16, axis=1)
            parts.append(yh * cos + jnp.where(lo, fwd, bwd) * sin)
        return jnp.concatenate(parts, axis=1)

    for r in range(tm // rc):
        rows = slice(r * rc, (r + 1) * rc)
        h = _norm_mod(x_ref[rows, :], g, shift, scale).astype(BF16)
        q = jnp.dot(h, w_ref[:, 0:512], preferred_element_type=F32)
        k = jnp.dot(h, w_ref[:, 512:1024], preferred_element_type=F32)
        if rope:
            cos, sin = cos_ref[rows, :], sin_ref[rows, :]
            q, k = rotary(q, cos, sin), rotary(k, cos, sin)
        q_ref[rows, :] = (q * Q_SCALE).astype(q_ref.dtype)
        k_ref[rows, :] = k.astype(k_ref.dtype)
        v_ref[rows, :] = jnp.dot(h, w_ref[:, 1024:1536], preferred_element_type=F32).astype(v_ref.dtype)
        u_ref[rows, :] = jnp.dot(h, w_ref[:, 1536:2048], preferred_element_type=F32)


def _even_in(x, mod, layer, group_of_step, g1, w_bf, rope_tabs, seq_len, kv_dtype):
    n = x.shape[0]
    tm, rc = 1024, 512
    row = lambda i: (i, 0)
    in_specs = [pl.BlockSpec((tm, D_MODEL), row), _mod_spec(layer, group_of_step),
                _layer_of(*g1), _layer_of(*w_bf)]
    args = [x, mod, g1[0], w_bf[0]]
    if rope_tabs is not None:
        per_seq = seq_len // tm
        pos = lambda i: (i % per_seq, 0)
        in_specs += [pl.BlockSpec((tm, 128), pos), pl.BlockSpec((tm, 128), pos)]
        args += list(rope_tabs)
    out_dt = (BF16, kv_dtype, kv_dtype, F32)
    return pl.pallas_call(
        functools.partial(_even_in_kernel, tm=tm, rc=rc, rope=rope_tabs is not None),
        grid=(n // tm,),
        in_specs=in_specs,
        out_specs=[pl.BlockSpec((tm, 512), row)] * 4,
        out_shape=[jax.ShapeDtypeStruct((n, 512), dt) for dt in out_dt],
        compiler_params=_params(1),
        name="even_in",
    )(*args)


def _even_mix_kernel(*refs, seq, tq, n_cache, lam_init):
    if n_cache:
        (x_ref, q_ref, qseq_ref, k_ref, v_ref, u_ref, ck_ref, cv_ref, mod_ref, pairsum_ref, lamp_ref,
         subg_ref, wpool_ref, pscale_ref, wout_ref, o_ref, upad, bound) = refs
    else:
        (x_ref, q_ref, qseq_ref, k_ref, v_ref, u_ref, mod_ref, pairsum_ref, lamp_ref,
         subg_ref, wpool_ref, pscale_ref, wout_ref, o_ref, upad, bound) = refs
    qi = pl.program_id(1)
    lane = lax.broadcasted_iota(jnp.int32, (1, 128), 1)

    def max_pair_norm2(rows):
        xf = rows.astype(F32)
        n2 = jnp.dot((xf * xf).astype(BF16), pairsum_ref[...], preferred_element_type=F32)
        return jnp.max(n2, axis=0, keepdims=True)

    @pl.when(qi == 0)
    def _per_sequence():
        k2 = max_pair_norm2(k_ref[...])
        if n_cache:
            k2 = jnp.maximum(k2, max_pair_norm2(ck_ref[...]))
        bound[0:1, :] = max_pair_norm2(qseq_ref[...]) * k2 * NORM_SLACK
        zeros = jnp.zeros((POOL_PAD, POOL_WIDTH), F32)
        upad[0:POOL_PAD, :] = zeros
        upad[POOL_PAD + seq:POOL_PAD + seq + POOL_PAD, :] = zeros
        upad[POOL_PAD:POOL_PAD + seq, :] = u_ref[...]

    def pooled_rows():
        base = pl.multiple_of(qi * tq, tq)
        n_blk = tq + 2 * POOL_PAD
        t = lax.broadcasted_iota(jnp.int32, (tq, 1), 0) + qi * tq
        ys = []
        for g, w in enumerate(POOL_WINDOWS):
            cols = slice(g * POOL_GROUP, (g + 1) * POOL_GROUP)
            xg = upad[pl.ds(base, n_blk), cols]
            p = xg + pltpu.roll(xg, 1, axis=0)
            half = 1
            while 2 * half < w:
                p = pltpu.roll(p, half, axis=0) + pltpu.roll(p, n_blk - half, axis=0)
                half *= 2
            cnt = (jnp.minimum(t - w // 2 + w, seq) - jnp.maximum(t - w // 2, 0)).astype(F32)
            ys.append((p[POOL_PAD:POOL_PAD + tq] / cnt - xg[POOL_PAD:POOL_PAD + tq]).astype(BF16))
        return ys

    lp = lamp_ref[...]
    lam = (jnp.exp(jnp.sum(lp[0:1, :] * lp[1:2, :], axis=-1, keepdims=True))
           - jnp.exp(jnp.sum(lp[2:3, :] * lp[3:4, :], axis=-1, keepdims=True)) + lam_init)

    m1 = jnp.where(lane < ATT_DH, 1.0, 0.0).astype(BF16)
    m2 = jnp.where(lane >= ATT_DH, 1.0, 0.0).astype(BF16)
    nt = (((1,), (1,)), ((), ()))

    bound2 = bound[0:1, :]
    worst2 = jnp.max(bound2)

    def scores(h):
        cols = slice(h * 128, (h + 1) * 128)
        qh = q_ref[:, cols]
        q2 = jnp.concatenate([qh * m1, qh * m2], axis=0)
        s = lax.dot_general(q2, k_ref[:, cols].astype(BF16), nt, preferred_element_type=F32)
        sc = None
        if n_cache:
            sc = lax.dot_general(q2, ck_ref[:, cols].astype(BF16), nt, preferred_element_type=F32)
        return s, sc

    def mix(exact_max):
        heads = []
        pending = scores(0)
        ys = pooled_rows()
        for h in range(ATT_HEADS):
            cols = slice(h * 128, (h + 1) * 128)
            s, sc = pending
            if h + 1 < ATT_HEADS:
                pending = scores(h + 1)
            vh = v_ref[:, cols].astype(BF16)
            if n_cache:
                cvh = cv_ref[:, cols].astype(BF16)
            if exact_max:
                mx = jnp.max(s, axis=-1, keepdims=True)
                if n_cache:
                    mx = jnp.maximum(mx, jnp.max(sc, axis=-1, keepdims=True))
            else:
                l1, l2 = NORM_LANES * 2 * h, NORM_LANES * (2 * h + 1)
                mx = jnp.sqrt(jnp.concatenate([jnp.broadcast_to(bound2[:, l1:l1 + 1], (tq, 1)),
                                               jnp.broadcast_to(bound2[:, l2:l2 + 1], (tq, 1))], axis=0))
            e = jnp.exp2(s - mx)
            den = jnp.sum(e, axis=-1, keepdims=True)
            if n_cache:
                ec = jnp.exp2(sc - mx)
                den = den + jnp.sum(ec, axis=-1, keepdims=True)
            inv1 = 1.0 / den[:tq]
            ratio = lam * den[:tq] / den[tq:]
            a = (e[:tq] - ratio * e[tq:]).astype(BF16)
            o = jnp.dot(a, vh, preferred_element_type=F32)
            if n_cache:
                ac = (ec[:tq] - ratio * ec[tq:]).astype(BF16)
                o = o + jnp.dot(ac, cvh, preferred_element_type=F32)
            o = o * inv1
            o = o * lax.rsqrt(jnp.mean(o * o, axis=-1, keepdims=True) + EPS)
            heads.append((o * subg_ref[:, cols]) * (1.0 - lam_init))
        o_att = jnp.concatenate(heads, axis=1).astype(BF16)

        pooled = [jnp.dot(y, wpool_ref[g], preferred_element_type=F32) for g, y in enumerate(ys)]
        p = (jnp.concatenate(pooled, axis=1) * pscale_ref[...]).astype(BF16)

        out = (jnp.dot(o_att, wout_ref[0:ATT_WIDTH, :], preferred_element_type=F32)
               + jnp.dot(p, wout_ref[ATT_WIDTH:, :], preferred_element_type=F32))
        o_ref[...] = x_ref[...] + mod_ref[2:3, :] * out

    lax.cond(worst2 <= SAFE_SHIFT * SAFE_SHIFT, lambda: mix(False), lambda: mix(True))


def _even_mix(x, q, k, v, u, cache, mod, layer, group_of_seq, lamp, subg, wpool_bf, pscale,
              wout_bf, seq, lam_init):
    n = x.shape[0]
    nseq = n // seq
    tq = 256
    nq = seq // tq
    blk = lambda b, qi: (b * nq + qi, 0)
    whole = lambda b, qi: (b, 0)
    in_specs = [pl.BlockSpec((tq, D_MODEL), blk), pl.BlockSpec((tq, 512), blk),
                pl.BlockSpec((seq, 512), whole), pl.BlockSpec((seq, 512), whole),
                pl.BlockSpec((seq, 512), whole), pl.BlockSpec((seq, 512), whole)]
    args = [x, q, q, k, v, u]
    n_cache = 0
    if cache is not None:
        ck, cv, il = cache
        n_cache = ck.shape[2]
        cspec = pl.BlockSpec((None, None, n_cache, 512), lambda b, qi: (b, il, 0, 0))
        in_specs += [cspec, cspec]
        args += [ck, cv]
    pairsum = (jnp.arange(ATT_WIDTH)[:, None] // ATT_DH == jnp.arange(128)[None, :] // NORM_LANES).astype(BF16)
    weights = [lamp, subg, wpool_bf, pscale, wout_bf]
    in_specs += ([_mod_spec(layer, lambda b, qi: group_of_seq(b)), _resident((ATT_WIDTH, 128))]
                 + [_layer_of(*w) for w in weights])
    args += [mod, pairsum] + [w[0] for w in weights]
    return pl.pallas_call(
        functools.partial(_even_mix_kernel, seq=seq, tq=tq, n_cache=n_cache, lam_init=lam_init),
        grid=(nseq, nq),
        in_specs=in_specs,
        out_specs=pl.BlockSpec((tq, D_MODEL), blk),
        out_shape=jax.ShapeDtypeStruct((n, D_MODEL), F32),
        scratch_shapes=[pltpu.VMEM((seq + 2 * POOL_PAD, POOL_WIDTH), F32),
                        pltpu.VMEM((8, 128), F32)],
        compiler_params=_params(2),
        name="even_mix",
    )(*args)


def _ffn_kernel(*refs, gla_pre, final):
    refs = list(refs)
    x_ref, mod_ref = refs.pop(0), refs.pop(0)
    if gla_pre:
        og_ref, r_ref, gn_ref, wmix_ref = (refs.pop(0) for _ in range(4))
    g_ref, wi_ref, wo_ref = (refs.pop(0) for _ in range(3))
    fg_ref = refs.pop(0) if final else None
    o_ref, acc_ref = refs
    x = x_ref[...]
    if gla_pre:
        parts = []
        for h in range(GLA_HEADS):
            cols = slice(h * GLA_DV, (h + 1) * GLA_DV)
            oh = og_ref[:, cols]
            oh = oh * lax.rsqrt(jnp.mean(oh * oh, axis=-1, keepdims=True) + EPS)
            parts.append(oh * gn_ref[:, cols])
        z = (jnp.concatenate(parts, axis=1) * _silu(r_ref[...])).astype(BF16)
        x = x + mod_ref[2:3, :] * jnp.dot(z, wmix_ref[...], preferred_element_type=F32)
    h = _norm_mod(x, g_ref[...], mod_ref[3:4, :], mod_ref[4:5, :]).astype(BF16)
    for c in range(FFN_HIDDEN // FFN_CHUNK):
        lo, hi = c * FFN_CHUNK, (c + 1) * FFN_CHUNK
        a = jnp.dot(h, wi_ref[:, lo:hi], preferred_element_type=F32)
        b = jnp.dot(h, wi_ref[:, FFN_HIDDEN + lo:FFN_HIDDEN + hi], preferred_element_type=F32)
        z = (_silu(a) * b).astype(BF16)
        part = jnp.dot(z, wo_ref[lo:hi, :], preferred_element_type=F32)
        if c == 0:
            acc_ref[...] = part
        else:
            acc_ref[...] += part
    y = x + mod_ref[5:6, :] * acc_ref[...]
    if final:
        y = (y * lax.rsqrt(jnp.mean(y * y, axis=-1, keepdims=True) + EPS)) * fg_ref[...]
    o_ref[...] = y


def _ffn(x, mod, layer, group_of_step, g2, wi_bf, wo_bf, gla_pre=None, final_g=None):
    n = x.shape[0]
    tm = FFN_TM
    row = lambda i: (i, 0)
    tile = pl.BlockSpec((tm, D_MODEL), row)
    in_specs = [tile, _mod_spec(layer, group_of_step)]
    args = [x, mod]
    if gla_pre is not None:
        o, r, gn, wmix_bf = gla_pre
        in_specs += [tile, tile, _layer_of(*gn), _layer_of(*wmix_bf)]
        args += [o, r, gn[0], wmix_bf[0]]
    in_specs += [_layer_of(*g2), _layer_of(*wi_bf), _layer_of(*wo_bf)]
    args += [g2[0], wi_bf[0], wo_bf[0]]
    if final_g is not None:
        in_specs.append(_resident((1, D_MODEL)))
        args.append(final_g)
    return pl.pallas_call(
        functools.partial(_ffn_kernel, gla_pre=gla_pre is not None, final=final_g is not None),
        grid=(n // tm,),
        in_specs=in_specs,
        out_specs=tile,
        out_shape=jax.ShapeDtypeStruct((n, D_MODEL), F32),
        scratch_shapes=[pltpu.VMEM((tm, D_MODEL), F32)],
        compiler_params=_params(1),
        name="ffn",
    )(*args)


def _chunk_scan(x, reverse):
    n = x.shape[0]
    pos = lax.broadcasted_iota(jnp.int32, x.shape, 0) % CHUNK
    step = 1
    while step < CHUNK:
        if reverse:
            x = x + jnp.where(pos < CHUNK - step, pltpu.roll(x, n - step, axis=0), 0.0)
        else:
            x = x + jnp.where(pos >= step, pltpu.roll(x, step, axis=0), 0.0)
        step *= 2
    return x


def _odd_in_kernel(x_ref, mod_ref, g_ref, w_ref, wgd_ref, wup_ref, bg_ref,
                   qk_ref, v_ref, r_ref, gate_ref, *, tm, rc):
    shift, scale, g = mod_ref[0:1, :], mod_ref[1:2, :], g_ref[...]
    for c in range(tm // rc):
        rows = slice(c * rc, (c + 1) * rc)
        h = _norm_mod(x_ref[rows, :], g, shift, scale).astype(BF16)
        gd = jnp.dot(h, wgd_ref[...], preferred_element_type=F32).astype(BF16)
        logit = jnp.dot(gd, wup_ref[...], preferred_element_type=F32) + bg_ref[...]
        qk_ref[rows, :] = jnp.dot(h, w_ref[:, 0:1024], preferred_element_type=F32)
        v_ref[rows, :] = jnp.dot(h, w_ref[:, 1024:2048], preferred_element_type=F32).astype(BF16)
        r_ref[rows, :] = jnp.dot(h, w_ref[:, 2048:3072], preferred_element_type=F32)
        log_sig = jnp.minimum(logit, 0.0) - jnp.log(1.0 + jnp.exp(-jnp.abs(logit)))
        gate_ref[rows, :] = log_sig / GATE_TEMP


def _odd_in(x, mod, layer, group_of_step, g1, w_bf, wgd_bf, wup_bf, bg):
    n = x.shape[0]
    tm, rc = 1024, 512
    row = lambda i: (i, 0)
    return pl.pallas_call(
        functools.partial(_odd_in_kernel, tm=tm, rc=rc),
        grid=(n // tm,),
        in_specs=[pl.BlockSpec((tm, D_MODEL), row), _mod_spec(layer, group_of_step),
                  _layer_of(*g1), _layer_of(*w_bf),
                  _resident((D_MODEL, 128)), _resident((128, D_MODEL)), _layer_of(*bg)],
        out_specs=[pl.BlockSpec((tm, D_MODEL), row)] * 4,
        out_shape=[jax.ShapeDtypeStruct((n, D_MODEL), dt) for dt in (F32, BF16, F32, F32)],
        compiler_params=_params(1),
        name="odd_in",
    )(x, mod, g1[0], w_bf[0], wgd_bf, wup_bf, bg[0])


def _gla_kernel(*refs, tb, nsb, has_init, emit_state):
    refs = list(refs)
    ins = [refs.pop(0) for _ in range(6)]
    init_ref = refs.pop(0) if has_init else None
    o_ref = refs.pop(0)
    state_ref = refs.pop(0) if emit_state else None
    st = refs.pop(0)
    n = pl.program_id(1)

    @pl.when(n == 0)
    def _init():
        o_ref[...] = jnp.zeros(o_ref.shape, F32)
        if has_init:
            st[...] = init_ref[...]
        else:
            st[...] = jnp.zeros(st.shape, F32)

    ri = lax.broadcasted_iota(jnp.int32, (GLA_STEP, GLA_STEP), 0)
    ci = lax.broadcasted_iota(jnp.int32, (GLA_STEP, GLA_STEP), 1)
    first = lax.broadcasted_iota(jnp.int32, (GLA_STEP, 512), 0) < CHUNK
    nt = (((1,), (1,)), ((), ()))
    nstep = tb // GLA_STEP

    def body(i, carry):
        for d in range(2):
            qk_ref, v_ref, g_ref = ins[3 * d:3 * d + 3]
            c = i if d == 0 else nstep - 1 - i
            block = n if d == 0 else nsb - 1 - n
            rows = pl.ds(pl.multiple_of(c * GLA_STEP, GLA_STEP), GLA_STEP)
            orows = pl.ds(pl.multiple_of(block * tb + c * GLA_STEP, GLA_STEP), GLA_STEP)
            p = _chunk_scan(g_ref[rows, :], reverse=d == 1)
            if d == 0:
                mask = ci <= ri
                tot_in, tot_out = p[CHUNK - 1:CHUNK, :], p[GLA_STEP - 1:GLA_STEP, :]
                b_mid = jnp.where(first, p - tot_in, p)
            else:
                mask = ci >= ri
                tot_out, tot_in = p[0:1, :], p[CHUNK:CHUNK + 1, :]
                b_mid = jnp.where(first, p, p - tot_in)
            tot = tot_in + tot_out
            q = qk_ref[rows, 0:512] * (GLA_DK ** -0.5)
            k = qk_ref[rows, 512:1024]
            q_mid = q * jnp.exp(b_mid)
            k_mid = k * jnp.exp(-b_mid)
            q_t = q_mid.astype(BF16)
            k_t = k_mid.astype(BF16)
            q_in = (q_mid * jnp.exp(tot_in)).astype(BF16)
            k_end = k_mid * jnp.exp(tot_out)
            decay = jnp.exp(jnp.broadcast_to(tot, (8, 512)).T[:, 0:1])
            for h in range(GLA_HEADS):
                kc = slice(h * GLA_DK, (h + 1) * GLA_DK)
                vc = slice(h * GLA_DV, (h + 1) * GLA_DV)
                vh = v_ref[rows, vc]
                a = lax.dot_general(q_t[:, kc], k_t[:, kc], nt, preferred_element_type=F32)
                a = jnp.where(mask, a, 0.0).astype(BF16)
                lhs = jnp.concatenate([a, k_end[:, kc].T.astype(BF16)], axis=0)
                both = jnp.dot(lhs, vh, preferred_element_type=F32)
                s = st[d, h]
                o = both[:GLA_STEP] + jnp.dot(q_in[:, kc], s.astype(BF16), preferred_element_type=F32)
                o_ref[orows, vc] += o
                st[d, h] = s * decay[kc, :] + both[GLA_STEP:]
        return carry

    lax.fori_loop(0, nstep, body, 0, unroll=2)

    if emit_state:
        @pl.when(n == pl.num_programs(1) - 1)
        def _emit():
            state_ref[...] = st[...]


def _gla(qk, v, gate, init, seq, emit_state):
    n = qk.shape[0]
    nseq = n // seq
    tb = min(seq, 512)
    nsb = seq // tb
    fwd = lambda b, s: (b * nsb + s, 0)
    bwd = lambda b, s: (b * nsb + nsb - 1 - s, 0)
    bwd_gate = lambda b, s: (b * nsb + nsb - 1 - s, 1)
    in_specs = [pl.BlockSpec((tb, 1024), fwd), pl.BlockSpec((tb, 1024), fwd), pl.BlockSpec((tb, 512), fwd),
                pl.BlockSpec((tb, 1024), bwd), pl.BlockSpec((tb, 1024), bwd), pl.BlockSpec((tb, 512), bwd_gate)]
    args = [qk, v, gate, qk, v, gate]
    if init is not None:
        state, il = init
        in_specs.append(pl.BlockSpec((None, None, 2, GLA_HEADS, GLA_DK, GLA_DV),
                                     lambda b, s: (b, il, 0, 0, 0, 0)))
        args.append(state)
    out_specs = [pl.BlockSpec((seq, 1024), lambda b, s: (b, 0))]
    out_shape = [jax.ShapeDtypeStruct((n, 1024), F32)]
    if emit_state:
        out_specs.append(pl.BlockSpec((None, 2, GLA_HEADS, GLA_DK, GLA_DV), lambda b, s: (b, 0, 0, 0, 0)))
        out_shape.append(jax.ShapeDtypeStruct((nseq, 2, GLA_HEADS, GLA_DK, GLA_DV), F32))
    return pl.pallas_call(
        functools.partial(_gla_kernel, tb=tb, nsb=nsb, has_init=init is not None, emit_state=emit_state),
        grid=(nseq, nsb),
        in_specs=in_specs,
        out_specs=out_specs,
        out_shape=out_shape,
        scratch_shapes=[pltpu.VMEM((2, GLA_HEADS, GLA_DK, GLA_DV), F32)],
        compiler_params=_params(2),
        name="gla",
    )(*args)


def _rope_tables(n_lat):
    half = ATT_DH // 2
    inv = 1.0 / (ROPE_BASE ** (jnp.arange(0, half, 2, dtype=F32) / half))
    t = jnp.arange(n_lat)
    rows = (t // GRID_W).astype(F32)
    cols = (t % GRID_W).astype(F32)

    def tab(pos):
        ang = pos[:, None] * inv[None, :]
        ang = jnp.concatenate([ang, ang], axis=-1)
        return jnp.cos(ang), jnp.sin(ang)

    (cr, sr), (cc, sc) = tab(rows), tab(cols)
    sign = jnp.concatenate([-jnp.ones((half // 2,), F32), jnp.ones((half // 2,), F32)])
    cos = jnp.concatenate([cr, cc], axis=-1)
    sin = jnp.concatenate([sr * sign, sc * sign], axis=-1)
    return jnp.tile(cos, (1, 2)), jnp.tile(sin, (1, 2))


def kernel(x_prompt, x_sample, c, cache_attn_k, cache_attn_v, state_gla, c_ctx, norm1_g, norm2_g, w_mod, b_mod, w_in_even, lam_params, subln_g, w_pool, pool_scale, w_out_even, w_in_odd, w_gate_up, b_gate, gla_norm_g, w_out_odd, w_ffn_in, w_ffn_out, final_g):
    bp, tp, _ = x_prompt.shape
    bs, ts, _ = x_sample.shape
    past = cache_attn_k.shape[2]
    n_even = cache_attn_k.shape[1]
    assert bs + 1 <= MOD_ROWS

    xp = x_prompt.reshape(bp * tp, D_MODEL)
    xs = x_sample.reshape(bs * ts, D_MODEL)
    ck = cache_attn_k.reshape(bs, n_even, past, ATT_WIDTH)
    cv = cache_attn_v.reshape(bs, n_even, past, ATT_WIDTH)

    cvec = jnp.concatenate([c_ctx[None, :], c, jnp.zeros((MOD_ROWS - 1 - bs, D_MODEL), F32)], axis=0)
    mod = _adaln(cvec, w_mod, b_mod)
    ctx_group = lambda i: 0

    def lat_group(tm):
        return lambda i: 1 + (i * tm) // ts

    rope = _rope_tables(ts)

    row3 = lambda a: a.reshape(a.shape[0], 1, a.shape[1])
    n1, n2 = row3(norm1_g), row3(norm2_g)
    w_in_even_bf, w_out_even_bf, w_pool_bf = (w.astype(BF16) for w in (w_in_even, w_out_even, w_pool))
    w_in_odd_bf, w_out_odd_bf = w_in_odd.astype(BF16), w_out_odd.astype(BF16)
    w_ffn_in_bf, w_ffn_out_bf = w_ffn_in.astype(BF16), w_ffn_out.astype(BF16)
    subg3, pscale3, gn3 = row3(subln_g), row3(pool_scale), row3(gla_norm_g)
    bg3 = b_gate.reshape(b_gate.shape[0], 1, D_MODEL)

    fg = final_g.reshape(1, D_MODEL)
    ks_new, vs_new, ss_new = [], [], []
    for l in range(DEPTH):
        i = l // 2
        g1, g2 = (n1, l), (n2, l)
        last = fg if l == DEPTH - 1 else None
        pre_p = pre_s = None
        if l % 2 == 0:
            lam_init = 0.8 - 0.6 * math.exp(-0.3 * l)
            w_in, w_out, wpool = (w_in_even_bf, i), (w_out_even_bf, i), (w_pool_bf, i)
            subg, pscale, lamp = (subg3, i), (pscale3, i), (lam_params, i)
            q, k, v, u = _even_in(xp, mod, l, ctx_group, g1, w_in, None, tp, F32)
            ks_new.append(k.reshape(bp, tp, ATT_WIDTH))
            vs_new.append(v.reshape(bp, tp, ATT_WIDTH))
            xp = _even_mix(xp, q, k, v, u, None, mod, l, ctx_group, lamp, subg, wpool, pscale,
                           w_out, tp, lam_init)
            q, k, v, u = _even_in(xs, mod, l, lat_group(1024), g1, w_in, rope, ts, BF16)
            xs = _even_mix(xs, q, k, v, u, (ck, cv, i), mod, l, lambda b: 1 + b, lamp, subg, wpool,
                           pscale, w_out, ts, lam_init)
        else:
            w_main, w_out, gn, bg = (w_in_odd_bf, i), (w_out_odd_bf, i), (gn3, i), (bg3, i)
            wgd = jnp.pad(w_in_odd[i][:, 3072:], ((0, 0), (0, 128 - 2 * GATE_RANK))).astype(BF16)
            wup = jnp.zeros((128, D_MODEL), F32)
            wup = wup.at[0:GATE_RANK, 0:512].set(w_gate_up[i, 0])
            wup = wup.at[GATE_RANK:2 * GATE_RANK, 512:1024].set(w_gate_up[i, 1]).astype(BF16)
            qk, v, r, gate = _odd_in(xp, mod, l, ctx_group, g1, w_main, wgd, wup, bg)
            o, s_ctx = _gla(qk, v, gate, None, tp, True)
            ss_new.append(s_ctx)
            pre_p = (o, r, gn, w_out)
            qk, v, r, gate = _odd_in(xs, mod, l, lat_group(1024), g1, w_main, wgd, wup, bg)
            (o,) = _gla(qk, v, gate, (state_gla, i), ts, False)
            pre_s = (o, r, gn, w_out)
        wi, wo = (w_ffn_in_bf, l), (w_ffn_out_bf, l)
        xp = _ffn(xp, mod, l, ctx_group, g2, wi, wo, pre_p, last)
        xs = _ffn(xs, mod, l, lat_group(FFN_TM), g2, wi, wo, pre_s, last)

    y_prompt = xp.reshape(bp, tp, D_MODEL)
    y_sample = xs.reshape(bs, ts, D_MODEL)
    new_attn_k = jnp.stack(ks_new, axis=1).reshape(bp, n_even, tp, ATT_HEADS, 2, ATT_DH)
    new_attn_v = jnp.stack(vs_new, axis=1).reshape(bp, n_even, tp, ATT_HEADS, ATT_DV)
    new_gla_state = jnp.stack(ss_new, axis=1)
    return (y_prompt, y_sample, new_attn_k, new_attn_v, new_gla_state)
```

```python
import functools
import math

import jax
import jax.numpy as jnp
from jax import lax
from jax.experimental import pallas as pl
from jax.experimental.pallas import tpu as pltpu

F32 = jnp.float32
BF16 = jnp.bfloat16

D_MODEL = 1024
DEPTH = 4
GRID_W = 64
ATT_WIDTH = 512
POOL_WIDTH = 512
ATT_HEADS = 4
ATT_DH = 64
ATT_DV = 128
ROPE_BASE = 10000.0
POOL_WINDOWS = (2, 4, 8, 16)
POOL_GROUP = 128
POOL_PAD = 8
GLA_HEADS = 4
GLA_DK = 128
GLA_DV = 256
GATE_RANK = 16
GATE_TEMP = 16.0
CHUNK = 64
GLA_STEP = 2 * CHUNK
FFN_HIDDEN = 2816
FFN_CHUNK = 256
FFN_TM = 512
EPS = 1e-6
Q_SCALE = ATT_DH ** -0.5 * math.log2(math.e)
SAFE_SHIFT = 50.0
NORM_SLACK = 1.05
NORM_LANES = 16
MOD_ROWS = 16

V7X_VMEM_LIMIT = 56 * 1024 * 1024


def _params(n_axes, vmem_bytes=V7X_VMEM_LIMIT):
    return pltpu.CompilerParams(dimension_semantics=("arbitrary",) * n_axes,
                                vmem_limit_bytes=vmem_bytes)


def _resident(shape):
    nd = len(shape)
    return pl.BlockSpec(shape, lambda *_: (0,) * nd, pipeline_mode=pl.Buffered(1))


def _layer_of(stack, idx):
    nd = stack.ndim - 1
    return pl.BlockSpec((None,) + stack.shape[1:], lambda *_: (idx,) + (0,) * nd,
                        pipeline_mode=pl.Buffered(1))


def _norm_mod(x, g, shift, scale):
    y = x * lax.rsqrt(jnp.mean(x * x, axis=-1, keepdims=True) + EPS)
    return (y * g) * (1.0 + scale) + shift


def _silu(x):
    return x * jax.nn.sigmoid(x)


def _adaln_kernel(cv_ref, w_ref, b_ref, o_ref):
    s = _silu(cv_ref[...]).astype(BF16)
    o_ref[...] = jnp.dot(s, w_ref[...].astype(BF16), preferred_element_type=F32) + b_ref[...]


def _adaln(cvec, w_mod, b_mod):
    tn = 1536
    out = pl.pallas_call(
        _adaln_kernel,
        grid=(DEPTH, 6 * D_MODEL // tn),
        in_specs=[pl.BlockSpec((MOD_ROWS, D_MODEL), lambda l, j: (0, 0)),
                  pl.BlockSpec((None, D_MODEL, tn), lambda l, j: (l, 0, j)),
                  pl.BlockSpec((None, 1, tn), lambda l, j: (l, 0, j))],
        out_specs=pl.BlockSpec((None, MOD_ROWS, tn), lambda l, j: (l, 0, j)),
        out_shape=jax.ShapeDtypeStruct((DEPTH, MOD_ROWS, 6 * D_MODEL), F32),
        compiler_params=_params(2),
        name="adaln",
    )(cvec, w_mod, b_mod.reshape(DEPTH, 1, 6 * D_MODEL))
    return out.reshape(DEPTH, MOD_ROWS, 6, D_MODEL)


def _mod_spec(layer, group_of_step):
    return pl.BlockSpec((None, None, 6, D_MODEL),
                        lambda *ids: (layer, group_of_step(*ids), 0, 0))


def _even_in_kernel(*refs, tm, rc, rope):
    if rope:
        x_ref, mod_ref, g_ref, w_ref, cos_ref, sin_ref, q_ref, k_ref, v_ref, u_ref = refs
    else:
        x_ref, mod_ref, g_ref, w_ref, q_ref, k_ref, v_ref, u_ref = refs
    shift, scale, g = mod_ref[0:1, :], mod_ref[1:2, :], g_ref[...]
    lo = (lax.broadcasted_iota(jnp.int32, (rc, 128), 1) % 32) < 16

    def rotary(y, cos, sin):
        parts = []
        for h in range(ATT_HEADS):
            yh = y[:, h * 128:(h + 1) * 128]
            fwd = pltpu.roll(yh, 128 - 16, axis=1)
            bwd = pltpu.roll(yh, 16, axis=1)
            parts.append(yh * cos + jnp.where(lo, fwd, bwd) * sin)
        return jnp.concatenate(parts, axis=1)

    for r in range(tm // rc):
        rows = slice(r * rc, (r + 1) * rc)
        h = _norm_mod(x_ref[rows, :], g, shift, scale).astype(BF16)
        q = jnp.dot(h, w_ref[:, 0:512], preferred_element_type=F32)
        k = jnp.dot(h, w_ref[:, 512:1024], preferred_element_type=F32)
        if rope:
            cos, sin = cos_ref[rows, :], sin_ref[rows, :]
            q, k = rotary(q, cos, sin), rotary(k, cos, sin)
        q_ref[rows, :] = (q * Q_SCALE).astype(q_ref.dtype)
        k_ref[rows, :] = k.astype(k_ref.dtype)
        v_ref[rows, :] = jnp.dot(h, w_ref[:, 1024:1536], preferred_element_type=F32).astype(v_ref.dtype)
        u_ref[rows, :] = jnp.dot(h, w_ref[:, 1536:2048], preferred_element_type=F32)


def _even_in(x, mod, layer, group_of_step, g1, w_bf, rope_tabs, seq_len, kv_dtype):
    n = x.shape[0]
    tm, rc = 1024, 512
    row = lambda i: (i, 0)
    in_specs = [pl.BlockSpec((tm, D_MODEL), row), _mod_spec(layer, group_of_step),
                _layer_of(*g1), _layer_of(*w_bf)]
    args = [x, mod, g1[0], w_bf[0]]
    if rope_tabs is not None:
        per_seq = seq_len // tm
        pos = lambda i: (i % per_seq, 0)
        in_specs += [pl.BlockSpec((tm, 128), pos), pl.BlockSpec((tm, 128), pos)]
        args += list(rope_tabs)
    out_dt = (BF16, kv_dtype, kv_dtype, F32)
    return pl.pallas_call(
        functools.partial(_even_in_kernel, tm=tm, rc=rc, rope=rope_tabs is not None),
        grid=(n // tm,),
        in_specs=in_specs,
        out_specs=[pl.BlockSpec((tm, 512), row)] * 4,
        out_shape=[jax.ShapeDtypeStruct((n, 512), dt) for dt in out_dt],
        compiler_params=_params(1),
        name="even_in",
    )(*args)


def _even_mix_kernel(*refs, seq, tq, n_cache, lam_init):
    if n_cache:
        (x_ref, q_ref, qseq_ref, k_ref, v_ref, u_ref, ck_ref, cv_ref, mod_ref, pairsum_ref, lamp_ref,
         subg_ref, wpool_ref, pscale_ref, wout_ref, o_ref, upad, bound) = refs
    else:
        (x_ref, q_ref, qseq_ref, k_ref, v_ref, u_ref, mod_ref, pairsum_ref, lamp_ref,
         subg_ref, wpool_ref, pscale_ref, wout_ref, o_ref, upad, bound) = refs
    qi = pl.program_id(1)
    lane = lax.broadcasted_iota(jnp.int32, (1, 128), 1)

    def max_pair_norm2(rows):
        xf = rows.astype(F32)
        n2 = jnp.dot((xf * xf).astype(BF16), pairsum_ref[...], preferred_element_type=F32)
        return jnp.max(n2, axis=0, keepdims=True)

    @pl.when(qi == 0)
    def _per_sequence():
        k2 = max_pair_norm2(k_ref[...])
        if n_cache:
            k2 = jnp.maximum(k2, max_pair_norm2(ck_ref[...]))
        bound[0:1, :] = max_pair_norm2(qseq_ref[...]) * k2 * NORM_SLACK
        zeros = jnp.zeros((POOL_PAD, POOL_WIDTH), F32)
        upad[0:POOL_PAD, :] = zeros
        upad[POOL_PAD + seq:POOL_PAD + seq + POOL_PAD, :] = zeros
        upad[POOL_PAD:POOL_PAD + seq, :] = u_ref[...]

    def pooled_rows():
        base = pl.multiple_of(qi * tq, tq)
        n_blk = tq + 2 * POOL_PAD
        t = lax.broadcasted_iota(jnp.int32, (tq, 1), 0) + qi * tq
        ys = []
        for g, w in enumerate(POOL_WINDOWS):
            cols = slice(g * POOL_GROUP, (g + 1) * POOL_GROUP)
            xg = upad[pl.ds(base, n_blk), cols]
            p = xg + pltpu.roll(xg, 1, axis=0)
            half = 1
            while 2 * half < w:
                p = pltpu.roll(p, half, axis=0) + pltpu.roll(p, n_blk - half, axis=0)
                half *= 2
            cnt = (jnp.minimum(t - w // 2 + w, seq) - jnp.maximum(t - w // 2, 0)).astype(F32)
            ys.append((p[POOL_PAD:POOL_PAD + tq] / cnt - xg[POOL_PAD:POOL_PAD + tq]).astype(BF16))
        return ys

    lp = lamp_ref[...]
    lam = (jnp.exp(jnp.sum(lp[0:1, :] * lp[1:2, :], axis=-1, keepdims=True))
           - jnp.exp(jnp.sum(lp[2:3, :] * lp[3:4, :], axis=-1, keepdims=True)) + lam_init)

    m1 = jnp.where(lane < ATT_DH, 1.0, 0.0).astype(BF16)
    m2 = jnp.where(lane >= ATT_DH, 1.0, 0.0).astype(BF16)
    nt = (((1,), (1,)), ((), ()))

    bound2 = bound[0:1, :]
    worst2 = jnp.max(bound2)

    def scores(h):
        cols = slice(h * 128, (h + 1) * 128)
        qh = q_ref[:, cols]
        q2 = jnp.concatenate([qh * m1, qh * m2], axis=0)
        s = lax.dot_general(q2, k_ref[:, cols].astype(BF16), nt, preferred_element_type=F32)
        sc = None
        if n_cache:
            sc = lax.dot_general(q2, ck_ref[:, cols].astype(BF16), nt, preferred_element_type=F32)
        return s, sc

    def mix(exact_max):
        heads = []
        pending = scores(0)
        ys = pooled_rows()
        for h in range(ATT_HEADS):
            cols = slice(h * 128, (h + 1) * 128)
            s, sc = pending
            if h + 1 < ATT_HEADS:
                pending = scores(h + 1)
            vh = v_ref[:, cols].astype(BF16)
            if n_cache:
                cvh = cv_ref[:, cols].astype(BF16)
            if exact_max:
                mx = jnp.max(s, axis=-1, keepdims=True)
                if n_cache:
                    mx = jnp.maximum(mx, jnp.max(sc, axis=-1, keepdims=True))
            else:
                l1, l2 = NORM_LANES * 2 * h, NORM_LANES * (2 * h + 1)
                mx = jnp.sqrt(jnp.concatenate([jnp.broadcast_to(bound2[:, l1:l1 + 1], (tq, 1)),
                                               jnp.broadcast_to(bound2[:, l2:l2 + 1], (tq, 1))], axis=0))
            e = jnp.exp2(s - mx)
            den = jnp.sum(e, axis=-1, keepdims=True)
            if n_cache:
                ec = jnp.exp2(sc - mx)
                den = den + jnp.sum(ec, axis=-1, keepdims=True)
            inv1 = 1.0 / den[:tq]
            ratio = lam * den[:tq] / den[tq:]
            a = (e[:tq] - ratio * e[tq:]).astype(BF16)
            o = jnp.dot(a, vh, preferred_element_type=F32)
            if n_cache:
                ac = (ec[:tq] - ratio * ec[tq:]).astype(BF16)
                o = o + jnp.dot(ac, cvh, preferred_element_type=F32)
            o = o * inv1
            o = o * lax.rsqrt(jnp.mean(o * o, axis=-1, keepdims=True) + EPS)
            heads.append((o * subg_ref[:, cols]) * (1.0 - lam_init))
        o_att = jnp.concatenate(heads, axis=1).astype(BF16)

        pooled = [jnp.dot(y, wpool_ref[g], preferred_element_type=F32) for g, y in enumerate(ys)]
        p = (jnp.concatenate(pooled, axis=1) * pscale_ref[...]).astype(BF16)

        out = (jnp.dot(o_att, wout_ref[0:ATT_WIDTH, :], preferred_element_type=F32)
               + jnp.dot(p, wout_ref[ATT_WIDTH:, :], preferred_element_type=F32))
        o_ref[...] = x_ref[...] + mod_ref[2:3, :] * out

    lax.cond(worst2 <= SAFE_SHIFT * SAFE_SHIFT, lambda: mix(False), lambda: mix(True))


def _even_mix(x, q, k, v, u, cache, mod, layer, group_of_seq, lamp, subg, wpool_bf, pscale,
              wout_bf, seq, lam_init):
    n = x.shape[0]
    nseq = n // seq
    tq = 256
    nq = seq // tq
    blk = lambda b, qi: (b * nq + qi, 0)
    whole = lambda b, qi: (b, 0)
    in_specs = [pl.BlockSpec((tq, D_MODEL), blk), pl.BlockSpec((tq, 512), blk),
                pl.BlockSpec((seq, 512), whole), pl.BlockSpec((seq, 512), whole),
                pl.BlockSpec((seq, 512), whole), pl.BlockSpec((seq, 512), whole)]
    args = [x, q, q, k, v, u]
    n_cache = 0
    if cache is not None:
        ck, cv, il = cache
        n_cache = ck.shape[2]
        cspec = pl.BlockSpec((None, None, n_cache, 512), lambda b, qi: (b, il, 0, 0))
        in_specs += [cspec, cspec]
        args += [ck, cv]
    pairsum = (jnp.arange(ATT_WIDTH)[:, None] // ATT_DH == jnp.arange(128)[None, :] // NORM_LANES).astype(BF16)
    weights = [lamp, subg, wpool_bf, pscale, wout_bf]
    in_specs += ([_mod_spec(layer, lambda b, qi: group_of_seq(b)), _resident((ATT_WIDTH, 128))]
                 + [_layer_of(*w) for w in weights])
    args += [mod, pairsum] + [w[0] for w in weights]
    return pl.pallas_call(
        functools.partial(_even_mix_kernel, seq=seq, tq=tq, n_cache=n_cache, lam_init=lam_init),
        grid=(nseq, nq),
        in_specs=in_specs,
        out_specs=pl.BlockSpec((tq, D_MODEL), blk),
        out_shape=jax.ShapeDtypeStruct((n, D_MODEL), F32),
        scratch_shapes=[pltpu.VMEM((seq + 2 * POOL_PAD, POOL_WIDTH), F32),
                        pltpu.VMEM((8, 128), F32)],
        compiler_params=_params(2),
        name="even_mix",
    )(*args)


def _ffn_kernel(*refs, gla_pre, final):
    refs = list(refs)
    x_ref, mod_ref = refs.pop(0), refs.pop(0)
    if gla_pre:
        og_ref, r_ref, gn_ref, wmix_ref = (refs.pop(0) for _ in range(4))
    g_ref, wi_ref, wo_ref = (refs.pop(0) for _ in range(3))
    fg_ref = refs.pop(0) if final else None
    o_ref, acc_ref = refs
    x = x_ref[...]
    if gla_pre:
        parts = []
        for h in range(GLA_HEADS):
            cols = slice(h * GLA_DV, (h + 1) * GLA_DV)
            oh = og_ref[:, cols]
            oh = oh * lax.rsqrt(jnp.mean(oh * oh, axis=-1, keepdims=True) + EPS)
            parts.append(oh * gn_ref[:, cols])
        z = (jnp.concatenate(parts, axis=1) * _silu(r_ref[...].astype(F32))).astype(BF16)
        x = x + mod_ref[2:3, :] * jnp.dot(z, wmix_ref[...], preferred_element_type=F32)
    h = _norm_mod(x, g_ref[...], mod_ref[3:4, :], mod_ref[4:5, :]).astype(BF16)
    for c in range(FFN_HIDDEN // FFN_CHUNK):
        lo, hi = c * FFN_CHUNK, (c + 1) * FFN_CHUNK
        a = jnp.dot(h, wi_ref[:, lo:hi], preferred_element_type=F32)
        b = jnp.dot(h, wi_ref[:, FFN_HIDDEN + lo:FFN_HIDDEN + hi], preferred_element_type=F32)
        z = (_silu(a) * b).astype(BF16)
        part = jnp.dot(z, wo_ref[lo:hi, :], preferred_element_type=F32)
        if c == 0:
            acc_ref[...] = part
        else:
            acc_ref[...] += part
    y = x + mod_ref[5:6, :] * acc_ref[...]
    if final:
        y = (y * lax.rsqrt(jnp.mean(y * y, axis=-1, keepdims=True) + EPS)) * fg_ref[...]
    o_ref[...] = y


def _ffn(x, mod, layer, group_of_step, g2, wi_bf, wo_bf, gla_pre=None, final_g=None):
    n = x.shape[0]
    tm = FFN_TM
    row = lambda i: (i, 0)
    tile = pl.BlockSpec((tm, D_MODEL), row)
    in_specs = [tile, _mod_spec(layer, group_of_step)]
    args = [x, mod]
    if gla_pre is not None:
        o, r, gn, wmix_bf = gla_pre
        in_specs += [tile, tile, _layer_of(*gn), _layer_of(*wmix_bf)]
        args += [o, r, gn[0], wmix_bf[0]]
    in_specs += [_layer_of(*g2), _layer_of(*wi_bf), _layer_of(*wo_bf)]
    args += [g2[0], wi_bf[0], wo_bf[0]]
    if final_g is not None:
        in_specs.append(_resident((1, D_MODEL)))
        args.append(final_g)
    return pl.pallas_call(
        functools.partial(_ffn_kernel, gla_pre=gla_pre is not None, final=final_g is not None),
        grid=(n // tm,),
        in_specs=in_specs,
        out_specs=tile,
        out_shape=jax.ShapeDtypeStruct((n, D_MODEL), F32),
        scratch_shapes=[pltpu.VMEM((tm, D_MODEL), F32)],
        compiler_params=_params(1),
        name="ffn",
    )(*args)


def _chunk_scan(x, reverse):
    n = x.shape[0]
    pos = lax.broadcasted_iota(jnp.int32, x.shape, 0) % CHUNK
    step = 1
    while step < CHUNK:
        if reverse:
            x = x + jnp.where(pos < CHUNK - step, pltpu.roll(x, n - step, axis=0), 0.0)
        else:
            x = x + jnp.where(pos >= step, pltpu.roll(x, step, axis=0), 0.0)
        step *= 2
    return x


def _odd_in_kernel(x_ref, mod_ref, g_ref, w_ref, wgd_ref, wup_ref, bg_ref,
                   qk_ref, v_ref, r_ref, gate_ref, *, tm, rc):
    shift, scale, g = mod_ref[0:1, :], mod_ref[1:2, :], g_ref[...]
    for c in range(tm // rc):
        rows = slice(c * rc, (c + 1) * rc)
        h = _norm_mod(x_ref[rows, :], g, shift, scale).astype(BF16)
        gd = jnp.dot(h, wgd_ref[...], preferred_element_type=F32).astype(BF16)
        logit = jnp.dot(gd, wup_ref[...], preferred_element_type=F32) + bg_ref[...]
        qk_ref[rows, :] = jnp.dot(h, w_ref[:, 0:1024], preferred_element_type=F32).astype(BF16)
        v_ref[rows, :] = jnp.dot(h, w_ref[:, 1024:2048], preferred_element_type=F32).astype(BF16)
        r_ref[rows, :] = jnp.dot(h, w_ref[:, 2048:3072], preferred_element_type=F32).astype(BF16)
        log_sig = jnp.minimum(logit, 0.0) - jnp.log(1.0 + jnp.exp(-jnp.abs(logit)))
        gate_ref[rows, :] = log_sig / GATE_TEMP


def _odd_in(x, mod, layer, group_of_step, g1, w_bf, wgd_bf, wup_bf, bg):
    n = x.shape[0]
    tm, rc = 1024, 512
    row = lambda i: (i, 0)
    return pl.pallas_call(
        functools.partial(_odd_in_kernel, tm=tm, rc=rc),
        grid=(n // tm,),
        in_specs=[pl.BlockSpec((tm, D_MODEL), row), _mod_spec(layer, group_of_step),
                  _layer_of(*g1), _layer_of(*w_bf),
                  _resident((D_MODEL, 128)), _resident((128, D_MODEL)), _layer_of(*bg)],
        out_specs=[pl.BlockSpec((tm, D_MODEL), row)] * 4,
        out_shape=[jax.ShapeDtypeStruct((n, D_MODEL), dt) for dt in (BF16, BF16, BF16, F32)],
        compiler_params=_params(1),
        name="odd_in",
    )(x, mod, g1[0], w_bf[0], wgd_bf, wup_bf, bg[0])


def _gla_kernel(*refs, tb, nsb, has_init, emit_state):
    refs = list(refs)
    ins = [refs.pop(0) for _ in range(6)]
    init_ref = refs.pop(0) if has_init else None
    o_ref = refs.pop(0)
    state_ref = refs.pop(0) if emit_state else None
    st = refs.pop(0)
    n = pl.program_id(1)

    @pl.when(n == 0)
    def _init():
        o_ref[...] = jnp.zeros(o_ref.shape, F32)
        if has_init:
            st[...] = init_ref[...]
        else:
            st[...] = jnp.zeros(st.shape, F32)

    ri = lax.broadcasted_iota(jnp.int32, (GLA_STEP, GLA_STEP), 0)
    ci = lax.broadcasted_iota(jnp.int32, (GLA_STEP, GLA_STEP), 1)
    first = lax.broadcasted_iota(jnp.int32, (GLA_STEP, 512), 0) < CHUNK
    nt = (((1,), (1,)), ((), ()))
    nstep = tb // GLA_STEP

    def body(i, carry):
        for d in range(2):
            qk_ref, v_ref, g_ref = ins[3 * d:3 * d + 3]
            c = i if d == 0 else nstep - 1 - i
            block = n if d == 0 else nsb - 1 - n
            rows = pl.ds(pl.multiple_of(c * GLA_STEP, GLA_STEP), GLA_STEP)
            orows = pl.ds(pl.multiple_of(block * tb + c * GLA_STEP, GLA_STEP), GLA_STEP)
            p = _chunk_scan(g_ref[rows, :], reverse=d == 1)
            if d == 0:
                mask = ci <= ri
                tot_in, tot_out = p[CHUNK - 1:CHUNK, :], p[GLA_STEP - 1:GLA_STEP, :]
                b_mid = jnp.where(first, p - tot_in, p)
            else:
                mask = ci >= ri
                tot_out, tot_in = p[0:1, :], p[CHUNK:CHUNK + 1, :]
                b_mid = jnp.where(first, p, p - tot_in)
            tot = tot_in + tot_out
            q = qk_ref[rows, 0:512].astype(F32) * (GLA_DK ** -0.5)
            k = qk_ref[rows, 512:1024].astype(F32)
            q_mid = q * jnp.exp(b_mid)
            k_mid = k * jnp.exp(-b_mid)
            q_t = q_mid.astype(BF16)
            k_t = k_mid.astype(BF16)
            q_in = (q_mid * jnp.exp(tot_in)).astype(BF16)
            k_end = k_mid * jnp.exp(tot_out)
            decay = jnp.exp(jnp.broadcast_to(tot, (8, 512)).T[:, 0:1])
            for h in range(GLA_HEADS):
                kc = slice(h * GLA_DK, (h + 1) * GLA_DK)
                vc = slice(h * GLA_DV, (h + 1) * GLA_DV)
                vh = v_ref[rows, vc]
                a = lax.dot_general(q_t[:, kc], k_t[:, kc], nt, preferred_element_type=F32)
                a = jnp.where(mask, a, 0.0).astype(BF16)
                lhs = jnp.concatenate([a, k_end[:, kc].T.astype(BF16)], axis=0)
                both = jnp.dot(lhs, vh, preferred_element_type=F32)
                s = st[d, h]
                o = both[:GLA_STEP] + jnp.dot(q_in[:, kc], s.astype(BF16), preferred_element_type=F32)
                o_ref[orows, vc] += o
                st[d, h] = s * decay[kc, :] + both[GLA_STEP:]
        return carry

    lax.fori_loop(0, nstep, body, 0, unroll=2)

    if emit_state:
        @pl.when(n == pl.num_programs(1) - 1)
        def _emit():
            state_ref[...] = st[...]


def _gla(qk, v, gate, init, seq, emit_state):
    n = qk.shape[0]
    nseq = n // seq
    tb = min(seq, 512)
    nsb = seq // tb
    fwd = lambda b, s: (b * nsb + s, 0)
    bwd = lambda b, s: (b * nsb + nsb - 1 - s, 0)
    bwd_gate = lambda b, s: (b * nsb + nsb - 1 - s, 1)
    in_specs = [pl.BlockSpec((tb, 1024), fwd), pl.BlockSpec((tb, 1024), fwd), pl.BlockSpec((tb, 512), fwd),
                pl.BlockSpec((tb, 1024), bwd), pl.BlockSpec((tb, 1024), bwd), pl.BlockSpec((tb, 512), bwd_gate)]
    args = [qk, v, gate, qk, v, gate]
    if init is not None:
        state, il = init
        in_specs.append(pl.BlockSpec((None, None, 2, GLA_HEADS, GLA_DK, GLA_DV),
                                     lambda b, s: (b, il, 0, 0, 0, 0)))
        args.append(state)
    out_specs = [pl.BlockSpec((seq, 1024), lambda b, s: (b, 0))]
    out_shape = [jax.ShapeDtypeStruct((n, 1024), F32)]
    if emit_state:
        out_specs.append(pl.BlockSpec((None, 2, GLA_HEADS, GLA_DK, GLA_DV), lambda b, s: (b, 0, 0, 0, 0)))
        out_shape.append(jax.ShapeDtypeStruct((nseq, 2, GLA_HEADS, GLA_DK, GLA_DV), F32))
    return pl.pallas_call(
        functools.partial(_gla_kernel, tb=tb, nsb=nsb, has_init=init is not None, emit_state=emit_state),
        grid=(nseq, nsb),
        in_specs=in_specs,
        out_specs=out_specs,
        out_shape=out_shape,
        scratch_shapes=[pltpu.VMEM((2, GLA_HEADS, GLA_DK, GLA_DV), F32)],
        compiler_params=_params(2),
        name="gla",
    )(*args)


def _rope_tables(n_lat):
    half = ATT_DH // 2
    inv = 1.0 / (ROPE_BASE ** (jnp.arange(0, half, 2, dtype=F32) / half))
    t = jnp.arange(n_lat)
    rows = (t // GRID_W).astype(F32)
    cols = (t % GRID_W).astype(F32)

    def tab(pos):
        ang = pos[:, None] * inv[None, :]
        ang = jnp.concatenate([ang, ang], axis=-1)
        return jnp.cos(ang), jnp.sin(ang)

    (cr, sr), (cc, sc) = tab(rows), tab(cols)
    sign = jnp.concatenate([-jnp.ones((half // 2,), F32), jnp.ones((half // 2,), F32)])
    cos = jnp.concatenate([cr, cc], axis=-1)
    sin = jnp.concatenate([sr * sign, sc * sign], axis=-1)
    return jnp.tile(cos, (1, 2)), jnp.tile(sin, (1, 2))


def kernel(x_prompt, x_sample, c, cache_attn_k, cache_attn_v, state_gla, c_ctx, norm1_g, norm2_g, w_mod, b_mod, w_in_even, lam_params, subln_g, w_pool, pool_scale, w_out_even, w_in_odd, w_gate_up, b_gate, gla_norm_g, w_out_odd, w_ffn_in, w_ffn_out, final_g):
    bp, tp, _ = x_prompt.shape
    bs, ts, _ = x_sample.shape
    past = cache_attn_k.shape[2]
    n_even = cache_attn_k.shape[1]
    assert bs + 1 <= MOD_ROWS

    xp = x_prompt.reshape(bp * tp, D_MODEL)
    xs = x_sample.reshape(bs * ts, D_MODEL)
    ck = cache_attn_k.reshape(bs, n_even, past, ATT_WIDTH)
    cv = cache_attn_v.reshape(bs, n_even, past, ATT_WIDTH)

    cvec = jnp.concatenate([c_ctx[None, :], c, jnp.zeros((MOD_ROWS - 1 - bs, D_MODEL), F32)], axis=0)
    mod = _adaln(cvec, w_mod, b_mod)
    ctx_group = lambda i: 0

    def lat_group(tm):
        return lambda i: 1 + (i * tm) // ts

    rope = _rope_tables(ts)

    row3 = lambda a: a.reshape(a.shape[0], 1, a.shape[1])
    n1, n2 = row3(norm1_g), row3(norm2_g)
    w_in_even_bf, w_out_even_bf, w_pool_bf = (w.astype(BF16) for w in (w_in_even, w_out_even, w_pool))
    w_in_odd_bf, w_out_odd_bf = w_in_odd.astype(BF16), w_out_odd.astype(BF16)
    w_ffn_in_bf, w_ffn_out_bf = w_ffn_in.astype(BF16), w_ffn_out.astype(BF16)
    subg3, pscale3, gn3 = row3(subln_g), row3(pool_scale), row3(gla_norm_g)
    bg3 = b_gate.reshape(b_gate.shape[0], 1, D_MODEL)

    fg = final_g.reshape(1, D_MODEL)
    ks_new, vs_new, ss_new = [], [], []
    for l in range(DEPTH):
        i = l // 2
        g1, g2 = (n1, l), (n2, l)
        last = fg if l == DEPTH - 1 else None
        pre_p = pre_s = None
        if l % 2 == 0:
            lam_init = 0.8 - 0.6 * math.exp(-0.3 * l)
            w_in, w_out, wpool = (w_in_even_bf, i), (w_out_even_bf, i), (w_pool_bf, i)
            subg, pscale, lamp = (subg3, i), (pscale3, i), (lam_params, i)
            q, k, v, u = _even_in(xp, mod, l, ctx_group, g1, w_in, None, tp, F32)
            ks_new.append(k.reshape(bp, tp, ATT_WIDTH))
            vs_new.append(v.reshape(bp, tp, ATT_WIDTH))
            xp = _even_mix(xp, q, k, v, u, None, mod, l, ctx_group, lamp, subg, wpool, pscale,
                           w_out, tp, lam_init)
            q, k, v, u = _even_in(xs, mod, l, lat_group(1024), g1, w_in, rope, ts, BF16)
            xs = _even_mix(xs, q, k, v, u, (ck, cv, i), mod, l, lambda b: 1 + b, lamp, subg, wpool,
                           pscale, w_out, ts, lam_init)
        else:
            w_main, w_out, gn, bg = (w_in_odd_bf, i), (w_out_odd_bf, i), (gn3, i), (bg3, i)
            wgd = jnp.pad(w_in_odd[i][:, 3072:], ((0, 0), (0, 128 - 2 * GATE_RANK))).astype(BF16)
            wup = jnp.zeros((128, D_MODEL), F32)
            wup = wup.at[0:GATE_RANK, 0:512].set(w_gate_up[i, 0])
            wup = wup.at[GATE_RANK:2 * GATE_RANK, 512:1024].set(w_gate_up[i, 1]).astype(BF16)
            qk, v, r, gate = _odd_in(xp, mod, l, ctx_group, g1, w_main, wgd, wup, bg)
            o, s_ctx = _gla(qk, v, gate, None, tp, True)
            ss_new.append(s_ctx)
            pre_p = (o, r, gn, w_out)
            qk, v, r, gate = _odd_in(xs, mod, l, lat_group(1024), g1, w_main, wgd, wup, bg)
            (o,) = _gla(qk, v, gate, (state_gla, i), ts, False)
            pre_s = (o, r, gn, w_out)
        wi, wo = (w_ffn_in_bf, l), (w_ffn_out_bf, l)
        xp = _ffn(xp, mod, l, ctx_group, g2, wi, wo, pre_p, last)
        xs = _ffn(xs, mod, l, lat_group(FFN_TM), g2, wi, wo, pre_s, last)

    y_prompt = xp.reshape(bp, tp, D_MODEL)
    y_sample = xs.reshape(bs, ts, D_MODEL)
    new_attn_k = jnp.stack(ks_new, axis=1).reshape(bp, n_even, tp, ATT_HEADS, 2, ATT_DH)
    new_attn_v = jnp.stack(vs_new, axis=1).reshape(bp, n_even, tp, ATT_HEADS, ATT_DV)
    new_gla_state = jnp.stack(ss_new, axis=1)
    return (y_prompt, y_sample, new_attn_k, new_attn_v, new_gla_state)
```

```python
import functools
import math

import jax
import jax.numpy as jnp
from jax import lax
from jax.experimental import pallas as pl
from jax.experimental.pallas import tpu as pltpu

F32 = jnp.float32
BF16 = jnp.bfloat16

D_MODEL = 1024
DEPTH = 4
GRID_W = 64
ATT_WIDTH = 512
POOL_WIDTH = 512
ATT_HEADS = 4
ATT_DH = 64
ATT_DV = 128
ROPE_BASE = 10000.0
POOL_WINDOWS = (2, 4, 8, 16)
POOL_GROUP = 128
POOL_PAD = 8
GLA_HEADS = 4
GLA_DK = 128
GLA_DV = 256
GATE_RANK = 16
GATE_TEMP = 16.0
CHUNK = 64
GLA_STEP = 2 * CHUNK
FFN_HIDDEN = 2816
FFN_CHUNK = 256
FFN_TM = 512
EPS = 1e-6
Q_SCALE = ATT_DH ** -0.5 * math.log2(math.e)
SAFE_SHIFT = 50.0
NORM_SLACK = 1.05
NORM_LANES = 16
MOD_ROWS = 16

V7X_VMEM_LIMIT = 56 * 1024 * 1024


def _params(n_axes, vmem_bytes=V7X_VMEM_LIMIT):
    return pltpu.CompilerParams(dimension_semantics=("arbitrary",) * n_axes,
                                vmem_limit_bytes=vmem_bytes)


def _resident(shape):
    nd = len(shape)
    return pl.BlockSpec(shape, lambda *_: (0,) * nd, pipeline_mode=pl.Buffered(1))


def _layer_of(stack, idx):
    nd = stack.ndim - 1
    return pl.BlockSpec((None,) + stack.shape[1:], lambda *_: (idx,) + (0,) * nd,
                        pipeline_mode=pl.Buffered(1))


def _norm_mod(x, g, shift, scale):
    y = x * lax.rsqrt(jnp.mean(x * x, axis=-1, keepdims=True) + EPS)
    return (y * g) * (1.0 + scale) + shift


def _silu(x):
    return x * jax.nn.sigmoid(x)


def _adaln_kernel(cv_ref, w_ref, b_ref, o_ref):
    s = _silu(cv_ref[...]).astype(BF16)
    o_ref[...] = jnp.dot(s, w_ref[...].astype(BF16), preferred_element_type=F32) + b_ref[...]


def _adaln(cvec, w_mod, b_mod):
    tn = 1536
    out = pl.pallas_call(
        _adaln_kernel,
        grid=(DEPTH, 6 * D_MODEL // tn),
        in_specs=[pl.BlockSpec((MOD_ROWS, D_MODEL), lambda l, j: (0, 0)),
                  pl.BlockSpec((None, D_MODEL, tn), lambda l, j: (l, 0, j)),
                  pl.BlockSpec((None, 1, tn), lambda l, j: (l, 0, j))],
        out_specs=pl.BlockSpec((None, MOD_ROWS, tn), lambda l, j: (l, 0, j)),
        out_shape=jax.ShapeDtypeStruct((DEPTH, MOD_ROWS, 6 * D_MODEL), F32),
        compiler_params=_params(2),
        name="adaln",
    )(cvec, w_mod, b_mod.reshape(DEPTH, 1, 6 * D_MODEL))
    return out.reshape(DEPTH, MOD_ROWS, 6, D_MODEL)


def _mod_spec(layer, group_of_step):
    return pl.BlockSpec((None, None, 6, D_MODEL),
                        lambda *ids: (layer, group_of_step(*ids), 0, 0))


def _even_in_kernel(*refs, tm, rc, rope):
    if rope:
        x_ref, mod_ref, g_ref, w_ref, cos_ref, sin_ref, q_ref, k_ref, v_ref, u_ref = refs
    else:
        x_ref, mod_ref, g_ref, w_ref, q_ref, k_ref, v_ref, u_ref = refs
    shift, scale, g = mod_ref[0:1, :], mod_ref[1:2, :], g_ref[...]
    lo = (lax.broadcasted_iota(jnp.int32, (rc, 128), 1) % 32) < 16

    def rotary(y, cos, sin):
        parts = []
        for h in range(ATT_HEADS):
            yh = y[:, h * 128:(h + 1) * 128]
            fwd = pltpu.roll(yh, 128 - 16, axis=1)
            bwd = pltpu.roll(yh, 16, axis=1)
            parts.append(yh * cos + jnp.where(lo, fwd, bwd) * sin)
        return jnp.concatenate(parts, axis=1)

    for r in range(tm // rc):
        rows = slice(r * rc, (r + 1) * rc)
        h = _norm_mod(x_ref[rows, :], g, shift, scale).astype(BF16)
        q = jnp.dot(h, w_ref[:, 0:512], preferred_element_type=F32)
        k = jnp.dot(h, w_ref[:, 512:1024], preferred_element_type=F32)
        if rope:
            cos, sin = cos_ref[rows, :], sin_ref[rows, :]
            q, k = rotary(q, cos, sin), rotary(k, cos, sin)
        q_ref[rows, :] = (q * Q_SCALE).astype(q_ref.dtype)
        k_ref[rows, :] = k.astype(k_ref.dtype)
        v_ref[rows, :] = jnp.dot(h, w_ref[:, 1024:1536], preferred_element_type=F32).astype(v_ref.dtype)
        u_ref[rows, :] = jnp.dot(h, w_ref[:, 1536:2048], preferred_element_type=F32)


def _even_in(x, mod, layer, group_of_step, g1, w_bf, rope_tabs, seq_len, kv_dtype):
    n = x.shape[0]
    tm, rc = 1024, 512
    row = lambda i: (i, 0)
    in_specs = [pl.BlockSpec((tm, D_MODEL), row), _mod_spec(layer, group_of_step),
                _layer_of(*g1), _layer_of(*w_bf)]
    args = [x, mod, g1[0], w_bf[0]]
    if rope_tabs is not None:
        per_seq = seq_len // tm
        pos = lambda i: (i % per_seq, 0)
        in_specs += [pl.BlockSpec((tm, 128), pos), pl.BlockSpec((tm, 128), pos)]
        args += list(rope_tabs)
    out_dt = (BF16, kv_dtype, kv_dtype, F32)
    return pl.pallas_call(
        functools.partial(_even_in_kernel, tm=tm, rc=rc, rope=rope_tabs is not None),
        grid=(n // tm,),
        in_specs=in_specs,
        out_specs=[pl.BlockSpec((tm, 512), row)] * 4,
        out_shape=[jax.ShapeDtypeStruct((n, 512), dt) for dt in out_dt],
        compiler_params=_params(1),
        name="even_in",
    )(*args)


def _even_mix_kernel(*refs, seq, tq, n_cache, lam_init):
    if n_cache:
        (x_ref, q_ref, qseq_ref, k_ref, v_ref, u_ref, ck_ref, cv_ref, mod_ref, pairsum_ref, lamp_ref,
         subg_ref, wpool_ref, pscale_ref, wout_ref, o_ref, upad, bound) = refs
    else:
        (x_ref, q_ref, qseq_ref, k_ref, v_ref, u_ref, mod_ref, pairsum_ref, lamp_ref,
         subg_ref, wpool_ref, pscale_ref, wout_ref, o_ref, upad, bound) = refs
    qi = pl.program_id(1)
    lane = lax.broadcasted_iota(jnp.int32, (1, 128), 1)

    def max_pair_norm2(rows):
        xf = rows.astype(F32)
        n2 = jnp.dot((xf * xf).astype(BF16), pairsum_ref[...], preferred_element_type=F32)
        return jnp.max(n2, axis=0, keepdims=True)

    @pl.when(qi == 0)
    def _per_sequence():
        k2 = max_pair_norm2(k_ref[...])
        if n_cache:
            k2 = jnp.maximum(k2, max_pair_norm2(ck_ref[...]))
        bound[0:1, :] = max_pair_norm2(qseq_ref[...]) * k2 * NORM_SLACK
        zeros = jnp.zeros((POOL_PAD, POOL_WIDTH), F32)
        upad[0:POOL_PAD, :] = zeros
        upad[POOL_PAD + seq:POOL_PAD + seq + POOL_PAD, :] = zeros
        upad[POOL_PAD:POOL_PAD + seq, :] = u_ref[...]

    def pooled_rows():
        base = pl.multiple_of(qi * tq, tq)
        n_blk = tq + 2 * POOL_PAD
        t = lax.broadcasted_iota(jnp.int32, (tq, 1), 0) + qi * tq
        ys = []
        for g, w in enumerate(POOL_WINDOWS):
            cols = slice(g * POOL_GROUP, (g + 1) * POOL_GROUP)
            xg = upad[pl.ds(base, n_blk), cols]
            p = xg + pltpu.roll(xg, 1, axis=0)
            half = 1
            while 2 * half < w:
                p = pltpu.roll(p, half, axis=0) + pltpu.roll(p, n_blk - half, axis=0)
                half *= 2
            cnt = (jnp.minimum(t - w // 2 + w, seq) - jnp.maximum(t - w // 2, 0)).astype(F32)
            ys.append((p[POOL_PAD:POOL_PAD + tq] / cnt - xg[POOL_PAD:POOL_PAD + tq]).astype(BF16))
        return ys

    lp = lamp_ref[...]
    lam = (jnp.exp(jnp.sum(lp[0:1, :] * lp[1:2, :], axis=-1, keepdims=True))
           - jnp.exp(jnp.sum(lp[2:3, :] * lp[3:4, :], axis=-1, keepdims=True)) + lam_init)

    m1 = jnp.where(lane < ATT_DH, 1.0, 0.0).astype(BF16)
    m2 = jnp.where(lane >= ATT_DH, 1.0, 0.0).astype(BF16)
    nt = (((1,), (1,)), ((), ()))

    bound2 = bound[0:1, :]
    worst2 = jnp.max(bound2)

    def scores(h):
        cols = slice(h * 128, (h + 1) * 128)
        qh = q_ref[:, cols]
        q2 = jnp.concatenate([qh * m1, qh * m2], axis=0)
        s = lax.dot_general(q2, k_ref[:, cols].astype(BF16), nt, preferred_element_type=F32)
        sc = None
        if n_cache:
            sc = lax.dot_general(q2, ck_ref[:, cols].astype(BF16), nt, preferred_element_type=F32)
        return s, sc

    def mix(exact_max):
        heads = []
        pending = scores(0)
        ys = pooled_rows()
        for h in range(ATT_HEADS):
            cols = slice(h * 128, (h + 1) * 128)
            s, sc = pending
            if h + 1 < ATT_HEADS:
                pending = scores(h + 1)
            vh = v_ref[:, cols].astype(BF16)
            if n_cache:
                cvh = cv_ref[:, cols].astype(BF16)
            if exact_max:
                mx = jnp.max(s, axis=-1, keepdims=True)
                if n_cache:
                    mx = jnp.maximum(mx, jnp.max(sc, axis=-1, keepdims=True))
            else:
                l1, l2 = NORM_LANES * 2 * h, NORM_LANES * (2 * h + 1)
                mx = jnp.sqrt(jnp.concatenate([jnp.broadcast_to(bound2[:, l1:l1 + 1], (tq, 1)),
                                               jnp.broadcast_to(bound2[:, l2:l2 + 1], (tq, 1))], axis=0))
            e = jnp.exp2(s - mx)
            den = jnp.sum(e, axis=-1, keepdims=True)
            if n_cache:
                ec = jnp.exp2(sc - mx)
                den = den + jnp.sum(ec, axis=-1, keepdims=True)
            inv1 = 1.0 / den[:tq]
            ratio = lam * den[:tq] / den[tq:]
            a = (e[:tq] - ratio * e[tq:]).astype(BF16)
            o = jnp.dot(a, vh, preferred_element_type=F32)
            if n_cache:
                ac = (ec[:tq] - ratio * ec[tq:]).astype(BF16)
                o = o + jnp.dot(ac, cvh, preferred_element_type=F32)
            o = o * inv1
            o = o * lax.rsqrt(jnp.mean(o * o, axis=-1, keepdims=True) + EPS)
            heads.append((o * subg_ref[:, cols]) * (1.0 - lam_init))
        o_att = jnp.concatenate(heads, axis=1).astype(BF16)

        pooled = [jnp.dot(y, wpool_ref[g], preferred_element_type=F32) for g, y in enumerate(ys)]
        p = (jnp.concatenate(pooled, axis=1) * pscale_ref[...]).astype(BF16)

        out = (jnp.dot(o_att, wout_ref[0:ATT_WIDTH, :], preferred_element_type=F32)
               + jnp.dot(p, wout_ref[ATT_WIDTH:, :], preferred_element_type=F32))
        o_ref[...] = x_ref[...] + mod_ref[2:3, :] * out

    lax.cond(worst2 <= SAFE_SHIFT * SAFE_SHIFT, lambda: mix(False), lambda: mix(True))


def _even_mix(x, q, k, v, u, cache, mod, layer, group_of_seq, lamp, subg, wpool_bf, pscale,
              wout_bf, seq, lam_init):
    n = x.shape[0]
    nseq = n // seq
    tq = 256
    nq = seq // tq
    blk = lambda b, qi: (b * nq + qi, 0)
    whole = lambda b, qi: (b, 0)
    in_specs = [pl.BlockSpec((tq, D_MODEL), blk), pl.BlockSpec((tq, 512), blk),
                pl.BlockSpec((seq, 512), whole), pl.BlockSpec((seq, 512), whole),
                pl.BlockSpec((seq, 512), whole), pl.BlockSpec((seq, 512), whole)]
    args = [x, q, q, k, v, u]
    n_cache = 0
    if cache is not None:
        ck, cv, il = cache
        n_cache = ck.shape[2]
        cspec = pl.BlockSpec((None, None, n_cache, 512), lambda b, qi: (b, il, 0, 0))
        in_specs += [cspec, cspec]
        args += [ck, cv]
    pairsum = (jnp.arange(ATT_WIDTH)[:, None] // ATT_DH == jnp.arange(128)[None, :] // NORM_LANES).astype(BF16)
    weights = [lamp, subg, wpool_bf, pscale, wout_bf]
    in_specs += ([_mod_spec(layer, lambda b, qi: group_of_seq(b)), _resident((ATT_WIDTH, 128))]
                 + [_layer_of(*w) for w in weights])
    args += [mod, pairsum] + [w[0] for w in weights]
    return pl.pallas_call(
        functools.partial(_even_mix_kernel, seq=seq, tq=tq, n_cache=n_cache, lam_init=lam_init),
        grid=(nseq, nq),
        in_specs=in_specs,
        out_specs=pl.BlockSpec((tq, D_MODEL), blk),
        out_shape=jax.ShapeDtypeStruct((n, D_MODEL), F32),
        scratch_shapes=[pltpu.VMEM((seq + 2 * POOL_PAD, POOL_WIDTH), F32),
                        pltpu.VMEM((8, 128), F32)],
        compiler_params=_params(2),
        name="even_mix",
    )(*args)


def _ffn_kernel(*refs, gla_pre, final):
    refs = list(refs)
    x_ref, mod_ref = refs.pop(0), refs.pop(0)
    if gla_pre:
        og_ref, r_ref, gn_ref, wmix_ref = (refs.pop(0) for _ in range(4))
    g_ref, wi_ref, wo_ref = (refs.pop(0) for _ in range(3))
    fg_ref = refs.pop(0) if final else None
    o_ref, acc_ref = refs
    x = x_ref[...]
    if gla_pre:
        parts = []
        for h in range(GLA_HEADS):
            cols = slice(h * GLA_DV, (h + 1) * GLA_DV)
            oh = og_ref[:, cols]
            oh = oh * lax.rsqrt(jnp.mean(oh * oh, axis=-1, keepdims=True) + EPS)
            parts.append(oh * gn_ref[:, cols])
        z = (jnp.concatenate(parts, axis=1) * _silu(r_ref[...])).astype(BF16)
        x = x + mod_ref[2:3, :] * jnp.dot(z, wmix_ref[...], preferred_element_type=F32)
    h = _norm_mod(x, g_ref[...], mod_ref[3:4, :], mod_ref[4:5, :]).astype(BF16)
    for c in range(FFN_HIDDEN // FFN_CHUNK):
        lo, hi = c * FFN_CHUNK, (c + 1) * FFN_CHUNK
        a = jnp.dot(h, wi_ref[:, lo:hi], preferred_element_type=F32)
        b = jnp.dot(h, wi_ref[:, FFN_HIDDEN + lo:FFN_HIDDEN + hi], preferred_element_type=F32)
        z = (_silu(a) * b).astype(BF16)
        part = jnp.dot(z, wo_ref[lo:hi, :], preferred_element_type=F32)
        if c == 0:
            acc_ref[...] = part
        else:
            acc_ref[...] += part
    y = x + mod_ref[5:6, :] * acc_ref[...]
    if final:
        y = (y * lax.rsqrt(jnp.mean(y * y, axis=-1, keepdims=True) + EPS)) * fg_ref[...]
    o_ref[...] = y


def _ffn(x, mod, layer, group_of_step, g2, wi_bf, wo_bf, gla_pre=None, final_g=None):
    n = x.shape[0]
    tm = FFN_TM
    row = lambda i: (i, 0)
    tile = pl.BlockSpec((tm, D_MODEL), row)
    in_specs = [tile, _mod_spec(layer, group_of_step)]
    args = [x, mod]
    if gla_pre is not None:
        o, r, gn, wmix_bf = gla_pre
        in_specs += [tile, tile, _layer_of(*gn), _layer_of(*wmix_bf)]
        args += [o, r, gn[0], wmix_bf[0]]
    in_specs += [_layer_of(*g2), _layer_of(*wi_bf), _layer_of(*wo_bf)]
    args += [g2[0], wi_bf[0], wo_bf[0]]
    if final_g is not None:
        in_specs.append(_resident((1, D_MODEL)))
        args.append(final_g)
    return pl.pallas_call(
        functools.partial(_ffn_kernel, gla_pre=gla_pre is not None, final=final_g is not None),
        grid=(n // tm,),
        in_specs=in_specs,
        out_specs=tile,
        out_shape=jax.ShapeDtypeStruct((n, D_MODEL), F32),
        scratch_shapes=[pltpu.VMEM((tm, D_MODEL), F32)],
        compiler_params=_params(1),
        name="ffn",
    )(*args)


def _odd_in_kernel(x_ref, mod_ref, g_ref, w_ref, wgd_ref, wup_ref, bg_ref,
                   qk_ref, v_ref, r_ref, gate_ref, *, tm, rc):
    shift, scale, g = mod_ref[0:1, :], mod_ref[1:2, :], g_ref[...]
    for c in range(tm // rc):
        rows = slice(c * rc, (c + 1) * rc)
        h = _norm_mod(x_ref[rows, :], g, shift, scale).astype(BF16)
        gd = jnp.dot(h, wgd_ref[...], preferred_element_type=F32).astype(BF16)
        logit = jnp.dot(gd, wup_ref[...], preferred_element_type=F32) + bg_ref[...]
        qk_ref[rows, :] = jnp.dot(h, w_ref[:, 0:1024], preferred_element_type=F32)
        v_ref[rows, :] = jnp.dot(h, w_ref[:, 1024:2048], preferred_element_type=F32).astype(BF16)
        r_ref[rows, :] = jnp.dot(h, w_ref[:, 2048:3072], preferred_element_type=F32)
        log_sig = jnp.minimum(logit, 0.0) - jnp.log(1.0 + jnp.exp(-jnp.abs(logit)))
        gate_ref[rows, :] = log_sig / GATE_TEMP


def _odd_in(x, mod, layer, group_of_step, g1, w_bf, wgd_bf, wup_bf, bg):
    n = x.shape[0]
    tm, rc = 1024, 512
    row = lambda i: (i, 0)
    return pl.pallas_call(
        functools.partial(_odd_in_kernel, tm=tm, rc=rc),
        grid=(n // tm,),
        in_specs=[pl.BlockSpec((tm, D_MODEL), row), _mod_spec(layer, group_of_step),
                  _layer_of(*g1), _layer_of(*w_bf),
                  _resident((D_MODEL, 128)), _resident((128, D_MODEL)), _layer_of(*bg)],
        out_specs=[pl.BlockSpec((tm, D_MODEL), row)] * 4,
        out_shape=[jax.ShapeDtypeStruct((n, D_MODEL), dt) for dt in (F32, BF16, F32, F32)],
        compiler_params=_params(1),
        name="odd_in",
    )(x, mod, g1[0], w_bf[0], wgd_bf, wup_bf, bg[0])


def _chunk_scan(x, reverse):
    n = x.shape[0]
    pos = lax.broadcasted_iota(jnp.int32, x.shape, 0) % CHUNK
    step = 1
    while step < CHUNK:
        if reverse:
            x = x + jnp.where(pos < CHUNK - step, pltpu.roll(x, n - step, axis=0), 0.0)
        else:
            x = x + jnp.where(pos >= step, pltpu.roll(x, step, axis=0), 0.0)
        step *= 2
    return x


def _gla_kernel(*refs, tb, nsb, has_init, state_slot, has_earlier):
    refs = list(refs)
    ins = [refs.pop(0) for _ in range(6)]
    init_ref = refs.pop(0) if has_init else None
    if has_earlier:
        refs.pop(0)
    o_ref = refs.pop(0)
    state_ref = refs.pop(0) if state_slot is not None else None
    st = refs.pop(0)
    n = pl.program_id(1)

    @pl.when(n == 0)
    def _init():
        o_ref[...] = jnp.zeros(o_ref.shape, F32)
        if has_init:
            st[...] = init_ref[...]
        else:
            st[...] = jnp.zeros(st.shape, F32)

    ri = lax.broadcasted_iota(jnp.int32, (GLA_STEP, GLA_STEP), 0)
    ci = lax.broadcasted_iota(jnp.int32, (GLA_STEP, GLA_STEP), 1)
    first = lax.broadcasted_iota(jnp.int32, (GLA_STEP, 512), 0) < CHUNK
    nt = (((1,), (1,)), ((), ()))
    nstep = tb // GLA_STEP

    def body(i, carry):
        for d in range(2):
            qk_ref, v_ref, g_ref = ins[3 * d:3 * d + 3]
            c = i if d == 0 else nstep - 1 - i
            block = n if d == 0 else nsb - 1 - n
            rows = pl.ds(pl.multiple_of(c * GLA_STEP, GLA_STEP), GLA_STEP)
            orows = pl.ds(pl.multiple_of(block * tb + c * GLA_STEP, GLA_STEP), GLA_STEP)
            p = _chunk_scan(g_ref[rows, :], reverse=d == 1)
            if d == 0:
                mask = ci <= ri
                tot_in, tot_out = p[CHUNK - 1:CHUNK, :], p[GLA_STEP - 1:GLA_STEP, :]
                b_mid = jnp.where(first, p - tot_in, p)
            else:
                mask = ci >= ri
                tot_out, tot_in = p[0:1, :], p[CHUNK:CHUNK + 1, :]
                b_mid = jnp.where(first, p, p - tot_in)
            tot = tot_in + tot_out
            q = qk_ref[rows, 0:512] * (GLA_DK ** -0.5)
            k = qk_ref[rows, 512:1024]
            q_mid = q * jnp.exp(b_mid)
            k_mid = k * jnp.exp(-b_mid)
            q_t = q_mid.astype(BF16)
            k_t = k_mid.astype(BF16)
            q_in = (q_mid * jnp.exp(tot_in)).astype(BF16)
            k_end = k_mid * jnp.exp(tot_out)
            decay = jnp.exp(jnp.broadcast_to(tot, (8, 512)).T[:, 0:1])
            for h in range(GLA_HEADS):
                kc = slice(h * GLA_DK, (h + 1) * GLA_DK)
                vc = slice(h * GLA_DV, (h + 1) * GLA_DV)
                vh = v_ref[rows, vc]
                a = lax.dot_general(q_t[:, kc], k_t[:, kc], nt, preferred_element_type=F32)
                a = jnp.where(mask, a, 0.0).astype(BF16)
                lhs = jnp.concatenate([a, k_end[:, kc].T.astype(BF16)], axis=0)
                both = jnp.dot(lhs, vh, preferred_element_type=F32)
                s = st[d, h]
                o = both[:GLA_STEP] + jnp.dot(q_in[:, kc], s.astype(BF16), preferred_element_type=F32)
                o_ref[orows, vc] += o
                st[d, h] = s * decay[kc, :] + both[GLA_STEP:]
        return carry

    lax.fori_loop(0, nstep, body, 0, unroll=2)

    if state_slot is not None:
        @pl.when(n == pl.num_programs(1) - 1)
        def _emit():
            for s in range(state_ref.shape[0]):
                state_ref[s] = st[...] if s == state_slot else jnp.zeros(st.shape, F32)


def _gla(qk, v, gate, init, seq, emit_state):
    n = qk.shape[0]
    nseq = n // seq
    tb = min(seq, 512)
    nsb = seq // tb
    fwd = lambda b, s: (b * nsb + s, 0)
    bwd = lambda b, s: (b * nsb + nsb - 1 - s, 0)
    bwd_gate = lambda b, s: (b * nsb + nsb - 1 - s, 1)
    in_specs = [pl.BlockSpec((tb, 1024), fwd), pl.BlockSpec((tb, 1024), fwd), pl.BlockSpec((tb, 512), fwd),
                pl.BlockSpec((tb, 1024), bwd), pl.BlockSpec((tb, 1024), bwd), pl.BlockSpec((tb, 512), bwd_gate)]
    args = [qk, v, gate, qk, v, gate]
    if init is not None:
        state, il = init
        in_specs.append(pl.BlockSpec((None, None, 2, GLA_HEADS, GLA_DK, GLA_DV),
                                     lambda b, s: (b, il, 0, 0, 0, 0)))
        args.append(state)
    out_specs = [pl.BlockSpec((seq, 1024), lambda b, s: (b, 0))]
    out_shape = [jax.ShapeDtypeStruct((n, 1024), F32)]
    aliases = {}
    if emit_state is not None:
        slot, n_slots, earlier = emit_state
        out_shape.append(jax.ShapeDtypeStruct((nseq, n_slots, 2, GLA_HEADS, GLA_DK, GLA_DV), F32))
        if earlier is None:
            out_specs.append(pl.BlockSpec((None, n_slots, 2, GLA_HEADS, GLA_DK, GLA_DV),
                                          lambda b, s: (b, 0, 0, 0, 0, 0)))
        else:
            out_specs.append(pl.BlockSpec((None, 1, 2, GLA_HEADS, GLA_DK, GLA_DV),
                                          lambda b, s: (b, slot, 0, 0, 0, 0)))
            in_specs.append(pl.BlockSpec(memory_space=pl.ANY))
            args.append(earlier)
            aliases = {len(args) - 1: 1}
    return pl.pallas_call(
        functools.partial(_gla_kernel, tb=tb, nsb=nsb, has_init=init is not None,
                          state_slot=None if emit_state is None else (0 if aliases else emit_state[0]),
                          has_earlier=bool(aliases)),
        grid=(nseq, nsb),
        in_specs=in_specs,
        out_specs=out_specs,
        out_shape=out_shape,
        input_output_aliases=aliases,
        scratch_shapes=[pltpu.VMEM((2, GLA_HEADS, GLA_DK, GLA_DV), F32)],
        compiler_params=_params(2),
        name="gla",
    )(*args)


def _rope_tables(n_lat):
    half = ATT_DH // 2
    inv = 1.0 / (ROPE_BASE ** (jnp.arange(0, half, 2, dtype=F32) / half))
    t = jnp.arange(n_lat)
    rows = (t // GRID_W).astype(F32)
    cols = (t % GRID_W).astype(F32)

    def tab(pos):
        ang = pos[:, None] * inv[None, :]
        ang = jnp.concatenate([ang, ang], axis=-1)
        return jnp.cos(ang), jnp.sin(ang)

    (cr, sr), (cc, sc) = tab(rows), tab(cols)
    sign = jnp.concatenate([-jnp.ones((half // 2,), F32), jnp.ones((half // 2,), F32)])
    cos = jnp.concatenate([cr, cc], axis=-1)
    sin = jnp.concatenate([sr * sign, sc * sign], axis=-1)
    return jnp.tile(cos, (1, 2)), jnp.tile(sin, (1, 2))


def kernel(x_prompt, x_sample, c, cache_attn_k, cache_attn_v, state_gla, c_ctx, norm1_g, norm2_g, w_mod, b_mod, w_in_even, lam_params, subln_g, w_pool, pool_scale, w_out_even, w_in_odd, w_gate_up, b_gate, gla_norm_g, w_out_odd, w_ffn_in, w_ffn_out, final_g):
    bp, tp, _ = x_prompt.shape
    bs, ts, _ = x_sample.shape
    past = cache_attn_k.shape[2]
    n_even = cache_attn_k.shape[1]
    assert bs + 1 <= MOD_ROWS

    xp = x_prompt.reshape(bp * tp, D_MODEL)
    xs = x_sample.reshape(bs * ts, D_MODEL)
    ck = cache_attn_k.reshape(bs, n_even, past, ATT_WIDTH)
    cv = cache_attn_v.reshape(bs, n_even, past, ATT_WIDTH)

    cvec = jnp.concatenate([c_ctx[None, :], c, jnp.zeros((MOD_ROWS - 1 - bs, D_MODEL), F32)], axis=0)
    mod = _adaln(cvec, w_mod, b_mod)
    ctx_group = lambda i: 0

    def lat_group(tm):
        return lambda i: 1 + (i * tm) // ts

    rope = _rope_tables(ts)

    row3 = lambda a: a.reshape(a.shape[0], 1, a.shape[1])
    n1, n2 = row3(norm1_g), row3(norm2_g)
    w_in_even_bf, w_out_even_bf, w_pool_bf = (w.astype(BF16) for w in (w_in_even, w_out_even, w_pool))
    w_in_odd_bf, w_out_odd_bf = w_in_odd.astype(BF16), w_out_odd.astype(BF16)
    w_ffn_in_bf, w_ffn_out_bf = w_ffn_in.astype(BF16), w_ffn_out.astype(BF16)
    subg3, pscale3, gn3 = row3(subln_g), row3(pool_scale), row3(gla_norm_g)
    bg3 = b_gate.reshape(b_gate.shape[0], 1, D_MODEL)

    fg = final_g.reshape(1, D_MODEL)
    ks_new, vs_new, states = [], [], None
    for l in range(DEPTH):
        i = l // 2
        g1, g2 = (n1, l), (n2, l)
        last = fg if l == DEPTH - 1 else None
        pre_p = pre_s = None
        if l % 2 == 0:
            lam_init = 0.8 - 0.6 * math.exp(-0.3 * l)
            w_in, w_out, wpool = (w_in_even_bf, i), (w_out_even_bf, i), (w_pool_bf, i)
            subg, pscale, lamp = (subg3, i), (pscale3, i), (lam_params, i)
            q, k, v, u = _even_in(xp, mod, l, ctx_group, g1, w_in, None, tp, F32)
            ks_new.append(k.reshape(bp, tp, ATT_WIDTH))
            vs_new.append(v.reshape(bp, tp, ATT_WIDTH))
            xp = _even_mix(xp, q, k, v, u, None, mod, l, ctx_group, lamp, subg, wpool, pscale,
                           w_out, tp, lam_init)
            q, k, v, u = _even_in(xs, mod, l, lat_group(1024), g1, w_in, rope, ts, BF16)
            xs = _even_mix(xs, q, k, v, u, (ck, cv, i), mod, l, lambda b: 1 + b, lamp, subg, wpool,
                           pscale, w_out, ts, lam_init)
        else:
            w_main, w_out, gn, bg = (w_in_odd_bf, i), (w_out_odd_bf, i), (gn3, i), (bg3, i)
            wgd = jnp.pad(w_in_odd[i][:, 3072:], ((0, 0), (0, 128 - 2 * GATE_RANK))).astype(BF16)
            wup = jnp.zeros((128, D_MODEL), F32)
            wup = wup.at[0:GATE_RANK, 0:512].set(w_gate_up[i, 0])
            wup = wup.at[GATE_RANK:2 * GATE_RANK, 512:1024].set(w_gate_up[i, 1]).astype(BF16)
            qk, v, r, gate = _odd_in(xp, mod, l, ctx_group, g1, w_main, wgd, wup, bg)
            o, states = _gla(qk, v, gate, None, tp, (i, w_in_odd.shape[0], states))
            pre_p = (o, r, gn, w_out)
            qk, v, r, gate = _odd_in(xs, mod, l, lat_group(1024), g1, w_main, wgd, wup, bg)
            (o,) = _gla(qk, v, gate, (state_gla, i), ts, None)
            pre_s = (o, r, gn, w_out)
        wi, wo = (w_ffn_in_bf, l), (w_ffn_out_bf, l)
        xp = _ffn(xp, mod, l, ctx_group, g2, wi, wo, pre_p, last)
        xs = _ffn(xs, mod, l, lat_group(FFN_TM), g2, wi, wo, pre_s, last)

    y_prompt = xp.reshape(bp, tp, D_MODEL)
    y_sample = xs.reshape(bs, ts, D_MODEL)
    new_attn_k = jnp.stack(ks_new, axis=1).reshape(bp, n_even, tp, ATT_HEADS, 2, ATT_DH)
    new_attn_v = jnp.stack(vs_new, axis=1).reshape(bp, n_even, tp, ATT_HEADS, ATT_DV)
    return (y_prompt, y_sample, new_attn_k, new_attn_v, states)
```

```python
import functools
import math

import jax
import jax.numpy as jnp
from jax import lax
from jax.experimental import pallas as pl
from jax.experimental.pallas import tpu as pltpu

F32 = jnp.float32
BF16 = jnp.bfloat16

D_MODEL = 1024
DEPTH = 4
GRID_W = 64
ATT_WIDTH = 512
POOL_WIDTH = 512
ATT_HEADS = 4
ATT_DH = 64
ATT_DV = 128
ROPE_BASE = 10000.0
POOL_WINDOWS = (2, 4, 8, 16)
POOL_GROUP = 128
POOL_PAD = 8
GLA_HEADS = 4
GLA_DK = 128
GLA_DV = 256
GATE_RANK = 16
GATE_TEMP = 16.0
CHUNK = 64
GLA_STEP = 2 * CHUNK
FFN_HIDDEN = 2816
FFN_CHUNK = 256
FFN_TM = 512
EPS = 1e-6
Q_SCALE = ATT_DH ** -0.5 * math.log2(math.e)
SAFE_SHIFT = 50.0
NORM_SLACK = 1.05
NORM_LANES = 16
MOD_ROWS = 16

V7X_VMEM_LIMIT = 56 * 1024 * 1024


def _params(n_axes, vmem_bytes=V7X_VMEM_LIMIT):
    return pltpu.CompilerParams(dimension_semantics=("arbitrary",) * n_axes,
                                vmem_limit_bytes=vmem_bytes)


def _resident(shape):
    nd = len(shape)
    return pl.BlockSpec(shape, lambda *_: (0,) * nd, pipeline_mode=pl.Buffered(1))


def _layer_of(stack, idx):
    nd = stack.ndim - 1
    return pl.BlockSpec((None,) + stack.shape[1:], lambda *_: (idx,) + (0,) * nd,
                        pipeline_mode=pl.Buffered(1))


def _norm_mod(x, g, shift, scale):
    y = x * lax.rsqrt(jnp.mean(x * x, axis=-1, keepdims=True) + EPS)
    return (y * g) * (1.0 + scale) + shift


def _silu(x):
    return x * jax.nn.sigmoid(x)


def _adaln_kernel(cv_ref, w_ref, b_ref, o_ref):
    s = _silu(cv_ref[...]).astype(BF16)
    o_ref[...] = jnp.dot(s, w_ref[...].astype(BF16), preferred_element_type=F32) + b_ref[...]


def _adaln(cvec, w_mod, b_mod):
    tn = 1536
    out = pl.pallas_call(
        _adaln_kernel,
        grid=(DEPTH, 6 * D_MODEL // tn),
        in_specs=[pl.BlockSpec((MOD_ROWS, D_MODEL), lambda l, j: (0, 0)),
                  pl.BlockSpec((None, D_MODEL, tn), lambda l, j: (l, 0, j)),
                  pl.BlockSpec((None, 1, tn), lambda l, j: (l, 0, j))],
        out_specs=pl.BlockSpec((None, MOD_ROWS, tn), lambda l, j: (l, 0, j)),
        out_shape=jax.ShapeDtypeStruct((DEPTH, MOD_ROWS, 6 * D_MODEL), F32),
        compiler_params=_params(2),
        name="adaln",
    )(cvec, w_mod, b_mod.reshape(DEPTH, 1, 6 * D_MODEL))
    return out.reshape(DEPTH, MOD_ROWS, 6, D_MODEL)


def _mod_spec(layer, group_of_step):
    return pl.BlockSpec((None, None, 6, D_MODEL),
                        lambda *ids: (layer, group_of_step(*ids), 0, 0))


def _even_in_kernel(*refs, tm, rc, rope, kv_slot, seq):
    refs = list(refs)
    x_ref, mod_ref, g_ref, w_ref = (refs.pop(0) for _ in range(4))
    cos_ref, sin_ref = (refs.pop(0), refs.pop(0)) if rope else (None, None)
    q_ref, k_ref, v_ref, u_ref = refs[-4:]
    shift, scale, g = mod_ref[0:1, :], mod_ref[1:2, :], g_ref[...]

    def put_kv(dst, rows, val):
        if kv_slot is None:
            dst[rows, :] = val.astype(dst.dtype)
            return
        s0, s1 = rows.start // seq, rows.stop // seq
        for s in range(dst.shape[1]):
            dst[s0:s1, s] = (val.reshape(s1 - s0, seq, val.shape[-1]) if s == kv_slot
                             else jnp.zeros((s1 - s0, seq, val.shape[-1]), dst.dtype))
    lo = (lax.broadcasted_iota(jnp.int32, (rc, 128), 1) % 32) < 16

    def rotary(y, cos, sin):
        parts = []
        for h in range(ATT_HEADS):
            yh = y[:, h * 128:(h + 1) * 128]
            fwd = pltpu.roll(yh, 128 - 16, axis=1)
            bwd = pltpu.roll(yh, 16, axis=1)
            parts.append(yh * cos + jnp.where(lo, fwd, bwd) * sin)
        return jnp.concatenate(parts, axis=1)

    for r in range(tm // rc):
        rows = slice(r * rc, (r + 1) * rc)
        h = _norm_mod(x_ref[rows, :], g, shift, scale).astype(BF16)
        q = jnp.dot(h, w_ref[:, 0:512], preferred_element_type=F32)
        k = jnp.dot(h, w_ref[:, 512:1024], preferred_element_type=F32)
        if rope:
            cos, sin = cos_ref[rows, :], sin_ref[rows, :]
            q, k = rotary(q, cos, sin), rotary(k, cos, sin)
        q_ref[rows, :] = (q * Q_SCALE).astype(q_ref.dtype)
        put_kv(k_ref, rows, k)
        put_kv(v_ref, rows, jnp.dot(h, w_ref[:, 1024:1536], preferred_element_type=F32))
        u_ref[rows, :] = jnp.dot(h, w_ref[:, 1536:2048], preferred_element_type=F32)


def _even_in(x, mod, layer, group_of_step, g1, w_bf, rope_tabs, seq_len, kv_dtype, cache_out=None):
    n = x.shape[0]
    tm, rc = 1024, 512
    row = lambda i: (i, 0)
    in_specs = [pl.BlockSpec((tm, D_MODEL), row), _mod_spec(layer, group_of_step),
                _layer_of(*g1), _layer_of(*w_bf)]
    args = [x, mod, g1[0], w_bf[0]]
    if rope_tabs is not None:
        per_seq = seq_len // tm
        pos = lambda i: (i % per_seq, 0)
        in_specs += [pl.BlockSpec((tm, 128), pos), pl.BlockSpec((tm, 128), pos)]
        args += list(rope_tabs)
    tile = pl.BlockSpec((tm, 512), row)
    kv_spec, kv_shape, kv_slot, aliases = tile, jax.ShapeDtypeStruct((n, 512), kv_dtype), None, {}
    if cache_out is not None:
        slot, n_slots, ek, ev = cache_out
        spt = tm // seq_len
        kv_shape = jax.ShapeDtypeStruct((n // seq_len, n_slots, seq_len, 512), kv_dtype)
        if ek is None:
            kv_spec, kv_slot = pl.BlockSpec((spt, n_slots, seq_len, 512), lambda i: (i, 0, 0, 0)), slot
        else:
            kv_spec, kv_slot = pl.BlockSpec((spt, 1, seq_len, 512), lambda i: (i, slot, 0, 0)), 0
            in_specs += [pl.BlockSpec(memory_space=pl.ANY)] * 2
            args += [ek, ev]
            aliases = {len(args) - 2: 1, len(args) - 1: 2}
    return pl.pallas_call(
        functools.partial(_even_in_kernel, tm=tm, rc=rc, rope=rope_tabs is not None,
                          kv_slot=kv_slot, seq=seq_len),
        grid=(n // tm,),
        in_specs=in_specs,
        out_specs=[tile, kv_spec, kv_spec, tile],
        out_shape=[jax.ShapeDtypeStruct((n, 512), BF16), kv_shape, kv_shape,
                   jax.ShapeDtypeStruct((n, 512), F32)],
        input_output_aliases=aliases,
        compiler_params=_params(1),
        name="even_in",
    )(*args)


def _even_mix_kernel(*refs, seq, tq, n_cache, lam_init):
    if n_cache:
        (x_ref, q_ref, qseq_ref, k_ref, v_ref, u_ref, ck_ref, cv_ref, mod_ref, pairsum_ref, lamp_ref,
         subg_ref, wpool_ref, pscale_ref, wout_ref, o_ref, upad, bound) = refs
    else:
        (x_ref, q_ref, qseq_ref, k_ref, v_ref, u_ref, mod_ref, pairsum_ref, lamp_ref,
         subg_ref, wpool_ref, pscale_ref, wout_ref, o_ref, upad, bound) = refs
    qi = pl.program_id(1)
    lane = lax.broadcasted_iota(jnp.int32, (1, 128), 1)

    def max_pair_norm2(rows):
        xf = rows.astype(F32)
        n2 = jnp.dot((xf * xf).astype(BF16), pairsum_ref[...], preferred_element_type=F32)
        return jnp.max(n2, axis=0, keepdims=True)

    @pl.when(qi == 0)
    def _per_sequence():
        k2 = max_pair_norm2(k_ref[...])
        if n_cache:
            k2 = jnp.maximum(k2, max_pair_norm2(ck_ref[...]))
        bound[0:1, :] = max_pair_norm2(qseq_ref[...]) * k2 * NORM_SLACK
        zeros = jnp.zeros((POOL_PAD, POOL_WIDTH), F32)
        upad[0:POOL_PAD, :] = zeros
        upad[POOL_PAD + seq:POOL_PAD + seq + POOL_PAD, :] = zeros
        upad[POOL_PAD:POOL_PAD + seq, :] = u_ref[...]

    def pooled_rows():
        base = pl.multiple_of(qi * tq, tq)
        n_blk = tq + 2 * POOL_PAD
        t = lax.broadcasted_iota(jnp.int32, (tq, 1), 0) + qi * tq
        ys = []
        for g, w in enumerate(POOL_WINDOWS):
            cols = slice(g * POOL_GROUP, (g + 1) * POOL_GROUP)
            xg = upad[pl.ds(base, n_blk), cols]
            p = xg + pltpu.roll(xg, 1, axis=0)
            half = 1
            while 2 * half < w:
                p = pltpu.roll(p, half, axis=0) + pltpu.roll(p, n_blk - half, axis=0)
                half *= 2
            cnt = (jnp.minimum(t - w // 2 + w, seq) - jnp.maximum(t - w // 2, 0)).astype(F32)
            ys.append((p[POOL_PAD:POOL_PAD + tq] / cnt - xg[POOL_PAD:POOL_PAD + tq]).astype(BF16))
        return ys

    lp = lamp_ref[...]
    lam = (jnp.exp(jnp.sum(lp[0:1, :] * lp[1:2, :], axis=-1, keepdims=True))
           - jnp.exp(jnp.sum(lp[2:3, :] * lp[3:4, :], axis=-1, keepdims=True)) + lam_init)

    m1 = jnp.where(lane < ATT_DH, 1.0, 0.0).astype(BF16)
    m2 = jnp.where(lane >= ATT_DH, 1.0, 0.0).astype(BF16)
    nt = (((1,), (1,)), ((), ()))

    bound2 = bound[0:1, :]
    worst2 = jnp.max(bound2)

    def scores(h):
        cols = slice(h * 128, (h + 1) * 128)
        qh = q_ref[:, cols]
        q2 = jnp.concatenate([qh * m1, qh * m2], axis=0)
        s = lax.dot_general(q2, k_ref[:, cols].astype(BF16), nt, preferred_element_type=F32)
        sc = None
        if n_cache:
            sc = lax.dot_general(q2, ck_ref[:, cols].astype(BF16), nt, preferred_element_type=F32)
        return s, sc

    def mix(exact_max):
        heads = []
        pending = scores(0)
        ys = pooled_rows()
        for h in range(ATT_HEADS):
            cols = slice(h * 128, (h + 1) * 128)
            s, sc = pending
            if h + 1 < ATT_HEADS:
                pending = scores(h + 1)
            vh = v_ref[:, cols].astype(BF16)
            if n_cache:
                cvh = cv_ref[:, cols].astype(BF16)
            if exact_max:
                mx = jnp.max(s, axis=-1, keepdims=True)
                if n_cache:
                    mx = jnp.maximum(mx, jnp.max(sc, axis=-1, keepdims=True))
            else:
                l1, l2 = NORM_LANES * 2 * h, NORM_LANES * (2 * h + 1)
                mx = jnp.sqrt(jnp.concatenate([jnp.broadcast_to(bound2[:, l1:l1 + 1], (tq, 1)),
                                               jnp.broadcast_to(bound2[:, l2:l2 + 1], (tq, 1))], axis=0))
            e = jnp.exp2(s - mx)
            den = jnp.sum(e, axis=-1, keepdims=True)
            if n_cache:
                ec = jnp.exp2(sc - mx)
                den = den + jnp.sum(ec, axis=-1, keepdims=True)
            inv1 = 1.0 / den[:tq]
            ratio = lam * den[:tq] / den[tq:]
            a = (e[:tq] - ratio * e[tq:]).astype(BF16)
            o = jnp.dot(a, vh, preferred_element_type=F32)
            if n_cache:
                ac = (ec[:tq] - ratio * ec[tq:]).astype(BF16)
                o = o + jnp.dot(ac, cvh, preferred_element_type=F32)
            o = o * inv1
            o = o * lax.rsqrt(jnp.mean(o * o, axis=-1, keepdims=True) + EPS)
            heads.append((o * subg_ref[:, cols]) * (1.0 - lam_init))
        o_att = jnp.concatenate(heads, axis=1).astype(BF16)

        pooled = [jnp.dot(y, wpool_ref[g], preferred_element_type=F32) for g, y in enumerate(ys)]
        p = (jnp.concatenate(pooled, axis=1) * pscale_ref[...]).astype(BF16)

        out = (jnp.dot(o_att, wout_ref[0:ATT_WIDTH, :], preferred_element_type=F32)
               + jnp.dot(p, wout_ref[ATT_WIDTH:, :], preferred_element_type=F32))
        o_ref[...] = x_ref[...] + mod_ref[2:3, :] * out

    lax.cond(worst2 <= SAFE_SHIFT * SAFE_SHIFT, lambda: mix(False), lambda: mix(True))


def _even_mix(x, q, k, v, u, cache, mod, layer, group_of_seq, lamp, subg, wpool_bf, pscale,
              wout_bf, seq, lam_init, kv_slot=None):
    n = x.shape[0]
    nseq = n // seq
    tq = 256
    nq = seq // tq
    blk = lambda b, qi: (b * nq + qi, 0)
    whole = lambda b, qi: (b, 0)
    seq_blk = pl.BlockSpec((seq, 512), whole)
    kv_blk = seq_blk if kv_slot is None else pl.BlockSpec((None, None, seq, 512),
                                                          lambda b, qi: (b, kv_slot, 0, 0))
    in_specs = [pl.BlockSpec((tq, D_MODEL), blk), pl.BlockSpec((tq, 512), blk),
                seq_blk, kv_blk, kv_blk, seq_blk]
    args = [x, q, q, k, v, u]
    n_cache = 0
    if cache is not None:
        ck, cv, il = cache
        n_cache = ck.shape[2]
        cspec = pl.BlockSpec((None, None, n_cache, 512), lambda b, qi: (b, il, 0, 0))
        in_specs += [cspec, cspec]
        args += [ck, cv]
    pairsum = (jnp.arange(ATT_WIDTH)[:, None] // ATT_DH == jnp.arange(128)[None, :] // NORM_LANES).astype(BF16)
    weights = [lamp, subg, wpool_bf, pscale, wout_bf]
    in_specs += ([_mod_spec(layer, lambda b, qi: group_of_seq(b)), _resident((ATT_WIDTH, 128))]
                 + [_layer_of(*w) for w in weights])
    args += [mod, pairsum] + [w[0] for w in weights]
    return pl.pallas_call(
        functools.partial(_even_mix_kernel, seq=seq, tq=tq, n_cache=n_cache, lam_init=lam_init),
        grid=(nseq, nq),
        in_specs=in_specs,
        out_specs=pl.BlockSpec((tq, D_MODEL), blk),
        out_shape=jax.ShapeDtypeStruct((n, D_MODEL), F32),
        scratch_shapes=[pltpu.VMEM((seq + 2 * POOL_PAD, POOL_WIDTH), F32),
                        pltpu.VMEM((8, 128), F32)],
        compiler_params=_params(2),
        name="even_mix",
    )(*args)


def _ffn_kernel(*refs, gla_pre, final):
    refs = list(refs)
    x_ref, mod_ref = refs.pop(0), refs.pop(0)
    if gla_pre:
        og_ref, r_ref, gn_ref, wmix_ref = (refs.pop(0) for _ in range(4))
    g_ref, wi_ref, wo_ref = (refs.pop(0) for _ in range(3))
    fg_ref = refs.pop(0) if final else None
    o_ref, acc_ref = refs
    x = x_ref[...]
    if gla_pre:
        parts = []
        for h in range(GLA_HEADS):
            cols = slice(h * GLA_DV, (h + 1) * GLA_DV)
            oh = og_ref[:, cols]
            oh = oh * lax.rsqrt(jnp.mean(oh * oh, axis=-1, keepdims=True) + EPS)
            parts.append(oh * gn_ref[:, cols])
        z = (jnp.concatenate(parts, axis=1) * _silu(r_ref[...])).astype(BF16)
        x = x + mod_ref[2:3, :] * jnp.dot(z, wmix_ref[...], preferred_element_type=F32)
    h = _norm_mod(x, g_ref[...], mod_ref[3:4, :], mod_ref[4:5, :]).astype(BF16)
    for c in range(FFN_HIDDEN // FFN_CHUNK):
        lo, hi = c * FFN_CHUNK, (c + 1) * FFN_CHUNK
        a = jnp.dot(h, wi_ref[:, lo:hi], preferred_element_type=F32)
        b = jnp.dot(h, wi_ref[:, FFN_HIDDEN + lo:FFN_HIDDEN + hi], preferred_element_type=F32)
        z = (_silu(a) * b).astype(BF16)
        part = jnp.dot(z, wo_ref[lo:hi, :], preferred_element_type=F32)
        if c == 0:
            acc_ref[...] = part
        else:
            acc_ref[...] += part
    y = x + mod_ref[5:6, :] * acc_ref[...]
    if final:
        y = (y * lax.rsqrt(jnp.mean(y * y, axis=-1, keepdims=True) + EPS)) * fg_ref[...]
    o_ref[...] = y


def _ffn(x, mod, layer, group_of_step, g2, wi_bf, wo_bf, gla_pre=None, final_g=None):
    n = x.shape[0]
    tm = FFN_TM
    row = lambda i: (i, 0)
    tile = pl.BlockSpec((tm, D_MODEL), row)
    in_specs = [tile, _mod_spec(layer, group_of_step)]
    args = [x, mod]
    if gla_pre is not None:
        o, r, gn, wmix_bf = gla_pre
        in_specs += [tile, tile, _layer_of(*gn), _layer_of(*wmix_bf)]
        args += [o, r, gn[0], wmix_bf[0]]
    in_specs += [_layer_of(*g2), _layer_of(*wi_bf), _layer_of(*wo_bf)]
    args += [g2[0], wi_bf[0], wo_bf[0]]
    if final_g is not None:
        in_specs.append(_resident((1, D_MODEL)))
        args.append(final_g)
    return pl.pallas_call(
        functools.partial(_ffn_kernel, gla_pre=gla_pre is not None, final=final_g is not None),
        grid=(n // tm,),
        in_specs=in_specs,
        out_specs=tile,
        out_shape=jax.ShapeDtypeStruct((n, D_MODEL), F32),
        scratch_shapes=[pltpu.VMEM((tm, D_MODEL), F32)],
        compiler_params=_params(1),
        name="ffn",
    )(*args)


def _odd_in_kernel(x_ref, mod_ref, g_ref, w_ref, wgd_ref, wup_ref, bg_ref,
                   qk_ref, v_ref, r_ref, gate_ref, *, tm, rc):
    shift, scale, g = mod_ref[0:1, :], mod_ref[1:2, :], g_ref[...]
    for c in range(tm // rc):
        rows = slice(c * rc, (c + 1) * rc)
        h = _norm_mod(x_ref[rows, :], g, shift, scale).astype(BF16)
        gd = jnp.dot(h, wgd_ref[...], preferred_element_type=F32).astype(BF16)
        logit = jnp.dot(gd, wup_ref[...], preferred_element_type=F32) + bg_ref[...]
        qk_ref[rows, :] = jnp.dot(h, w_ref[:, 0:1024], preferred_element_type=F32)
        v_ref[rows, :] = jnp.dot(h, w_ref[:, 1024:2048], preferred_element_type=F32).astype(BF16)
        r_ref[rows, :] = jnp.dot(h, w_ref[:, 2048:3072], preferred_element_type=F32)
        log_sig = jnp.minimum(logit, 0.0) - jnp.log(1.0 + jnp.exp(-jnp.abs(logit)))
        gate_ref[rows, :] = log_sig / GATE_TEMP


def _odd_in(x, mod, layer, group_of_step, g1, w_bf, wgd_bf, wup_bf, bg):
    n = x.shape[0]
    tm, rc = 1024, 512
    row = lambda i: (i, 0)
    return pl.pallas_call(
        functools.partial(_odd_in_kernel, tm=tm, rc=rc),
        grid=(n // tm,),
        in_specs=[pl.BlockSpec((tm, D_MODEL), row), _mod_spec(layer, group_of_step),
                  _layer_of(*g1), _layer_of(*w_bf),
                  _resident((D_MODEL, 128)), _resident((128, D_MODEL)), _layer_of(*bg)],
        out_specs=[pl.BlockSpec((tm, D_MODEL), row)] * 4,
        out_shape=[jax.ShapeDtypeStruct((n, D_MODEL), dt) for dt in (F32, BF16, F32, F32)],
        compiler_params=_params(1),
        name="odd_in",
    )(x, mod, g1[0], w_bf[0], wgd_bf, wup_bf, bg[0])


def _chunk_scan(x, reverse):
    n = x.shape[0]
    pos = lax.broadcasted_iota(jnp.int32, x.shape, 0) % CHUNK
    step = 1
    while step < CHUNK:
        if reverse:
            x = x + jnp.where(pos < CHUNK - step, pltpu.roll(x, n - step, axis=0), 0.0)
        else:
            x = x + jnp.where(pos >= step, pltpu.roll(x, step, axis=0), 0.0)
        step *= 2
    return x


def _gla_kernel(*refs, tb, nsb, has_init, state_slot, has_earlier):
    refs = list(refs)
    ins = [refs.pop(0) for _ in range(6)]
    init_ref = refs.pop(0) if has_init else None
    if has_earlier:
        refs.pop(0)
    o_ref = refs.pop(0)
    state_ref = refs.pop(0) if state_slot is not None else None
    st = refs.pop(0)
    n = pl.program_id(1)

    @pl.when(n == 0)
    def _init():
        o_ref[...] = jnp.zeros(o_ref.shape, F32)
        if has_init:
            st[...] = init_ref[...]
        else:
            st[...] = jnp.zeros(st.shape, F32)

    ri = lax.broadcasted_iota(jnp.int32, (GLA_STEP, GLA_STEP), 0)
    ci = lax.broadcasted_iota(jnp.int32, (GLA_STEP, GLA_STEP), 1)
    first = lax.broadcasted_iota(jnp.int32, (GLA_STEP, 512), 0) < CHUNK
    nt = (((1,), (1,)), ((), ()))
    nstep = tb // GLA_STEP

    def body(i, carry):
        for d in range(2):
            qk_ref, v_ref, g_ref = ins[3 * d:3 * d + 3]
            c = i if d == 0 else nstep - 1 - i
            block = n if d == 0 else nsb - 1 - n
            rows = pl.ds(pl.multiple_of(c * GLA_STEP, GLA_STEP), GLA_STEP)
            orows = pl.ds(pl.multiple_of(block * tb + c * GLA_STEP, GLA_STEP), GLA_STEP)
            p = _chunk_scan(g_ref[rows, :], reverse=d == 1)
            if d == 0:
                mask = ci <= ri
                tot_in, tot_out = p[CHUNK - 1:CHUNK, :], p[GLA_STEP - 1:GLA_STEP, :]
                b_mid = jnp.where(first, p - tot_in, p)
            else:
                mask = ci >= ri
                tot_out, tot_in = p[0:1, :], p[CHUNK:CHUNK + 1, :]
                b_mid = jnp.where(first, p, p - tot_in)
            tot = tot_in + tot_out
            q = qk_ref[rows, 0:512] * (GLA_DK ** -0.5)
            k = qk_ref[rows, 512:1024]
            q_mid = q * jnp.exp(b_mid)
            k_mid = k * jnp.exp(-b_mid)
            q_t = q_mid.astype(BF16)
            k_t = k_mid.astype(BF16)
            q_in = (q_mid * jnp.exp(tot_in)).astype(BF16)
            k_end = k_mid * jnp.exp(tot_out)
            decay = jnp.exp(jnp.broadcast_to(tot, (8, 512)).T[:, 0:1])
            for h in range(GLA_HEADS):
                kc = slice(h * GLA_DK, (h + 1) * GLA_DK)
                vc = slice(h * GLA_DV, (h + 1) * GLA_DV)
                vh = v_ref[rows, vc]
                a = lax.dot_general(q_t[:, kc], k_t[:, kc], nt, preferred_element_type=F32)
                a = jnp.where(mask, a, 0.0).astype(BF16)
                lhs = jnp.concatenate([a, k_end[:, kc].T.astype(BF16)], axis=0)
                both = jnp.dot(lhs, vh, preferred_element_type=F32)
                s = st[d, h]
                o = both[:GLA_STEP] + jnp.dot(q_in[:, kc], s.astype(BF16), preferred_element_type=F32)
                o_ref[orows, vc] += o
                st[d, h] = s * decay[kc, :] + both[GLA_STEP:]
        return carry

    lax.fori_loop(0, nstep, body, 0, unroll=2)

    if state_slot is not None:
        @pl.when(n == pl.num_programs(1) - 1)
        def _emit():
            for s in range(state_ref.shape[0]):
                state_ref[s] = st[...] if s == state_slot else jnp.zeros(st.shape, F32)


def _gla(qk, v, gate, init, seq, emit_state):
    n = qk.shape[0]
    nseq = n // seq
    tb = min(seq, 512)
    nsb = seq // tb
    fwd = lambda b, s: (b * nsb + s, 0)
    bwd = lambda b, s: (b * nsb + nsb - 1 - s, 0)
    bwd_gate = lambda b, s: (b * nsb + nsb - 1 - s, 1)
    in_specs = [pl.BlockSpec((tb, 1024), fwd), pl.BlockSpec((tb, 1024), fwd), pl.BlockSpec((tb, 512), fwd),
                pl.BlockSpec((tb, 1024), bwd), pl.BlockSpec((tb, 1024), bwd), pl.BlockSpec((tb, 512), bwd_gate)]
    args = [qk, v, gate, qk, v, gate]
    if init is not None:
        state, il = init
        in_specs.append(pl.BlockSpec((None, None, 2, GLA_HEADS, GLA_DK, GLA_DV),
                                     lambda b, s: (b, il, 0, 0, 0, 0)))
        args.append(state)
    out_specs = [pl.BlockSpec((seq, 1024), lambda b, s: (b, 0))]
    out_shape = [jax.ShapeDtypeStruct((n, 1024), F32)]
    aliases = {}
    if emit_state is not None:
        slot, n_slots, earlier = emit_state
        out_shape.append(jax.ShapeDtypeStruct((nseq, n_slots, 2, GLA_HEADS, GLA_DK, GLA_DV), F32))
        if earlier is None:
            out_specs.append(pl.BlockSpec((None, n_slots, 2, GLA_HEADS, GLA_DK, GLA_DV),
                                          lambda b, s: (b, 0, 0, 0, 0, 0)))
        else:
            out_specs.append(pl.BlockSpec((None, 1, 2, GLA_HEADS, GLA_DK, GLA_DV),
                                          lambda b, s: (b, slot, 0, 0, 0, 0)))
            in_specs.append(pl.BlockSpec(memory_space=pl.ANY))
            args.append(earlier)
            aliases = {len(args) - 1: 1}
    return pl.pallas_call(
        functools.partial(_gla_kernel, tb=tb, nsb=nsb, has_init=init is not None,
                          state_slot=None if emit_state is None else (0 if aliases else emit_state[0]),
                          has_earlier=bool(aliases)),
        grid=(nseq, nsb),
        in_specs=in_specs,
        out_specs=out_specs,
        out_shape=out_shape,
        input_output_aliases=aliases,
        scratch_shapes=[pltpu.VMEM((2, GLA_HEADS, GLA_DK, GLA_DV), F32)],
        compiler_params=_params(2),
        name="gla",
    )(*args)


def _rope_tables(n_lat):
    half = ATT_DH // 2
    inv = 1.0 / (ROPE_BASE ** (jnp.arange(0, half, 2, dtype=F32) / half))
    t = jnp.arange(n_lat)
    rows = (t // GRID_W).astype(F32)
    cols = (t % GRID_W).astype(F32)

    def tab(pos):
        ang = pos[:, None] * inv[None, :]
        ang = jnp.concatenate([ang, ang], axis=-1)
        return jnp.cos(ang), jnp.sin(ang)

    (cr, sr), (cc, sc) = tab(rows), tab(cols)
    sign = jnp.concatenate([-jnp.ones((half // 2,), F32), jnp.ones((half // 2,), F32)])
    cos = jnp.concatenate([cr, cc], axis=-1)
    sin = jnp.concatenate([sr * sign, sc * sign], axis=-1)
    return jnp.tile(cos, (1, 2)), jnp.tile(sin, (1, 2))


def kernel(x_prompt, x_sample, c, cache_attn_k, cache_attn_v, state_gla, c_ctx, norm1_g, norm2_g, w_mod, b_mod, w_in_even, lam_params, subln_g, w_pool, pool_scale, w_out_even, w_in_odd, w_gate_up, b_gate, gla_norm_g, w_out_odd, w_ffn_in, w_ffn_out, final_g):
    bp, tp, _ = x_prompt.shape
    bs, ts, _ = x_sample.shape
    past = cache_attn_k.shape[2]
    n_even = cache_attn_k.shape[1]
    assert bs + 1 <= MOD_ROWS

    xp = x_prompt.reshape(bp * tp, D_MODEL)
    xs = x_sample.reshape(bs * ts, D_MODEL)
    ck = cache_attn_k.reshape(bs, n_even, past, ATT_WIDTH)
    cv = cache_attn_v.reshape(bs, n_even, past, ATT_WIDTH)

    cvec = jnp.concatenate([c_ctx[None, :], c, jnp.zeros((MOD_ROWS - 1 - bs, D_MODEL), F32)], axis=0)
    mod = _adaln(cvec, w_mod, b_mod)
    ctx_group = lambda i: 0

    def lat_group(tm):
        return lambda i: 1 + (i * tm) // ts

    rope = _rope_tables(ts)

    row3 = lambda a: a.reshape(a.shape[0], 1, a.shape[1])
    n1, n2 = row3(norm1_g), row3(norm2_g)
    w_in_even_bf, w_out_even_bf, w_pool_bf = (w.astype(BF16) for w in (w_in_even, w_out_even, w_pool))
    w_in_odd_bf, w_out_odd_bf = w_in_odd.astype(BF16), w_out_odd.astype(BF16)
    w_ffn_in_bf, w_ffn_out_bf = w_ffn_in.astype(BF16), w_ffn_out.astype(BF16)
    subg3, pscale3, gn3 = row3(subln_g), row3(pool_scale), row3(gla_norm_g)
    bg3 = b_gate.reshape(b_gate.shape[0], 1, D_MODEL)

    fg = final_g.reshape(1, D_MODEL)
    k_new = v_new = states = None
    for l in range(DEPTH):
        i = l // 2
        g1, g2 = (n1, l), (n2, l)
        last = fg if l == DEPTH - 1 else None
        pre_p = pre_s = None
        if l % 2 == 0:
            lam_init = 0.8 - 0.6 * math.exp(-0.3 * l)
            w_in, w_out, wpool = (w_in_even_bf, i), (w_out_even_bf, i), (w_pool_bf, i)
            subg, pscale, lamp = (subg3, i), (pscale3, i), (lam_params, i)
            q, k_new, v_new, u = _even_in(xp, mod, l, ctx_group, g1, w_in, None, tp, F32,
                                          cache_out=(i, n_even, k_new, v_new))
            xp = _even_mix(xp, q, k_new, v_new, u, None, mod, l, ctx_group, lamp, subg, wpool, pscale,
                           w_out, tp, lam_init, kv_slot=i)
            q, k, v, u = _even_in(xs, mod, l, lat_group(1024), g1, w_in, rope, ts, BF16)
            xs = _even_mix(xs, q, k, v, u, (ck, cv, i), mod, l, lambda b: 1 + b, lamp, subg, wpool,
                           pscale, w_out, ts, lam_init)
        else:
            w_main, w_out, gn, bg = (w_in_odd_bf, i), (w_out_odd_bf, i), (gn3, i), (bg3, i)
            wgd = jnp.pad(w_in_odd[i][:, 3072:], ((0, 0), (0, 128 - 2 * GATE_RANK))).astype(BF16)
            wup = jnp.zeros((128, D_MODEL), F32)
            wup = wup.at[0:GATE_RANK, 0:512].set(w_gate_up[i, 0])
            wup = wup.at[GATE_RANK:2 * GATE_RANK, 512:1024].set(w_gate_up[i, 1]).astype(BF16)
            qk, v, r, gate = _odd_in(xp, mod, l, ctx_group, g1, w_main, wgd, wup, bg)
            o, states = _gla(qk, v, gate, None, tp, (i, w_in_odd.shape[0], states))
            pre_p = (o, r, gn, w_out)
            qk, v, r, gate = _odd_in(xs, mod, l, lat_group(1024), g1, w_main, wgd, wup, bg)
            (o,) = _gla(qk, v, gate, (state_gla, i), ts, None)
            pre_s = (o, r, gn, w_out)
        wi, wo = (w_ffn_in_bf, l), (w_ffn_out_bf, l)
        xp = _ffn(xp, mod, l, ctx_group, g2, wi, wo, pre_p, last)
        xs = _ffn(xs, mod, l, lat_group(FFN_TM), g2, wi, wo, pre_s, last)

    y_prompt = xp.reshape(bp, tp, D_MODEL)
    y_sample = xs.reshape(bs, ts, D_MODEL)
    new_attn_k = k_new.reshape(bp, n_even, tp, ATT_HEADS, 2, ATT_DH)
    new_attn_v = v_new.reshape(bp, n_even, tp, ATT_HEADS, ATT_DV)
    return (y_prompt, y_sample, new_attn_k, new_attn_v, states)
```

```python
import functools
import math

import jax
import jax.numpy as jnp
from jax import lax
from jax.experimental import pallas as pl
from jax.experimental.pallas import tpu as pltpu

F32 = jnp.float32
BF16 = jnp.bfloat16

D_MODEL = 1024
DEPTH = 4
GRID_W = 64
ATT_WIDTH = 512
POOL_WIDTH = 512
ATT_HEADS = 4
ATT_DH = 64
ATT_DV = 128
ROPE_BASE = 10000.0
POOL_WINDOWS = (2, 4, 8, 16)
POOL_GROUP = 128
POOL_PAD = 8
GLA_HEADS = 4
GLA_DK = 128
GLA_DV = 256
GATE_RANK = 16
GATE_TEMP = 16.0
CHUNK = 64
GLA_STEP = 2 * CHUNK
FFN_HIDDEN = 2816
FFN_CHUNK = 256
FFN_TM = 512
EPS = 1e-6
Q_SCALE = ATT_DH ** -0.5 * math.log2(math.e)
SAFE_SHIFT = 50.0
NORM_SLACK = 1.05
NORM_LANES = 16
MOD_ROWS = 16

V7X_VMEM_LIMIT = 56 * 1024 * 1024


def _params(n_axes, vmem_bytes=V7X_VMEM_LIMIT):
    return pltpu.CompilerParams(dimension_semantics=("arbitrary",) * n_axes,
                                vmem_limit_bytes=vmem_bytes)


def _resident(shape):
    nd = len(shape)
    return pl.BlockSpec(shape, lambda *_: (0,) * nd, pipeline_mode=pl.Buffered(1))


def _layer_of(stack, idx):
    nd = stack.ndim - 1
    return pl.BlockSpec((None,) + stack.shape[1:], lambda *_: (idx,) + (0,) * nd,
                        pipeline_mode=pl.Buffered(1))


def _norm_mod(x, g, shift, scale):
    y = x * lax.rsqrt(jnp.mean(x * x, axis=-1, keepdims=True) + EPS)
    return (y * g) * (1.0 + scale) + shift


def _silu(x):
    return x * jax.nn.sigmoid(x)


def _adaln_kernel(cv_ref, w_ref, b_ref, o_ref):
    s = _silu(cv_ref[...]).astype(BF16)
    o_ref[...] = jnp.dot(s, w_ref[...].astype(BF16), preferred_element_type=F32) + b_ref[...]


def _adaln(cvec, w_mod, b_mod):
    tn = 1536
    out = pl.pallas_call(
        _adaln_kernel,
        grid=(DEPTH, 6 * D_MODEL // tn),
        in_specs=[pl.BlockSpec((MOD_ROWS, D_MODEL), lambda l, j: (0, 0)),
                  pl.BlockSpec((None, D_MODEL, tn), lambda l, j: (l, 0, j)),
                  pl.BlockSpec((None, 1, tn), lambda l, j: (l, 0, j))],
        out_specs=pl.BlockSpec((None, MOD_ROWS, tn), lambda l, j: (l, 0, j)),
        out_shape=jax.ShapeDtypeStruct((DEPTH, MOD_ROWS, 6 * D_MODEL), F32),
        compiler_params=_params(2),
        name="adaln",
    )(cvec, w_mod, b_mod.reshape(DEPTH, 1, 6 * D_MODEL))
    return out.reshape(DEPTH, MOD_ROWS, 6, D_MODEL)


def _mod_spec(layer, group_of_step):
    return pl.BlockSpec((None, None, 6, D_MODEL),
                        lambda *ids: (layer, group_of_step(*ids), 0, 0))


def _even_in_kernel(*refs, tm, rc, rope, kv_slot, seq):
    refs = list(refs)
    x_ref, mod_ref, g_ref, w_ref = (refs.pop(0) for _ in range(4))
    cos_ref, sin_ref = (refs.pop(0), refs.pop(0)) if rope else (None, None)
    q_ref, k_ref, v_ref, u_ref = refs[-4:]
    shift, scale, g = mod_ref[0:1, :], mod_ref[1:2, :], g_ref[...]

    def put_kv(dst, rows, val):
        if kv_slot is None:
            dst[rows, :] = val.astype(dst.dtype)
            return
        s0, s1 = rows.start // seq, rows.stop // seq
        for s in range(dst.shape[1]):
            dst[s0:s1, s] = (val.reshape(s1 - s0, seq, val.shape[-1]) if s == kv_slot
                             else jnp.zeros((s1 - s0, seq, val.shape[-1]), dst.dtype))
    lo = (lax.broadcasted_iota(jnp.int32, (rc, 128), 1) % 32) < 16

    def rotary(y, cos, sin):
        parts = []
        for h in range(ATT_HEADS):
            yh = y[:, h * 128:(h + 1) * 128]
            fwd = pltpu.roll(yh, 128 - 16, axis=1)
            bwd = pltpu.roll(yh, 16, axis=1)
            parts.append(yh * cos + jnp.where(lo, fwd, bwd) * sin)
        return jnp.concatenate(parts, axis=1)

    for r in range(tm // rc):
        rows = slice(r * rc, (r + 1) * rc)
        h = _norm_mod(x_ref[rows, :], g, shift, scale).astype(BF16)
        q = jnp.dot(h, w_ref[:, 0:512], preferred_element_type=F32)
        k = jnp.dot(h, w_ref[:, 512:1024], preferred_element_type=F32)
        if rope:
            cos, sin = cos_ref[rows, :], sin_ref[rows, :]
            q, k = rotary(q, cos, sin), rotary(k, cos, sin)
        q_ref[rows, :] = (q * Q_SCALE).astype(q_ref.dtype)
        put_kv(k_ref, rows, k)
        put_kv(v_ref, rows, jnp.dot(h, w_ref[:, 1024:1536], preferred_element_type=F32))
        u_ref[rows, :] = jnp.dot(h, w_ref[:, 1536:2048], preferred_element_type=F32)


def _even_in(x, mod, layer, group_of_step, g1, w_bf, rope_tabs, seq_len, kv_dtype, cache_out=None):
    n = x.shape[0]
    tm, rc = 1024, 512
    row = lambda i: (i, 0)
    in_specs = [pl.BlockSpec((tm, D_MODEL), row), _mod_spec(layer, group_of_step),
                _layer_of(*g1), _layer_of(*w_bf)]
    args = [x, mod, g1[0], w_bf[0]]
    if rope_tabs is not None:
        per_seq = seq_len // tm
        pos = lambda i: (i % per_seq, 0)
        in_specs += [pl.BlockSpec((tm, 128), pos), pl.BlockSpec((tm, 128), pos)]
        args += list(rope_tabs)
    tile = pl.BlockSpec((tm, 512), row)
    kv_spec, kv_shape, kv_slot, aliases = tile, jax.ShapeDtypeStruct((n, 512), kv_dtype), None, {}
    if cache_out is not None:
        slot, n_slots, ek, ev = cache_out
        spt = tm // seq_len
        kv_shape = jax.ShapeDtypeStruct((n // seq_len, n_slots, seq_len, 512), kv_dtype)
        if ek is None:
            kv_spec, kv_slot = pl.BlockSpec((spt, n_slots, seq_len, 512), lambda i: (i, 0, 0, 0)), slot
        else:
            kv_spec, kv_slot = pl.BlockSpec((spt, 1, seq_len, 512), lambda i: (i, slot, 0, 0)), 0
            in_specs += [pl.BlockSpec(memory_space=pl.ANY)] * 2
            args += [ek, ev]
            aliases = {len(args) - 2: 1, len(args) - 1: 2}
    return pl.pallas_call(
        functools.partial(_even_in_kernel, tm=tm, rc=rc, rope=rope_tabs is not None,
                          kv_slot=kv_slot, seq=seq_len),
        grid=(n // tm,),
        in_specs=in_specs,
        out_specs=[tile, kv_spec, kv_spec, tile],
        out_shape=[jax.ShapeDtypeStruct((n, 512), BF16), kv_shape, kv_shape,
                   jax.ShapeDtypeStruct((n, 512), F32)],
        input_output_aliases=aliases,
        compiler_params=_params(1),
        name="even_in",
    )(*args)


def _even_mix_kernel(*refs, seq, tq, n_cache, lam_init):
    if n_cache:
        (x_ref, q_ref, qseq_ref, k_ref, v_ref, u_ref, ck_ref, cv_ref, mod_ref, pairsum_ref, lamp_ref,
         subg_ref, wpool_ref, pscale_ref, wout_ref, o_ref, upad, bound, bound_ok) = refs
    else:
        (x_ref, q_ref, qseq_ref, k_ref, v_ref, u_ref, mod_ref, pairsum_ref, lamp_ref,
         subg_ref, wpool_ref, pscale_ref, wout_ref, o_ref, upad, bound, bound_ok) = refs
    qi = pl.program_id(1)
    lane = lax.broadcasted_iota(jnp.int32, (1, 128), 1)

    def max_pair_norm2(rows):
        xf = rows.astype(F32)
        n2 = jnp.dot((xf * xf).astype(BF16), pairsum_ref[...], preferred_element_type=F32)
        return jnp.max(n2, axis=0, keepdims=True)

    @pl.when(qi == 0)
    def _per_sequence():
        k2 = max_pair_norm2(k_ref[...])
        if n_cache:
            k2 = jnp.maximum(k2, max_pair_norm2(ck_ref[...]))
        b2 = max_pair_norm2(qseq_ref[...]) * k2 * NORM_SLACK
        bound[0:1, :] = b2
        bound_ok[0] = (jnp.max(b2) <= SAFE_SHIFT * SAFE_SHIFT).astype(jnp.int32)
        zeros = jnp.zeros((POOL_PAD, POOL_WIDTH), F32)
        upad[0:POOL_PAD, :] = zeros
        upad[POOL_PAD + seq:POOL_PAD + seq + POOL_PAD, :] = zeros
        upad[POOL_PAD:POOL_PAD + seq, :] = u_ref[...]

    def pooled_rows():
        base = pl.multiple_of(qi * tq, tq)
        n_blk = tq + 2 * POOL_PAD
        t = lax.broadcasted_iota(jnp.int32, (tq, 1), 0) + qi * tq
        ys = []
        for g, w in enumerate(POOL_WINDOWS):
            cols = slice(g * POOL_GROUP, (g + 1) * POOL_GROUP)
            xg = upad[pl.ds(base, n_blk), cols]
            p = xg + pltpu.roll(xg, 1, axis=0)
            half = 1
            while 2 * half < w:
                p = pltpu.roll(p, half, axis=0) + pltpu.roll(p, n_blk - half, axis=0)
                half *= 2
            cnt = (jnp.minimum(t - w // 2 + w, seq) - jnp.maximum(t - w // 2, 0)).astype(F32)
            ys.append((p[POOL_PAD:POOL_PAD + tq] / cnt - xg[POOL_PAD:POOL_PAD + tq]).astype(BF16))
        return ys

    m1 = jnp.where(lane < ATT_DH, 1.0, 0.0).astype(BF16)
    m2 = jnp.where(lane >= ATT_DH, 1.0, 0.0).astype(BF16)
    nt = (((1,), (1,)), ((), ()))

    def scores(h):
        cols = slice(h * 128, (h + 1) * 128)
        qh = q_ref[:, cols]
        q2 = jnp.concatenate([qh * m1, qh * m2], axis=0)
        s = lax.dot_general(q2, k_ref[:, cols].astype(BF16), nt, preferred_element_type=F32)
        sc = None
        if n_cache:
            sc = lax.dot_general(q2, ck_ref[:, cols].astype(BF16), nt, preferred_element_type=F32)
        return s, sc

    def mix(exact_max):
        lp = lamp_ref[...]
        lam = (jnp.exp(jnp.sum(lp[0:1, :] * lp[1:2, :], axis=-1, keepdims=True))
               - jnp.exp(jnp.sum(lp[2:3, :] * lp[3:4, :], axis=-1, keepdims=True)) + lam_init)
        bound2 = bound[0:1, :]
        heads = []
        pending = scores(0)
        ys = pooled_rows()
        for h in range(ATT_HEADS):
            cols = slice(h * 128, (h + 1) * 128)
            s, sc = pending
            if h + 1 < ATT_HEADS:
                pending = scores(h + 1)
            vh = v_ref[:, cols].astype(BF16)
            if n_cache:
                cvh = cv_ref[:, cols].astype(BF16)
            if exact_max:
                mx = jnp.max(s, axis=-1, keepdims=True)
                if n_cache:
                    mx = jnp.maximum(mx, jnp.max(sc, axis=-1, keepdims=True))
            else:
                l1, l2 = NORM_LANES * 2 * h, NORM_LANES * (2 * h + 1)
                mx = jnp.sqrt(jnp.concatenate([jnp.broadcast_to(bound2[:, l1:l1 + 1], (tq, 1)),
                                               jnp.broadcast_to(bound2[:, l2:l2 + 1], (tq, 1))], axis=0))
            e = jnp.exp2(s - mx)
            den = jnp.sum(e, axis=-1, keepdims=True)
            if n_cache:
                ec = jnp.exp2(sc - mx)
                den = den + jnp.sum(ec, axis=-1, keepdims=True)
            inv1 = 1.0 / den[:tq]
            ratio = lam * den[:tq] / den[tq:]
            a = (e[:tq] - ratio * e[tq:]).astype(BF16)
            o = jnp.dot(a, vh, preferred_element_type=F32)
            if n_cache:
                ac = (ec[:tq] - ratio * ec[tq:]).astype(BF16)
                o = o + jnp.dot(ac, cvh, preferred_element_type=F32)
            o = o * inv1
            o = o * lax.rsqrt(jnp.mean(o * o, axis=-1, keepdims=True) + EPS)
            heads.append((o * subg_ref[:, cols]) * (1.0 - lam_init))
        o_att = jnp.concatenate(heads, axis=1).astype(BF16)

        pooled = [jnp.dot(y, wpool_ref[g], preferred_element_type=F32) for g, y in enumerate(ys)]
        p = (jnp.concatenate(pooled, axis=1) * pscale_ref[...]).astype(BF16)

        out = (jnp.dot(o_att, wout_ref[0:ATT_WIDTH, :], preferred_element_type=F32)
               + jnp.dot(p, wout_ref[ATT_WIDTH:, :], preferred_element_type=F32))
        o_ref[...] = x_ref[...] + mod_ref[2:3, :] * out

    lax.cond(bound_ok[0] == 1, lambda: mix(False), lambda: mix(True))


def _even_mix(x, q, k, v, u, cache, mod, layer, group_of_seq, lamp, subg, wpool_bf, pscale,
              wout_bf, seq, lam_init, kv_slot=None):
    n = x.shape[0]
    nseq = n // seq
    tq = 256
    nq = seq // tq
    blk = lambda b, qi: (b * nq + qi, 0)
    whole = lambda b, qi: (b, 0)
    seq_blk = pl.BlockSpec((seq, 512), whole)
    kv_blk = seq_blk if kv_slot is None else pl.BlockSpec((None, None, seq, 512),
                                                          lambda b, qi: (b, kv_slot, 0, 0))
    in_specs = [pl.BlockSpec((tq, D_MODEL), blk), pl.BlockSpec((tq, 512), blk),
                seq_blk, kv_blk, kv_blk, seq_blk]
    args = [x, q, q, k, v, u]
    n_cache = 0
    if cache is not None:
        ck, cv, il = cache
        n_cache = ck.shape[2]
        cspec = pl.BlockSpec((None, None, n_cache, 512), lambda b, qi: (b, il, 0, 0))
        in_specs += [cspec, cspec]
        args += [ck, cv]
    pairsum = (jnp.arange(ATT_WIDTH)[:, None] // ATT_DH == jnp.arange(128)[None, :] // NORM_LANES).astype(BF16)
    weights = [lamp, subg, wpool_bf, pscale, wout_bf]
    in_specs += ([_mod_spec(layer, lambda b, qi: group_of_seq(b)), _resident((ATT_WIDTH, 128))]
                 + [_layer_of(*w) for w in weights])
    args += [mod, pairsum] + [w[0] for w in weights]
    return pl.pallas_call(
        functools.partial(_even_mix_kernel, seq=seq, tq=tq, n_cache=n_cache, lam_init=lam_init),
        grid=(nseq, nq),
        in_specs=in_specs,
        out_specs=pl.BlockSpec((tq, D_MODEL), blk),
        out_shape=jax.ShapeDtypeStruct((n, D_MODEL), F32),
        scratch_shapes=[pltpu.VMEM((seq + 2 * POOL_PAD, POOL_WIDTH), F32),
                        pltpu.VMEM((8, 128), F32),
                        pltpu.SMEM((1,), jnp.int32)],
        compiler_params=_params(2),
        name="even_mix",
    )(*args)


def _ffn_kernel(*refs, gla_pre, final):
    refs = list(refs)
    x_ref, mod_ref = refs.pop(0), refs.pop(0)
    if gla_pre:
        og_ref, r_ref, gn_ref, wmix_ref = (refs.pop(0) for _ in range(4))
    g_ref, wi_ref, wo_ref = (refs.pop(0) for _ in range(3))
    fg_ref = refs.pop(0) if final else None
    o_ref, acc_ref = refs
    x = x_ref[...]
    if gla_pre:
        parts = []
        for h in range(GLA_HEADS):
            cols = slice(h * GLA_DV, (h + 1) * GLA_DV)
            oh = og_ref[:, cols]
            oh = oh * lax.rsqrt(jnp.mean(oh * oh, axis=-1, keepdims=True) + EPS)
            parts.append(oh * gn_ref[:, cols])
        z = (jnp.concatenate(parts, axis=1) * _silu(r_ref[...])).astype(BF16)
        x = x + mod_ref[2:3, :] * jnp.dot(z, wmix_ref[...], preferred_element_type=F32)
    h = _norm_mod(x, g_ref[...], mod_ref[3:4, :], mod_ref[4:5, :]).astype(BF16)
    for c in range(FFN_HIDDEN // FFN_CHUNK):
        lo, hi = c * FFN_CHUNK, (c + 1) * FFN_CHUNK
        a = jnp.dot(h, wi_ref[:, lo:hi], preferred_element_type=F32)
        b = jnp.dot(h, wi_ref[:, FFN_HIDDEN + lo:FFN_HIDDEN + hi], preferred_element_type=F32)
        z = (_silu(a) * b).astype(BF16)
        part = jnp.dot(z, wo_ref[lo:hi, :], preferred_element_type=F32)
        if c == 0:
            acc_ref[...] = part
        else:
            acc_ref[...] += part
    y = x + mod_ref[5:6, :] * acc_ref[...]
    if final:
        y = (y * lax.rsqrt(jnp.mean(y * y, axis=-1, keepdims=True) + EPS)) * fg_ref[...]
    o_ref[...] = y


def _ffn(x, mod, layer, group_of_step, g2, wi_bf, wo_bf, gla_pre=None, final_g=None):
    n = x.shape[0]
    tm = FFN_TM
    row = lambda i: (i, 0)
    tile = pl.BlockSpec((tm, D_MODEL), row)
    in_specs = [tile, _mod_spec(layer, group_of_step)]
    args = [x, mod]
    if gla_pre is not None:
        o, r, gn, wmix_bf = gla_pre
        in_specs += [tile, tile, _layer_of(*gn), _layer_of(*wmix_bf)]
        args += [o, r, gn[0], wmix_bf[0]]
    in_specs += [_layer_of(*g2), _layer_of(*wi_bf), _layer_of(*wo_bf)]
    args += [g2[0], wi_bf[0], wo_bf[0]]
    if final_g is not None:
        in_specs.append(_resident((1, D_MODEL)))
        args.append(final_g)
    return pl.pallas_call(
        functools.partial(_ffn_kernel, gla_pre=gla_pre is not None, final=final_g is not None),
        grid=(n // tm,),
        in_specs=in_specs,
        out_specs=tile,
        out_shape=jax.ShapeDtypeStruct((n, D_MODEL), F32),
        scratch_shapes=[pltpu.VMEM((tm, D_MODEL), F32)],
        compiler_params=_params(1),
        name="ffn",
    )(*args)


def _odd_in_kernel(x_ref, mod_ref, g_ref, w_ref, wgd_ref, wup_ref, bg_ref,
                   qk_ref, v_ref, r_ref, gate_ref, *, tm, rc):
    shift, scale, g = mod_ref[0:1, :], mod_ref[1:2, :], g_ref[...]
    for c in range(tm // rc):
        rows = slice(c * rc, (c + 1) * rc)
        h = _norm_mod(x_ref[rows, :], g, shift, scale).astype(BF16)
        gd = jnp.dot(h, wgd_ref[...], preferred_element_type=F32).astype(BF16)
        logit = jnp.dot(gd, wup_ref[...], preferred_element_type=F32) + bg_ref[...]
        qk_ref[rows, :] = jnp.dot(h, w_ref[:, 0:1024], preferred_element_type=F32)
        v_ref[rows, :] = jnp.dot(h, w_ref[:, 1024:2048], preferred_element_type=F32).astype(BF16)
        r_ref[rows, :] = jnp.dot(h, w_ref[:, 2048:3072], preferred_element_type=F32)
        log_sig = jnp.minimum(logit, 0.0) - jnp.log(1.0 + jnp.exp(-jnp.abs(logit)))
        gate_ref[rows, :] = log_sig / GATE_TEMP


def _odd_in(x, mod, layer, group_of_step, g1, w_bf, wgd_bf, wup_bf, bg):
    n = x.shape[0]
    tm, rc = 1024, 512
    row = lambda i: (i, 0)
    return pl.pallas_call(
        functools.partial(_odd_in_kernel, tm=tm, rc=rc),
        grid=(n // tm,),
        in_specs=[pl.BlockSpec((tm, D_MODEL), row), _mod_spec(layer, group_of_step),
                  _layer_of(*g1), _layer_of(*w_bf),
                  _resident((D_MODEL, 128)), _resident((128, D_MODEL)), _layer_of(*bg)],
        out_specs=[pl.BlockSpec((tm, D_MODEL), row)] * 4,
        out_shape=[jax.ShapeDtypeStruct((n, D_MODEL), dt) for dt in (F32, BF16, F32, F32)],
        compiler_params=_params(1),
        name="odd_in",
    )(x, mod, g1[0], w_bf[0], wgd_bf, wup_bf, bg[0])


def _chunk_scan(x, reverse):
    n = x.shape[0]
    pos = lax.broadcasted_iota(jnp.int32, x.shape, 0) % CHUNK
    step = 1
    while step < CHUNK:
        if reverse:
            x = x + jnp.where(pos < CHUNK - step, pltpu.roll(x, n - step, axis=0), 0.0)
        else:
            x = x + jnp.where(pos >= step, pltpu.roll(x, step, axis=0), 0.0)
        step *= 2
    return x


def _gla_kernel(*refs, tb, nsb, has_init, state_slot, has_earlier):
    refs = list(refs)
    ins = [refs.pop(0) for _ in range(6)]
    init_ref = refs.pop(0) if has_init else None
    if has_earlier:
        refs.pop(0)
    o_ref = refs.pop(0)
    state_ref = refs.pop(0) if state_slot is not None else None
    st = refs.pop(0)
    n = pl.program_id(1)

    @pl.when(n == 0)
    def _init():
        o_ref[...] = jnp.zeros(o_ref.shape, F32)
        if has_init:
            st[...] = init_ref[...]
        else:
            st[...] = jnp.zeros(st.shape, F32)

    ri = lax.broadcasted_iota(jnp.int32, (GLA_STEP, GLA_STEP), 0)
    ci = lax.broadcasted_iota(jnp.int32, (GLA_STEP, GLA_STEP), 1)
    first = lax.broadcasted_iota(jnp.int32, (GLA_STEP, 512), 0) < CHUNK
    nt = (((1,), (1,)), ((), ()))
    nstep = tb // GLA_STEP

    def body(i, carry):
        for d in range(2):
            qk_ref, v_ref, g_ref = ins[3 * d:3 * d + 3]
            c = i if d == 0 else nstep - 1 - i
            block = n if d == 0 else nsb - 1 - n
            rows = pl.ds(pl.multiple_of(c * GLA_STEP, GLA_STEP), GLA_STEP)
            orows = pl.ds(pl.multiple_of(block * tb + c * GLA_STEP, GLA_STEP), GLA_STEP)
            p = _chunk_scan(g_ref[rows, :], reverse=d == 1)
            if d == 0:
                mask = ci <= ri
                tot_in, tot_out = p[CHUNK - 1:CHUNK, :], p[GLA_STEP - 1:GLA_STEP, :]
                b_mid = jnp.where(first, p - tot_in, p)
            else:
                mask = ci >= ri
                tot_out, tot_in = p[0:1, :], p[CHUNK:CHUNK + 1, :]
                b_mid = jnp.where(first, p, p - tot_in)
            tot = tot_in + tot_out
            q = qk_ref[rows, 0:512] * (GLA_DK ** -0.5)
            k = qk_ref[rows, 512:1024]
            q_mid = q * jnp.exp(b_mid)
            k_mid = k * jnp.exp(-b_mid)
            q_t = q_mid.astype(BF16)
            k_t = k_mid.astype(BF16)
            q_in = (q_mid * jnp.exp(tot_in)).astype(BF16)
            k_end = k_mid * jnp.exp(tot_out)
            decay = jnp.exp(jnp.broadcast_to(tot, (8, 512)).T[:, 0:1])
            for h in range(GLA_HEADS):
                kc = slice(h * GLA_DK, (h + 1) * GLA_DK)
                vc = slice(h * GLA_DV, (h + 1) * GLA_DV)
                vh = v_ref[rows, vc]
                a = lax.dot_general(q_t[:, kc], k_t[:, kc], nt, preferred_element_type=F32)
                a = jnp.where(mask, a, 0.0).astype(BF16)
                lhs = jnp.concatenate([a, k_end[:, kc].T.astype(BF16)], axis=0)
                both = jnp.dot(lhs, vh, preferred_element_type=F32)
                s = st[d, h]
                o = both[:GLA_STEP] + jnp.dot(q_in[:, kc], s.astype(BF16), preferred_element_type=F32)
                o_ref[orows, vc] += o
                st[d, h] = s * decay[kc, :] + both[GLA_STEP:]
        return carry

    lax.fori_loop(0, nstep, body, 0, unroll=2)

    if state_slot is not None:
        @pl.when(n == pl.num_programs(1) - 1)
        def _emit():
            for s in range(state_ref.shape[0]):
                state_ref[s] = st[...] if s == state_slot else jnp.zeros(st.shape, F32)


def _gla(qk, v, gate, init, seq, emit_state):
    n = qk.shape[0]
    nseq = n // seq
    tb = min(seq, 512)
    nsb = seq // tb
    fwd = lambda b, s: (b * nsb + s, 0)
    bwd = lambda b, s: (b * nsb + nsb - 1 - s, 0)
    bwd_gate = lambda b, s: (b * nsb + nsb - 1 - s, 1)
    in_specs = [pl.BlockSpec((tb, 1024), fwd), pl.BlockSpec((tb, 1024), fwd), pl.BlockSpec((tb, 512), fwd),
                pl.BlockSpec((tb, 1024), bwd), pl.BlockSpec((tb, 1024), bwd), pl.BlockSpec((tb, 512), bwd_gate)]
    args = [qk, v, gate, qk, v, gate]
    if init is not None:
        state, il = init
        in_specs.append(pl.BlockSpec((None, None, 2, GLA_HEADS, GLA_DK, GLA_DV),
                                     lambda b, s: (b, il, 0, 0, 0, 0)))
        args.append(state)
    out_specs = [pl.BlockSpec((seq, 1024), lambda b, s: (b, 0))]
    out_shape = [jax.ShapeDtypeStruct((n, 1024), F32)]
    aliases = {}
    if emit_state is not None:
        slot, n_slots, earlier = emit_state
        out_shape.append(jax.ShapeDtypeStruct((nseq, n_slots, 2, GLA_HEADS, GLA_DK, GLA_DV), F32))
        if earlier is None:
            out_specs.append(pl.BlockSpec((None, n_slots, 2, GLA_HEADS, GLA_DK, GLA_DV),
                                          lambda b, s: (b, 0, 0, 0, 0, 0)))
        else:
            out_specs.append(pl.BlockSpec((None, 1, 2, GLA_HEADS, GLA_DK, GLA_DV),
                                          lambda b, s: (b, slot, 0, 0, 0, 0)))
            in_specs.append(pl.BlockSpec(memory_space=pl.ANY))
            args.append(earlier)
            aliases = {len(args) - 1: 1}
    return pl.pallas_call(
        functools.partial(_gla_kernel, tb=tb, nsb=nsb, has_init=init is not None,
                          state_slot=None if emit_state is None else (0 if aliases else emit_state[0]),
                          has_earlier=bool(aliases)),
        grid=(nseq, nsb),
        in_specs=in_specs,
        out_specs=out_specs,
        out_shape=out_shape,
        input_output_aliases=aliases,
        scratch_shapes=[pltpu.VMEM((2, GLA_HEADS, GLA_DK, GLA_DV), F32)],
        compiler_params=_params(2),
        name="gla",
    )(*args)


def _rope_tables(n_lat):
    half = ATT_DH // 2
    inv = 1.0 / (ROPE_BASE ** (jnp.arange(0, half, 2, dtype=F32) / half))
    t = jnp.arange(n_lat)
    rows = (t // GRID_W).astype(F32)
    cols = (t % GRID_W).astype(F32)

    def tab(pos):
        ang = pos[:, None] * inv[None, :]
        ang = jnp.concatenate([ang, ang], axis=-1)
        return jnp.cos(ang), jnp.sin(ang)

    (cr, sr), (cc, sc) = tab(rows), tab(cols)
    sign = jnp.concatenate([-jnp.ones((half // 2,), F32), jnp.ones((half // 2,), F32)])
    cos = jnp.concatenate([cr, cc], axis=-1)
    sin = jnp.concatenate([sr * sign, sc * sign], axis=-1)
    return jnp.tile(cos, (1, 2)), jnp.tile(sin, (1, 2))


def kernel(x_prompt, x_sample, c, cache_attn_k, cache_attn_v, state_gla, c_ctx, norm1_g, norm2_g, w_mod, b_mod, w_in_even, lam_params, subln_g, w_pool, pool_scale, w_out_even, w_in_odd, w_gate_up, b_gate, gla_norm_g, w_out_odd, w_ffn_in, w_ffn_out, final_g):
    bp, tp, _ = x_prompt.shape
    bs, ts, _ = x_sample.shape
    past = cache_attn_k.shape[2]
    n_even = cache_attn_k.shape[1]
    assert bs + 1 <= MOD_ROWS

    xp = x_prompt.reshape(bp * tp, D_MODEL)
    xs = x_sample.reshape(bs * ts, D_MODEL)
    ck = cache_attn_k.reshape(bs, n_even, past, ATT_WIDTH)
    cv = cache_attn_v.reshape(bs, n_even, past, ATT_WIDTH)

    cvec = jnp.concatenate([c_ctx[None, :], c, jnp.zeros((MOD_ROWS - 1 - bs, D_MODEL), F32)], axis=0)
    mod = _adaln(cvec, w_mod, b_mod)
    ctx_group = lambda i: 0

    def lat_group(tm):
        return lambda i: 1 + (i * tm) // ts

    rope = _rope_tables(ts)

    row3 = lambda a: a.reshape(a.shape[0], 1, a.shape[1])
    n1, n2 = row3(norm1_g), row3(norm2_g)
    w_in_even_bf, w_out_even_bf, w_pool_bf = (w.astype(BF16) for w in (w_in_even, w_out_even, w_pool))
    w_in_odd_bf, w_out_odd_bf = w_in_odd.astype(BF16), w_out_odd.astype(BF16)
    w_ffn_in_bf, w_ffn_out_bf = w_ffn_in.astype(BF16), w_ffn_out.astype(BF16)
    subg3, pscale3, gn3 = row3(subln_g), row3(pool_scale), row3(gla_norm_g)
    bg3 = b_gate.reshape(b_gate.shape[0], 1, D_MODEL)

    fg = final_g.reshape(1, D_MODEL)
    k_new = v_new = states = None
    for l in range(DEPTH):
        i = l // 2
        g1, g2 = (n1, l), (n2, l)
        last = fg if l == DEPTH - 1 else None
        pre_p = pre_s = None
        if l % 2 == 0:
            lam_init = 0.8 - 0.6 * math.exp(-0.3 * l)
            w_in, w_out, wpool = (w_in_even_bf, i), (w_out_even_bf, i), (w_pool_bf, i)
            subg, pscale, lamp = (subg3, i), (pscale3, i), (lam_params, i)
            q, k_new, v_new, u = _even_in(xp, mod, l, ctx_group, g1, w_in, None, tp, F32,
                                          cache_out=(i, n_even, k_new, v_new))
            xp = _even_mix(xp, q, k_new, v_new, u, None, mod, l, ctx_group, lamp, subg, wpool, pscale,
                           w_out, tp, lam_init, kv_slot=i)
            q, k, v, u = _even_in(xs, mod, l, lat_group(1024), g1, w_in, rope, ts, BF16)
            xs = _even_mix(xs, q, k, v, u, (ck, cv, i), mod, l, lambda b: 1 + b, lamp, subg, wpool,
                           pscale, w_out, ts, lam_init)
        else:
            w_main, w_out, gn, bg = (w_in_odd_bf, i), (w_out_odd_bf, i), (gn3, i), (bg3, i)
            wgd = jnp.pad(w_in_odd[i][:, 3072:], ((0, 0), (0, 128 - 2 * GATE_RANK))).astype(BF16)
            wup = jnp.zeros((128, D_MODEL), F32)
            wup = wup.at[0:GATE_RANK, 0:512].set(w_gate_up[i, 0])
            wup = wup.at[GATE_RANK:2 * GATE_RANK, 512:1024].set(w_gate_up[i, 1]).astype(BF16)
            qk, v, r, gate = _odd_in(xp, mod, l, ctx_group, g1, w_main, wgd, wup, bg)
            o, states = _gla(qk, v, gate, None, tp, (i, w_in_odd.shape[0], states))
            pre_p = (o, r, gn, w_out)
            qk, v, r, gate = _odd_in(xs, mod, l, lat_group(1024), g1, w_main, wgd, wup, bg)
            (o,) = _gla(qk, v, gate, (state_gla, i), ts, None)
            pre_s = (o, r, gn, w_out)
        wi, wo = (w_ffn_in_bf, l), (w_ffn_out_bf, l)
        xp = _ffn(xp, mod, l, ctx_group, g2, wi, wo, pre_p, last)
        xs = _ffn(xs, mod, l, lat_group(FFN_TM), g2, wi, wo, pre_s, last)

    y_prompt = xp.reshape(bp, tp, D_MODEL)
    y_sample = xs.reshape(bs, ts, D_MODEL)
    new_attn_k = k_new.reshape(bp, n_even, tp, ATT_HEADS, 2, ATT_DH)
    new_attn_v = v_new.reshape(bp, n_even, tp, ATT_HEADS, ATT_DV)
    return (y_prompt, y_sample, new_attn_k, new_attn_v, states)
```

```python
import functools
import math

import jax
import jax.numpy as jnp
from jax import lax
from jax.experimental import pallas as pl
from jax.experimental.pallas import tpu as pltpu

F32 = jnp.float32
BF16 = jnp.bfloat16

D_MODEL = 1024
DEPTH = 4
GRID_W = 64
ATT_WIDTH = 512
POOL_WIDTH = 512
ATT_HEADS = 4
ATT_DH = 64
ATT_DV = 128
ROPE_BASE = 10000.0
POOL_WINDOWS = (2, 4, 8, 16)
POOL_GROUP = 128
POOL_PAD = 8
GLA_HEADS = 4
GLA_DK = 128
GLA_DV = 256
GATE_RANK = 16
GATE_TEMP = 16.0
CHUNK = 64
GLA_STEP = 2 * CHUNK
FFN_HIDDEN = 2816
FFN_CHUNK = 256
FFN_TM = 512
EPS = 1e-6
Q_SCALE = ATT_DH ** -0.5 * math.log2(math.e)
SAFE_SHIFT = 50.0
NORM_SLACK = 1.05
NORM_LANES = 16
MOD_ROWS = 16

V7X_VMEM_LIMIT = 56 * 1024 * 1024


def _params(n_axes, vmem_bytes=V7X_VMEM_LIMIT):
    return pltpu.CompilerParams(dimension_semantics=("arbitrary",) * n_axes,
                                vmem_limit_bytes=vmem_bytes)


def _resident(shape):
    nd = len(shape)
    return pl.BlockSpec(shape, lambda *_: (0,) * nd, pipeline_mode=pl.Buffered(1))


def _layer_of(stack, idx):
    nd = stack.ndim - 1
    return pl.BlockSpec((None,) + stack.shape[1:], lambda *_: (idx,) + (0,) * nd,
                        pipeline_mode=pl.Buffered(1))


def _norm_mod(x, g, shift, scale):
    y = x * lax.rsqrt(jnp.mean(x * x, axis=-1, keepdims=True) + EPS)
    return (y * g) * (1.0 + scale) + shift


def _silu(x):
    return x * jax.nn.sigmoid(x)


def _adaln_kernel(cv_ref, w_ref, b_ref, o_ref):
    s = _silu(cv_ref[...]).astype(BF16)
    o_ref[...] = jnp.dot(s, w_ref[...].astype(BF16), preferred_element_type=F32) + b_ref[...]


def _adaln(cvec, w_mod, b_mod):
    tn = 1536
    out = pl.pallas_call(
        _adaln_kernel,
        grid=(DEPTH, 6 * D_MODEL // tn),
        in_specs=[pl.BlockSpec((MOD_ROWS, D_MODEL), lambda l, j: (0, 0)),
                  pl.BlockSpec((None, D_MODEL, tn), lambda l, j: (l, 0, j)),
                  pl.BlockSpec((None, 1, tn), lambda l, j: (l, 0, j))],
        out_specs=pl.BlockSpec((None, MOD_ROWS, tn), lambda l, j: (l, 0, j)),
        out_shape=jax.ShapeDtypeStruct((DEPTH, MOD_ROWS, 6 * D_MODEL), F32),
        compiler_params=_params(2),
        name="adaln",
    )(cvec, w_mod, b_mod.reshape(DEPTH, 1, 6 * D_MODEL))
    return out.reshape(DEPTH, MOD_ROWS, 6, D_MODEL)


def _mod_spec(layer, group_of_step):
    return pl.BlockSpec((None, None, 6, D_MODEL),
                        lambda *ids: (layer, group_of_step(*ids), 0, 0))


def _even_in_kernel(*refs, tm, rc, rope, kv_slot, seq):
    refs = list(refs)
    x_ref, mod_ref, g_ref, w_ref = (refs.pop(0) for _ in range(4))
    cos_ref, sin_ref = (refs.pop(0), refs.pop(0)) if rope else (None, None)
    q_ref, k_ref, v_ref, u_ref = refs[-4:]
    shift, scale, g = mod_ref[0:1, :], mod_ref[1:2, :], g_ref[...]

    def put_kv(dst, rows, val):
        if kv_slot is None:
            dst[rows, :] = val.astype(dst.dtype)
            return
        s0, s1 = rows.start // seq, rows.stop // seq
        for s in range(dst.shape[1]):
            dst[s0:s1, s] = (val.reshape(s1 - s0, seq, val.shape[-1]) if s == kv_slot
                             else jnp.zeros((s1 - s0, seq, val.shape[-1]), dst.dtype))
    lo = (lax.broadcasted_iota(jnp.int32, (rc, 128), 1) % 32) < 16

    def rotary(y, cos, sin):
        parts = []
        for h in range(ATT_HEADS):
            yh = y[:, h * 128:(h + 1) * 128]
            fwd = pltpu.roll(yh, 128 - 16, axis=1)
            bwd = pltpu.roll(yh, 16, axis=1)
            parts.append(yh * cos + jnp.where(lo, fwd, bwd) * sin)
        return jnp.concatenate(parts, axis=1)

    for r in range(tm // rc):
        rows = slice(r * rc, (r + 1) * rc)
        h = _norm_mod(x_ref[rows, :], g, shift, scale).astype(BF16)
        q = jnp.dot(h, w_ref[:, 0:512], preferred_element_type=F32)
        k = jnp.dot(h, w_ref[:, 512:1024], preferred_element_type=F32)
        if rope:
            cos, sin = cos_ref[rows, :], sin_ref[rows, :]
            q, k = rotary(q, cos, sin), rotary(k, cos, sin)
        q_ref[rows, :] = (q * Q_SCALE).astype(q_ref.dtype)
        put_kv(k_ref, rows, k)
        put_kv(v_ref, rows, jnp.dot(h, w_ref[:, 1024:1536], preferred_element_type=F32))
        u_ref[rows, :] = jnp.dot(h, w_ref[:, 1536:2048], preferred_element_type=F32)


def _even_in(x, mod, layer, group_of_step, g1, w_bf, rope_tabs, seq_len, kv_dtype, cache_out=None):
    n = x.shape[0]
    tm, rc = 1024, 512
    row = lambda i: (i, 0)
    in_specs = [pl.BlockSpec((tm, D_MODEL), row), _mod_spec(layer, group_of_step),
                _layer_of(*g1), _layer_of(*w_bf)]
    args = [x, mod, g1[0], w_bf[0]]
    if rope_tabs is not None:
        per_seq = seq_len // tm
        pos = lambda i: (i % per_seq, 0)
        in_specs += [pl.BlockSpec((tm, 128), pos), pl.BlockSpec((tm, 128), pos)]
        args += list(rope_tabs)
    tile = pl.BlockSpec((tm, 512), row)
    kv_spec, kv_shape, kv_slot, aliases = tile, jax.ShapeDtypeStruct((n, 512), kv_dtype), None, {}
    if cache_out is not None:
        slot, n_slots, ek, ev = cache_out
        spt = tm // seq_len
        kv_shape = jax.ShapeDtypeStruct((n // seq_len, n_slots, seq_len, 512), kv_dtype)
        if ek is None:
            kv_spec, kv_slot = pl.BlockSpec((spt, n_slots, seq_len, 512), lambda i: (i, 0, 0, 0)), slot
        else:
            kv_spec, kv_slot = pl.BlockSpec((spt, 1, seq_len, 512), lambda i: (i, slot, 0, 0)), 0
            in_specs += [pl.BlockSpec(memory_space=pl.ANY)] * 2
            args += [ek, ev]
            aliases = {len(args) - 2: 1, len(args) - 1: 2}
    return pl.pallas_call(
        functools.partial(_even_in_kernel, tm=tm, rc=rc, rope=rope_tabs is not None,
                          kv_slot=kv_slot, seq=seq_len),
        grid=(n // tm,),
        in_specs=in_specs,
        out_specs=[tile, kv_spec, kv_spec, tile],
        out_shape=[jax.ShapeDtypeStruct((n, 512), BF16), kv_shape, kv_shape,
                   jax.ShapeDtypeStruct((n, 512), F32)],
        input_output_aliases=aliases,
        compiler_params=_params(1),
        name="even_in",
    )(*args)


def _even_mix_kernel(*refs, seq, tq, n_cache, lam_init):
    if n_cache:
        (x_ref, q_ref, qseq_ref, k_ref, v_ref, u_ref, ck_ref, cv_ref, mod_ref, pairsum_ref, lamp_ref,
         subg_ref, wpool_ref, pscale_ref, wout_ref, o_ref, upad, bound, bound_ok) = refs
    else:
        (x_ref, q_ref, qseq_ref, k_ref, v_ref, u_ref, mod_ref, pairsum_ref, lamp_ref,
         subg_ref, wpool_ref, pscale_ref, wout_ref, o_ref, upad, bound, bound_ok) = refs
    qi = pl.program_id(1)
    lane = lax.broadcasted_iota(jnp.int32, (1, 128), 1)

    def max_pair_norm2(rows):
        xf = rows.astype(F32)
        n2 = jnp.dot((xf * xf).astype(BF16), pairsum_ref[...], preferred_element_type=F32)
        return jnp.max(n2, axis=0, keepdims=True)

    @pl.when(qi == 0)
    def _per_sequence():
        k2 = max_pair_norm2(k_ref[...])
        if n_cache:
            k2 = jnp.maximum(k2, max_pair_norm2(ck_ref[...]))
        b2 = max_pair_norm2(qseq_ref[...]) * k2 * NORM_SLACK
        bound[0:1, :] = b2
        bound_ok[0] = (jnp.max(b2) <= SAFE_SHIFT * SAFE_SHIFT).astype(jnp.int32)
        zeros = jnp.zeros((POOL_PAD, POOL_WIDTH), F32)
        upad[0:POOL_PAD, :] = zeros
        upad[POOL_PAD + seq:POOL_PAD + seq + POOL_PAD, :] = zeros
        upad[POOL_PAD:POOL_PAD + seq, :] = u_ref[...]

    def pooled_rows():
        base = pl.multiple_of(qi * tq, tq)
        n_blk = tq + 2 * POOL_PAD
        t = lax.broadcasted_iota(jnp.int32, (tq, 1), 0) + qi * tq
        ys = []
        for g, w in enumerate(POOL_WINDOWS):
            cols = slice(g * POOL_GROUP, (g + 1) * POOL_GROUP)
            xg = upad[pl.ds(base, n_blk), cols]
            p = xg + pltpu.roll(xg, 1, axis=0)
            half = 1
            while 2 * half < w:
                p = pltpu.roll(p, half, axis=0) + pltpu.roll(p, n_blk - half, axis=0)
                half *= 2
            cnt = (jnp.minimum(t - w // 2 + w, seq) - jnp.maximum(t - w // 2, 0)).astype(F32)
            ys.append((p[POOL_PAD:POOL_PAD + tq] / cnt - xg[POOL_PAD:POOL_PAD + tq]).astype(BF16))
        return ys

    m1 = jnp.where(lane < ATT_DH, 1.0, 0.0).astype(BF16)
    m2 = jnp.where(lane >= ATT_DH, 1.0, 0.0).astype(BF16)
    nt = (((1,), (1,)), ((), ()))

    def scores(h):
        cols = slice(h * 128, (h + 1) * 128)
        qh = q_ref[:, cols]
        q2 = jnp.concatenate([qh * m1, qh * m2], axis=0)
        s = lax.dot_general(q2, k_ref[:, cols].astype(BF16), nt, preferred_element_type=F32)
        sc = None
        if n_cache:
            sc = lax.dot_general(q2, ck_ref[:, cols].astype(BF16), nt, preferred_element_type=F32)
        return s, sc

    def mix(exact_max):
        lp = lamp_ref[...]
        lam = (jnp.exp(jnp.sum(lp[0:1, :] * lp[1:2, :], axis=-1, keepdims=True))
               - jnp.exp(jnp.sum(lp[2:3, :] * lp[3:4, :], axis=-1, keepdims=True)) + lam_init)
        bound2 = bound[0:1, :]
        heads = []
        pending = scores(0)
        ys = pooled_rows()
        for h in range(ATT_HEADS):
            cols = slice(h * 128, (h + 1) * 128)
            s, sc = pending
            if h + 1 < ATT_HEADS:
                pending = scores(h + 1)
            vh = v_ref[:, cols].astype(BF16)
            if n_cache:
                cvh = cv_ref[:, cols].astype(BF16)
            if exact_max:
                mx = jnp.max(s, axis=-1, keepdims=True)
                if n_cache:
                    mx = jnp.maximum(mx, jnp.max(sc, axis=-1, keepdims=True))
            else:
                l1, l2 = NORM_LANES * 2 * h, NORM_LANES * (2 * h + 1)
                mx = jnp.sqrt(jnp.concatenate([jnp.broadcast_to(bound2[:, l1:l1 + 1], (tq, 1)),
                                               jnp.broadcast_to(bound2[:, l2:l2 + 1], (tq, 1))], axis=0))
            e = jnp.exp2(s - mx)
            den = jnp.sum(e, axis=-1, keepdims=True)
            if n_cache:
                ec = jnp.exp2(sc - mx)
                den = den + jnp.sum(ec, axis=-1, keepdims=True)
            inv1 = 1.0 / den[:tq]
            ratio = lam * den[:tq] / den[tq:]
            a = (e[:tq] - ratio * e[tq:]).astype(BF16)
            o = jnp.dot(a, vh, preferred_element_type=F32)
            if n_cache:
                ac = (ec[:tq] - ratio * ec[tq:]).astype(BF16)
                o = o + jnp.dot(ac, cvh, preferred_element_type=F32)
            o = o * inv1
            o = o * lax.rsqrt(jnp.mean(o * o, axis=-1, keepdims=True) + EPS)
            heads.append((o * subg_ref[:, cols]) * (1.0 - lam_init))
        o_att = jnp.concatenate(heads, axis=1).astype(BF16)

        pooled = [jnp.dot(y, wpool_ref[g], preferred_element_type=F32) for g, y in enumerate(ys)]
        p = (jnp.concatenate(pooled, axis=1) * pscale_ref[...]).astype(BF16)

        out = (jnp.dot(o_att, wout_ref[0:ATT_WIDTH, :], preferred_element_type=F32)
               + jnp.dot(p, wout_ref[ATT_WIDTH:, :], preferred_element_type=F32))
        o_ref[...] = x_ref[...] + mod_ref[2:3, :] * out

    lax.cond(bound_ok[0] == 1, lambda: mix(False), lambda: mix(True))


def _even_mix(x, q, k, v, u, cache, mod, layer, group_of_seq, lamp, subg, wpool_bf, pscale,
              wout_bf, seq, lam_init, kv_slot=None):
    n = x.shape[0]
    nseq = n // seq
    tq = 256
    nq = seq // tq
    blk = lambda b, qi: (b * nq + qi, 0)
    whole = lambda b, qi: (b, 0)
    seq_blk = pl.BlockSpec((seq, 512), whole)
    kv_blk = seq_blk if kv_slot is None else pl.BlockSpec((None, None, seq, 512),
                                                          lambda b, qi: (b, kv_slot, 0, 0))
    in_specs = [pl.BlockSpec((tq, D_MODEL), blk), pl.BlockSpec((tq, 512), blk),
                seq_blk, kv_blk, kv_blk, seq_blk]
    args = [x, q, q, k, v, u]
    n_cache = 0
    if cache is not None:
        ck, cv, il = cache
        n_cache = ck.shape[2]
        cspec = pl.BlockSpec((None, None, n_cache, 512), lambda b, qi: (b, il, 0, 0))
        in_specs += [cspec, cspec]
        args += [ck, cv]
    pairsum = (jnp.arange(ATT_WIDTH)[:, None] // ATT_DH == jnp.arange(128)[None, :] // NORM_LANES).astype(BF16)
    weights = [lamp, subg, wpool_bf, pscale, wout_bf]
    in_specs += ([_mod_spec(layer, lambda b, qi: group_of_seq(b)), _resident((ATT_WIDTH, 128))]
                 + [_layer_of(*w) for w in weights])
    args += [mod, pairsum] + [w[0] for w in weights]
    return pl.pallas_call(
        functools.partial(_even_mix_kernel, seq=seq, tq=tq, n_cache=n_cache, lam_init=lam_init),
        grid=(nseq, nq),
        in_specs=in_specs,
        out_specs=pl.BlockSpec((tq, D_MODEL), blk),
        out_shape=jax.ShapeDtypeStruct((n, D_MODEL), F32),
        scratch_shapes=[pltpu.VMEM((seq + 2 * POOL_PAD, POOL_WIDTH), F32),
                        pltpu.VMEM((8, 128), F32),
                        pltpu.SMEM((1,), jnp.int32)],
        compiler_params=_params(2),
        name="even_mix",
    )(*args)


def _ffn_kernel(*refs, gla_pre, final):
    refs = list(refs)
    x_ref, mod_ref = refs.pop(0), refs.pop(0)
    if gla_pre:
        og_ref, r_ref, gn_ref, wmix_ref = (refs.pop(0) for _ in range(4))
    g_ref, wi_ref, wo_ref = (refs.pop(0) for _ in range(3))
    fg_ref = refs.pop(0) if final else None
    o_ref, acc_ref = refs
    x = x_ref[...]
    if gla_pre:
        parts = []
        for h in range(GLA_HEADS):
            cols = slice(h * GLA_DV, (h + 1) * GLA_DV)
            oh = og_ref[:, cols]
            oh = oh * lax.rsqrt(jnp.mean(oh * oh, axis=-1, keepdims=True) + EPS)
            parts.append(oh * gn_ref[:, cols])
        z = (jnp.concatenate(parts, axis=1) * _silu(r_ref[...])).astype(BF16)
        x = x + mod_ref[2:3, :] * jnp.dot(z, wmix_ref[...], preferred_element_type=F32)
    h = _norm_mod(x, g_ref[...], mod_ref[3:4, :], mod_ref[4:5, :]).astype(BF16)
    for c in range(FFN_HIDDEN // FFN_CHUNK):
        lo, hi = c * FFN_CHUNK, (c + 1) * FFN_CHUNK
        a = jnp.dot(h, wi_ref[:, lo:hi], preferred_element_type=F32)
        b = jnp.dot(h, wi_ref[:, FFN_HIDDEN + lo:FFN_HIDDEN + hi], preferred_element_type=F32)
        z = (_silu(a) * b).astype(BF16)
        part = jnp.dot(z, wo_ref[lo:hi, :], preferred_element_type=F32)
        if c == 0:
            acc_ref[...] = part
        else:
            acc_ref[...] += part
    y = x + mod_ref[5:6, :] * acc_ref[...]
    if final:
        y = (y * lax.rsqrt(jnp.mean(y * y, axis=-1, keepdims=True) + EPS)) * fg_ref[...]
    o_ref[...] = y


def _ffn(x, mod, layer, group_of_step, g2, wi_bf, wo_bf, gla_pre=None, final_g=None):
    n = x.shape[0]
    tm = FFN_TM
    row = lambda i: (i, 0)
    tile = pl.BlockSpec((tm, D_MODEL), row)
    in_specs = [tile, _mod_spec(layer, group_of_step)]
    args = [x, mod]
    if gla_pre is not None:
        o, r, gn, wmix_bf = gla_pre
        in_specs += [tile, tile, _layer_of(*gn), _layer_of(*wmix_bf)]
        args += [o, r, gn[0], wmix_bf[0]]
    in_specs += [_layer_of(*g2), _layer_of(*wi_bf), _layer_of(*wo_bf)]
    args += [g2[0], wi_bf[0], wo_bf[0]]
    if final_g is not None:
        in_specs.append(_resident((1, D_MODEL)))
        args.append(final_g)
    return pl.pallas_call(
        functools.partial(_ffn_kernel, gla_pre=gla_pre is not None, final=final_g is not None),
        grid=(n // tm,),
        in_specs=in_specs,
        out_specs=tile,
        out_shape=jax.ShapeDtypeStruct((n, D_MODEL), F32),
        scratch_shapes=[pltpu.VMEM((tm, D_MODEL), F32)],
        compiler_params=_params(1),
        name="ffn",
    )(*args)


def _odd_in_kernel(x_ref, mod_ref, g_ref, w_ref, wgd_ref, wup_ref, bg_ref,
                   qk_ref, v_ref, r_ref, gate_ref, *, tm, rc):
    shift, scale, g = mod_ref[0:1, :], mod_ref[1:2, :], g_ref[...]
    for c in range(tm // rc):
        rows = slice(c * rc, (c + 1) * rc)
        h = _norm_mod(x_ref[rows, :], g, shift, scale).astype(BF16)
        gd = jnp.dot(h, wgd_ref[...], preferred_element_type=F32).astype(BF16)
        logit = jnp.dot(gd, wup_ref[...], preferred_element_type=F32) + bg_ref[...]
        qk_ref[rows, :] = jnp.dot(h, w_ref[:, 0:1024], preferred_element_type=F32)
        v_ref[rows, :] = jnp.dot(h, w_ref[:, 1024:2048], preferred_element_type=F32).astype(BF16)
        r_ref[rows, :] = jnp.dot(h, w_ref[:, 2048:3072], preferred_element_type=F32)
        log_sig = jnp.minimum(logit, 0.0) - jnp.log(1.0 + jnp.exp(-jnp.abs(logit)))
        gate_ref[rows, :] = log_sig / GATE_TEMP


def _odd_in(x, mod, layer, group_of_step, g1, w_bf, wgd_bf, wup_bf, bg):
    n = x.shape[0]
    tm, rc = 1024, 512
    row = lambda i: (i, 0)
    return pl.pallas_call(
        functools.partial(_odd_in_kernel, tm=tm, rc=rc),
        grid=(n // tm,),
        in_specs=[pl.BlockSpec((tm, D_MODEL), row), _mod_spec(layer, group_of_step),
                  _layer_of(*g1), _layer_of(*w_bf),
                  _resident((D_MODEL, 128)), _resident((128, D_MODEL)), _layer_of(*bg)],
        out_specs=[pl.BlockSpec((tm, D_MODEL), row)] * 4,
        out_shape=[jax.ShapeDtypeStruct((n, D_MODEL), dt) for dt in (F32, BF16, F32, F32)],
        compiler_params=_params(1),
        name="odd_in",
    )(x, mod, g1[0], w_bf[0], wgd_bf, wup_bf, bg[0])


def _chunk_scan(x, reverse):
    n = x.shape[0]
    pos = lax.broadcasted_iota(jnp.int32, x.shape, 0) % CHUNK
    step = 1
    while step < CHUNK:
        if reverse:
            x = x + jnp.where(pos < CHUNK - step, pltpu.roll(x, n - step, axis=0), 0.0)
        else:
            x = x + jnp.where(pos >= step, pltpu.roll(x, step, axis=0), 0.0)
        step *= 2
    return x


def _gla_kernel(*refs, tb, nsb, has_init, state_slot, has_earlier):
    refs = list(refs)
    ins = [refs.pop(0) for _ in range(6)]
    init_ref = refs.pop(0) if has_init else None
    if has_earlier:
        refs.pop(0)
    o_ref = refs.pop(0)
    state_ref = refs.pop(0) if state_slot is not None else None
    st = refs.pop(0)
    n = pl.program_id(1)

    @pl.when(n == 0)
    def _init():
        o_ref[...] = jnp.zeros(o_ref.shape, F32)
        if has_init:
            st[...] = init_ref[...]
        else:
            st[...] = jnp.zeros(st.shape, F32)

    ri = lax.broadcasted_iota(jnp.int32, (GLA_STEP, GLA_STEP), 0)
    ci = lax.broadcasted_iota(jnp.int32, (GLA_STEP, GLA_STEP), 1)
    first = lax.broadcasted_iota(jnp.int32, (GLA_STEP, 512), 0) < CHUNK
    nt = (((1,), (1,)), ((), ()))
    nstep = tb // GLA_STEP

    def body(i, carry):
        for d in range(2):
            qk_ref, v_ref, g_ref = ins[3 * d:3 * d + 3]
            c = i if d == 0 else nstep - 1 - i
            block = n if d == 0 else nsb - 1 - n
            rows = pl.ds(pl.multiple_of(c * GLA_STEP, GLA_STEP), GLA_STEP)
            orows = pl.ds(pl.multiple_of(block * tb + c * GLA_STEP, GLA_STEP), GLA_STEP)
            p = _chunk_scan(g_ref[rows, :], reverse=d == 1)
            if d == 0:
                mask = ci <= ri
                tot_in, tot_out = p[CHUNK - 1:CHUNK, :], p[GLA_STEP - 1:GLA_STEP, :]
                b_mid = jnp.where(first, p - tot_in, p)
            else:
                mask = ci >= ri
                tot_out, tot_in = p[0:1, :], p[CHUNK:CHUNK + 1, :]
                b_mid = jnp.where(first, p, p - tot_in)
            tot = tot_in + tot_out
            q = qk_ref[rows, 0:512] * (GLA_DK ** -0.5)
            k = qk_ref[rows, 512:1024]
            q_mid = q * jnp.exp(b_mid)
            k_mid = k * jnp.exp(-b_mid)
            q_t = q_mid.astype(BF16)
            k_t = k_mid.astype(BF16)
            q_in = (q_mid * jnp.exp(tot_in)).astype(BF16)
            k_end = k_mid * jnp.exp(tot_out)
            decay = jnp.exp(jnp.broadcast_to(tot, (8, 512)).T[:, 0:1])
            for h in range(GLA_HEADS):
                kc = slice(h * GLA_DK, (h + 1) * GLA_DK)
                vc = slice(h * GLA_DV, (h + 1) * GLA_DV)
                vh = v_ref[rows, vc]
                a = lax.dot_general(q_t[:, kc], k_t[:, kc], nt, preferred_element_type=F32)
                a = jnp.where(mask, a, 0.0).astype(BF16)
                lhs = jnp.concatenate([a, k_end[:, kc].T.astype(BF16)], axis=0)
                both = jnp.dot(lhs, vh, preferred_element_type=F32)
                s = st[d, h]
                o = both[:GLA_STEP] + jnp.dot(q_in[:, kc], s.astype(BF16), preferred_element_type=F32)
                o_ref[orows, vc] += o
                st[d, h] = s * decay[kc, :] + both[GLA_STEP:]
        return carry

    lax.fori_loop(0, nstep, body, 0, unroll=True)

    if state_slot is not None:
        @pl.when(n == pl.num_programs(1) - 1)
        def _emit():
            for s in range(state_ref.shape[0]):
                state_ref[s] = st[...] if s == state_slot else jnp.zeros(st.shape, F32)


def _gla(qk, v, gate, init, seq, emit_state):
    n = qk.shape[0]
    nseq = n // seq
    tb = min(seq, 512)
    nsb = seq // tb
    fwd = lambda b, s: (b * nsb + s, 0)
    bwd = lambda b, s: (b * nsb + nsb - 1 - s, 0)
    bwd_gate = lambda b, s: (b * nsb + nsb - 1 - s, 1)
    in_specs = [pl.BlockSpec((tb, 1024), fwd), pl.BlockSpec((tb, 1024), fwd), pl.BlockSpec((tb, 512), fwd),
                pl.BlockSpec((tb, 1024), bwd), pl.BlockSpec((tb, 1024), bwd), pl.BlockSpec((tb, 512), bwd_gate)]
    args = [qk, v, gate, qk, v, gate]
    if init is not None:
        state, il = init
        in_specs.append(pl.BlockSpec((None, None, 2, GLA_HEADS, GLA_DK, GLA_DV),
                                     lambda b, s: (b, il, 0, 0, 0, 0)))
        args.append(state)
    out_specs = [pl.BlockSpec((seq, 1024), lambda b, s: (b, 0))]
    out_shape = [jax.ShapeDtypeStruct((n, 1024), F32)]
    aliases = {}
    if emit_state is not None:
        slot, n_slots, earlier = emit_state
        out_shape.append(jax.ShapeDtypeStruct((nseq, n_slots, 2, GLA_HEADS, GLA_DK, GLA_DV), F32))
        if earlier is None:
            out_specs.append(pl.BlockSpec((None, n_slots, 2, GLA_HEADS, GLA_DK, GLA_DV),
                                          lambda b, s: (b, 0, 0, 0, 0, 0)))
        else:
            out_specs.append(pl.BlockSpec((None, 1, 2, GLA_HEADS, GLA_DK, GLA_DV),
                                          lambda b, s: (b, slot, 0, 0, 0, 0)))
            in_specs.append(pl.BlockSpec(memory_space=pl.ANY))
            args.append(earlier)
            aliases = {len(args) - 1: 1}
    return pl.pallas_call(
        functools.partial(_gla_kernel, tb=tb, nsb=nsb, has_init=init is not None,
                          state_slot=None if emit_state is None else (0 if aliases else emit_state[0]),
                          has_earlier=bool(aliases)),
        grid=(nseq, nsb),
        in_specs=in_specs,
        out_specs=out_specs,
        out_shape=out_shape,
        input_output_aliases=aliases,
        scratch_shapes=[pltpu.VMEM((2, GLA_HEADS, GLA_DK, GLA_DV), F32)],
        compiler_params=_params(2),
        name="gla",
    )(*args)


def _rope_tables(n_lat):
    half = ATT_DH // 2
    inv = 1.0 / (ROPE_BASE ** (jnp.arange(0, half, 2, dtype=F32) / half))
    t = jnp.arange(n_lat)
    rows = (t // GRID_W).astype(F32)
    cols = (t % GRID_W).astype(F32)

    def tab(pos):
        ang = pos[:, None] * inv[None, :]
        ang = jnp.concatenate([ang, ang], axis=-1)
        return jnp.cos(ang), jnp.sin(ang)

    (cr, sr), (cc, sc) = tab(rows), tab(cols)
    sign = jnp.concatenate([-jnp.ones((half // 2,), F32), jnp.ones((half // 2,), F32)])
    cos = jnp.concatenate([cr, cc], axis=-1)
    sin = jnp.concatenate([sr * sign, sc * sign], axis=-1)
    return jnp.tile(cos, (1, 2)), jnp.tile(sin, (1, 2))


def kernel(x_prompt, x_sample, c, cache_attn_k, cache_attn_v, state_gla, c_ctx, norm1_g, norm2_g, w_mod, b_mod, w_in_even, lam_params, subln_g, w_pool, pool_scale, w_out_even, w_in_odd, w_gate_up, b_gate, gla_norm_g, w_out_odd, w_ffn_in, w_ffn_out, final_g):
    bp, tp, _ = x_prompt.shape
    bs, ts, _ = x_sample.shape
    past = cache_attn_k.shape[2]
    n_even = cache_attn_k.shape[1]
    assert bs + 1 <= MOD_ROWS

    xp = x_prompt.reshape(bp * tp, D_MODEL)
    xs = x_sample.reshape(bs * ts, D_MODEL)
    ck = cache_attn_k.reshape(bs, n_even, past, ATT_WIDTH)
    cv = cache_attn_v.reshape(bs, n_even, past, ATT_WIDTH)

    cvec = jnp.concatenate([c_ctx[None, :], c, jnp.zeros((MOD_ROWS - 1 - bs, D_MODEL), F32)], axis=0)
    mod = _adaln(cvec, w_mod, b_mod)
    ctx_group = lambda i: 0

    def lat_group(tm):
        return lambda i: 1 + (i * tm) // ts

    rope = _rope_tables(ts)

    row3 = lambda a: a.reshape(a.shape[0], 1, a.shape[1])
    n1, n2 = row3(norm1_g), row3(norm2_g)
    w_in_even_bf, w_out_even_bf, w_pool_bf = (w.astype(BF16) for w in (w_in_even, w_out_even, w_pool))
    w_in_odd_bf, w_out_odd_bf = w_in_odd.astype(BF16), w_out_odd.astype(BF16)
    w_ffn_in_bf, w_ffn_out_bf = w_ffn_in.astype(BF16), w_ffn_out.astype(BF16)
    subg3, pscale3, gn3 = row3(subln_g), row3(pool_scale), row3(gla_norm_g)
    bg3 = b_gate.reshape(b_gate.shape[0], 1, D_MODEL)

    fg = final_g.reshape(1, D_MODEL)
    k_new = v_new = states = None
    for l in range(DEPTH):
        i = l // 2
        g1, g2 = (n1, l), (n2, l)
        last = fg if l == DEPTH - 1 else None
        pre_p = pre_s = None
        if l % 2 == 0:
            lam_init = 0.8 - 0.6 * math.exp(-0.3 * l)
            w_in, w_out, wpool = (w_in_even_bf, i), (w_out_even_bf, i), (w_pool_bf, i)
            subg, pscale, lamp = (subg3, i), (pscale3, i), (lam_params, i)
            q, k_new, v_new, u = _even_in(xp, mod, l, ctx_group, g1, w_in, None, tp, F32,
                                          cache_out=(i, n_even, k_new, v_new))
            xp = _even_mix(xp, q, k_new, v_new, u, None, mod, l, ctx_group, lamp, subg, wpool, pscale,
                           w_out, tp, lam_init, kv_slot=i)
            q, k, v, u = _even_in(xs, mod, l, lat_group(1024), g1, w_in, rope, ts, BF16)
            xs = _even_mix(xs, q, k, v, u, (ck, cv, i), mod, l, lambda b: 1 + b, lamp, subg, wpool,
                           pscale, w_out, ts, lam_init)
        else:
            w_main, w_out, gn, bg = (w_in_odd_bf, i), (w_out_odd_bf, i), (gn3, i), (bg3, i)
            wgd = jnp.pad(w_in_odd[i][:, 3072:], ((0, 0), (0, 128 - 2 * GATE_RANK))).astype(BF16)
            wup = jnp.zeros((128, D_MODEL), F32)
            wup = wup.at[0:GATE_RANK, 0:512].set(w_gate_up[i, 0])
            wup = wup.at[GATE_RANK:2 * GATE_RANK, 512:1024].set(w_gate_up[i, 1]).astype(BF16)
            qk, v, r, gate = _odd_in(xp, mod, l, ctx_group, g1, w_main, wgd, wup, bg)
            o, states = _gla(qk, v, gate, None, tp, (i, w_in_odd.shape[0], states))
            pre_p = (o, r, gn, w_out)
            qk, v, r, gate = _odd_in(xs, mod, l, lat_group(1024), g1, w_main, wgd, wup, bg)
            (o,) = _gla(qk, v, gate, (state_gla, i), ts, None)
            pre_s = (o, r, gn, w_out)
        wi, wo = (w_ffn_in_bf, l), (w_ffn_out_bf, l)
        xp = _ffn(xp, mod, l, ctx_group, g2, wi, wo, pre_p, last)
        xs = _ffn(xs, mod, l, lat_group(FFN_TM), g2, wi, wo, pre_s, last)

    y_prompt = xp.reshape(bp, tp, D_MODEL)
    y_sample = xs.reshape(bs, ts, D_MODEL)
    new_attn_k = k_new.reshape(bp, n_even, tp, ATT_HEADS, 2, ATT_DH)
    new_attn_v = v_new.reshape(bp, n_even, tp, ATT_HEADS, ATT_DV)
    return (y_prompt, y_sample, new_attn_k, new_attn_v, states)
```
